```python
import jax
import jax.numpy as jnp
from jax import lax
import numpy as np

D_MODEL = 4096
BATCH = 1
SEQ = 16384
DEPTH = 2
DEC_BATCH = 32
DEC_SEQ = 16
PAST_LEN = 4096

CHUNK = 64
NORM_EPS = 1e-6
PLE_DIM = 256

R_HEAD_DIM = 64
R_WIDTH = D_MODEL // 2
R_HEADS = R_WIDTH // R_HEAD_DIM
R_LORA_DECAY = 128
R_LORA_AAA = 128
R_LORA_GATE = 480
R_GN_EPS = 64e-5

G_HEADS = 8
G_VAL_WIDTH = D_MODEL // 2
G_KEY_WIDTH = D_MODEL // 4
G_KEY_DIM = G_KEY_WIDTH // G_HEADS
G_VAL_DIM = G_VAL_WIDTH // G_HEADS
G_GATE_RANK = 16
G_GATE_TAU = 16.0

R_COLS = 3 * R_WIDTH + R_LORA_DECAY + R_LORA_AAA + R_LORA_GATE
G_COLS = 2 * G_KEY_WIDTH + 2 * G_VAL_WIDTH + G_GATE_RANK
N_BRANCH = 2
GATE_COLS = N_BRANCH * D_MODEL
IN_COLS = R_COLS + G_COLS + GATE_COLS

D_FF = ((8 * D_MODEL + 3 * 256 - 1) // (3 * 256)) * 256

kernel_name = "rwkv7_gla_gated_hybrid_stream_step"


def rms_norm(x, gain, eps=NORM_EPS):
    xf = x.astype(jnp.float32)
    y = xf * lax.rsqrt(jnp.mean(xf * xf, axis=-1, keepdims=True) + eps)
    return (y * gain.astype(jnp.float32)).astype(x.dtype)


def head_layer_norm(x, w, b, eps):
    mu = jnp.mean(x, axis=-1, keepdims=True)
    var = jnp.mean(jnp.square(x - mu), axis=-1, keepdims=True)
    y = ((x - mu) * lax.rsqrt(var + eps)).reshape(*x.shape[:-2], -1)
    return y * w.astype(jnp.float32) + b.astype(jnp.float32)


def head_rms_norm(x, w, eps=NORM_EPS):
    y = x * lax.rsqrt(jnp.mean(x * x, axis=-1, keepdims=True) + eps)
    return y.reshape(*x.shape[:-2], -1) * w.astype(jnp.float32)


def rwkv7_recurrence(state0, r, decay, k, v, kk, a):
    def step(S, inp):
        r_t, w_t, k_t, v_t, kk_t, a_t = inp
        sa = jnp.einsum('bhvk,bhk->bhv', S, -kk_t)
        S = (S * w_t[:, :, None, :]
             + sa[..., None] * (kk_t * a_t)[:, :, None, :]
             + v_t[..., None] * k_t[:, :, None, :])
        return S, jnp.einsum('bhvk,bhk->bhv', S, r_t)
    xs = tuple(jnp.moveaxis(t, 1, 0) for t in (r, decay, k, v, kk, a))
    S, ys = lax.scan(step, state0, xs)
    return jnp.moveaxis(ys, 0, 1), S


def gla_chunk(S, q, k, v, log_a):
    C = q.shape[2]
    b = jnp.cumsum(log_a, axis=2)
    causal = jnp.tril(jnp.ones((C, C), dtype=bool))
    diff = b[:, :, :, None, :] - b[:, :, None, :, :]
    dec = jnp.exp(jnp.where(causal[:, :, None], diff, -jnp.inf))
    scores = jnp.einsum('bhtd,bhsd,bhtsd->bhts', q, k, dec)
    o = (jnp.einsum('bhts,bhsv->bhtv', scores, v)
         + jnp.einsum('bhtd,bhdv->bhtv', q * jnp.exp(b), S))
    b_last = b[:, :, -1:, :]
    S_new = (S * jnp.exp(b_last[:, :, 0, :])[..., None]
             + jnp.einsum('bhsd,bhsv->bhdv', k * jnp.exp(b_last - b), v))
    return o, S_new


def gla_recurrence(state0, q, k, v, log_a):
    B, T, H, _ = q.shape
    C = min(CHUNK, T)
    nc = T // C

    def to_chunks(t):
        return t.reshape(B, nc, C, H, t.shape[-1]).transpose(1, 0, 3, 2, 4)

    def step(S, inp):
        o, S = gla_chunk(S, *inp)
        return S, o

    S, o = lax.scan(step, state0, tuple(to_chunks(t) for t in (q, k, v, log_a)))
    o = o.transpose(1, 0, 3, 2, 4).reshape(B, T, H, -1)
    return o, S


def token_mixers(z, shift_prev, s_rwkv, s_gla, lp):
    f32 = jnp.float32
    B, T, _ = z.shape
    proj = z @ lp['w_in']
    pr, pg, pgate = jnp.split(proj, [R_COLS, R_COLS + G_COLS], axis=-1)

    prev = jnp.concatenate([shift_prev[:, None, :].astype(pr.dtype), pr[:, :-1]], axis=1)
    pr_mix = pr + (prev - pr) * lp['rwkv_mu']
    new_shift = pr[:, -1]
    r, k, v, wd, ad, gd = jnp.split(
        pr_mix, [R_WIDTH, 2 * R_WIDTH, 3 * R_WIDTH, 3 * R_WIDTH + R_LORA_DECAY,
                 3 * R_WIDTH + R_LORA_DECAY + R_LORA_AAA], axis=-1)
    w = -jax.nn.softplus(-(lp['rwkv_w0'] + jnp.tanh(wd) @ lp['rwkv_w2']).astype(f32)) - 0.5
    decay = jnp.exp(-jnp.exp(w))
    a = jax.nn.sigmoid((lp['rwkv_a0'] + ad @ lp['rwkv_a2']).astype(f32))
    g = (jax.nn.sigmoid(gd) @ lp['rwkv_g2']).astype(f32)

    def hd(t):
        return t.astype(f32).reshape(*t.shape[:-1], R_HEADS, R_HEAD_DIM)

    r_h, k_h, v_h, decay_h, a_h = hd(r), hd(k), hd(v), hd(decay), hd(a)
    kk = k_h * hd(lp['rwkv_k_k'])
    kk = kk / jnp.maximum(jnp.sqrt(jnp.sum(kk * kk, axis=-1, keepdims=True)), 1e-12)
    k_h = k_h * (1.0 + (a_h - 1.0) * hd(lp['rwkv_k_a']))
    y_r, s_rwkv_new = rwkv7_recurrence(s_rwkv.astype(f32), r_h, decay_h, k_h, v_h, kk, a_h)
    y_r = head_layer_norm(y_r, lp['rwkv_ln_w'], lp['rwkv_ln_b'], R_GN_EPS)
    bonus = jnp.sum(r_h * k_h * lp['rwkv_r_k'].astype(f32), axis=-1, keepdims=True) * v_h
    o_r = ((y_r + bonus.reshape(B, T, R_WIDTH)) * g).astype(z.dtype)

    q_g, k_g, v_g, rg, ag = jnp.split(
        pg, [G_KEY_WIDTH, 2 * G_KEY_WIDTH, 2 * G_KEY_WIDTH + G_VAL_WIDTH,
             2 * G_KEY_WIDTH + 2 * G_VAL_WIDTH], axis=-1)
    log_a = jax.nn.log_sigmoid((ag @ lp['gla_a2'] + lp['gla_ab']).astype(f32)) / G_GATE_TAU
    gk = lambda t: t.astype(f32).reshape(B, T, G_HEADS, G_KEY_DIM)
    o_g, s_gla_new = gla_recurrence(
        s_gla.astype(f32), gk(q_g) * (G_KEY_DIM ** -0.5), gk(k_g),
        v_g.astype(f32).reshape(B, T, G_HEADS, G_VAL_DIM), gk(log_a))
    o_g = (head_rms_norm(o_g, lp['gla_norm']) * jax.nn.silu(rg.astype(f32))).astype(z.dtype)

    gate_r, gate_g = jnp.split(pgate, [D_MODEL], axis=-1)
    merged = (jax.nn.sigmoid(gate_r) * (o_r @ lp['w_proj_rwkv'])
              + jax.nn.sigmoid(gate_g) * (o_g @ lp['w_proj_gla']))
    out = merged @ lp['w_out']
    return (out, new_shift.astype(z.dtype), s_rwkv_new.astype(z.dtype), s_gla_new.astype(z.dtype))


def trunk_layer(h, p_i, shift_prev, s_rwkv, s_gla, lp):
    z = rms_norm(h, lp['norm_mix_pre'])
    mix, new_shift, new_rwkv, new_gla = token_mixers(z, shift_prev, s_rwkv, s_gla, lp)
    h = h + rms_norm(mix, lp['norm_mix_post'])
    u = rms_norm(h, lp['norm_ffn_pre'])
    f = (jax.nn.silu(u @ lp['ffn_gate']) * (u @ lp['ffn_up'])) @ lp['ffn_down']
    h = h + rms_norm(f, lp['norm_ffn_post'])
    h = h + jax.nn.sigmoid(rms_norm(h, lp['norm_ple']) @ lp['ple_gate']) * (p_i @ lp['ple_proj'])
    return h, new_shift, new_rwkv, new_gla


def setup_inputs(seed: int = 0) -> dict:
    key = jax.random.key(seed)
    ks = jax.random.split(key, 40)
    f32 = jnp.float32

    def nrm(k, shape, scale=1.0, shift=0.0):
        return shift + scale * jax.random.normal(k, shape, dtype=f32)

    L = DEPTH
    return {
        'x_prompt': nrm(ks[0], (BATCH, SEQ, D_MODEL)),
        'x_sample': nrm(ks[1], (DEC_BATCH, DEC_SEQ, D_MODEL)),
        'state_shift': nrm(ks[2], (L, DEC_BATCH, R_COLS)),
        'state_rwkv': nrm(ks[3], (L, DEC_BATCH, R_HEADS, R_HEAD_DIM, R_HEAD_DIM), 0.5),
        'state_gla': nrm(ks[4], (L, DEC_BATCH, G_HEADS, G_KEY_DIM, G_VAL_DIM), 0.5),
        'p_prompt': nrm(ks[5], (L, BATCH, SEQ, PLE_DIM)),
        'p_sample': nrm(ks[6], (L, DEC_BATCH, DEC_SEQ, PLE_DIM)),
        'norm_mix_pre': nrm(ks[7], (L, D_MODEL), 0.05, 1.0),
        'norm_mix_post': nrm(ks[8], (L, D_MODEL), 0.05, 1.0),
        'norm_ffn_pre': nrm(ks[9], (L, D_MODEL), 0.05, 1.0),
        'norm_ffn_post': nrm(ks[10], (L, D_MODEL), 0.05, 1.0),
        'norm_ple': nrm(ks[11], (L, D_MODEL), 0.05, 1.0),
        'w_in': nrm(ks[12], (L, D_MODEL, IN_COLS), D_MODEL ** -0.5),
        'rwkv_mu': jax.random.uniform(ks[13], (L, R_COLS), dtype=f32),
        'rwkv_w0': nrm(ks[14], (L, R_WIDTH), 0.5, -1.0),
        'rwkv_w2': nrm(ks[15], (L, R_LORA_DECAY, R_WIDTH), 0.5 * R_LORA_DECAY ** -0.5),
        'rwkv_a0': nrm(ks[16], (L, R_WIDTH), 0.5),
        'rwkv_a2': nrm(ks[17], (L, R_LORA_AAA, R_WIDTH), R_LORA_AAA ** -0.5),
        'rwkv_g2': nrm(ks[18], (L, R_LORA_GATE, R_WIDTH), R_LORA_GATE ** -0.5),
        'rwkv_k_k': nrm(ks[19], (L, R_WIDTH), 0.05, 0.85),
        'rwkv_k_a': nrm(ks[20], (L, R_WIDTH), 0.05, 1.0),
        'rwkv_r_k': nrm(ks[21], (L, R_HEADS, R_HEAD_DIM), 0.1),
        'rwkv_ln_w': nrm(ks[22], (L, R_WIDTH), 0.05, 1.0),
        'rwkv_ln_b': nrm(ks[23], (L, R_WIDTH), 0.01),
        'gla_a2': nrm(ks[24], (L, G_GATE_RANK, G_KEY_WIDTH), G_GATE_RANK ** -0.5),
        'gla_ab': nrm(ks[25], (L, G_KEY_WIDTH), 0.01),
        'gla_norm': nrm(ks[26], (L, G_VAL_WIDTH), 0.05, 1.0),
        'w_proj_rwkv': nrm(ks[27], (L, R_WIDTH, D_MODEL), R_WIDTH ** -0.5),
        'w_proj_gla': nrm(ks[28], (L, G_VAL_WIDTH, D_MODEL), G_VAL_WIDTH ** -0.5),
        'w_out': nrm(ks[29], (L, D_MODEL, D_MODEL), D_MODEL ** -0.5),
        'ffn_gate': nrm(ks[30], (L, D_MODEL, D_FF), D_MODEL ** -0.5),
        'ffn_up': nrm(ks[31], (L, D_MODEL, D_FF), D_MODEL ** -0.5),
        'ffn_down': nrm(ks[32], (L, D_FF, D_MODEL), D_FF ** -0.5),
        'ple_proj': nrm(ks[33], (L, PLE_DIM, D_MODEL), PLE_DIM ** -0.5),
        'ple_gate': nrm(ks[34], (L, D_MODEL, D_MODEL), D_MODEL ** -0.5),
    }


def reference(x_prompt, x_sample, state_shift, state_rwkv, state_gla, p_prompt, p_sample,
              norm_mix_pre, norm_mix_post, norm_ffn_pre, norm_ffn_post, norm_ple,
              w_in, rwkv_mu, rwkv_w0, rwkv_w2, rwkv_a0, rwkv_a2, rwkv_g2, rwkv_k_k, rwkv_k_a,
              rwkv_r_k, rwkv_ln_w, rwkv_ln_b, gla_a2, gla_ab, gla_norm,
              w_proj_rwkv, w_proj_gla, w_out, ffn_gate, ffn_up, ffn_down, ple_proj, ple_gate):
    B = x_prompt.shape[0]
    dt = x_prompt.dtype
    h_p = x_prompt
    h_s = x_sample
    shift_p, rwkv_p, gla_p = [], [], []
    shift_s, rwkv_s, gla_s = [], [], []
    for i in range(DEPTH):
        lp = {
            'norm_mix_pre': norm_mix_pre[i], 'norm_mix_post': norm_mix_post[i],
            'norm_ffn_pre': norm_ffn_pre[i], 'norm_ffn_post': norm_ffn_post[i],
            'norm_ple': norm_ple[i], 'w_in': w_in[i], 'rwkv_mu': rwkv_mu[i],
            'rwkv_w0': rwkv_w0[i], 'rwkv_w2': rwkv_w2[i], 'rwkv_a0': rwkv_a0[i],
            'rwkv_a2': rwkv_a2[i], 'rwkv_g2': rwkv_g2[i], 'rwkv_k_k': rwkv_k_k[i],
            'rwkv_k_a': rwkv_k_a[i], 'rwkv_r_k': rwkv_r_k[i], 'rwkv_ln_w': rwkv_ln_w[i],
            'rwkv_ln_b': rwkv_ln_b[i], 'gla_a2': gla_a2[i], 'gla_ab': gla_ab[i],
            'gla_norm': gla_norm[i], 'w_proj_rwkv': w_proj_rwkv[i], 'w_proj_gla': w_proj_gla[i],
            'w_out': w_out[i], 'ffn_gate': ffn_gate[i], 'ffn_up': ffn_up[i],
            'ffn_down': ffn_down[i], 'ple_proj': ple_proj[i], 'ple_gate': ple_gate[i],
        }
        h_p, ns, nr, ng = trunk_layer(
            h_p, p_prompt[i],
            jnp.zeros((B, R_COLS), dt),
            jnp.zeros((B, R_HEADS, R_HEAD_DIM, R_HEAD_DIM), dt),
            jnp.zeros((B, G_HEADS, G_KEY_DIM, G_VAL_DIM), dt), lp)
        shift_p.append(ns)
        rwkv_p.append(nr)
        gla_p.append(ng)
        h_s, ns, nr, ng = trunk_layer(h_s, p_sample[i], state_shift[i], state_rwkv[i], state_gla[i], lp)
        shift_s.append(ns)
        rwkv_s.append(nr)
        gla_s.append(ng)
    y_prompt = h_p
    y_sample = h_s
    new_shift_prompt = jnp.stack(shift_p, axis=0)
    new_rwkv_prompt = jnp.stack(rwkv_p, axis=0)
    new_gla_prompt = jnp.stack(gla_p, axis=0)
    new_shift_sample = jnp.stack(shift_s, axis=0)
    new_rwkv_sample = jnp.stack(rwkv_s, axis=0)
    new_gla_sample = jnp.stack(gla_s, axis=0)
    return (y_prompt, y_sample, new_shift_prompt, new_rwkv_prompt, new_gla_prompt,
            new_shift_sample, new_rwkv_sample, new_gla_sample)
```

```python
import functools

import jax
import jax.numpy as jnp
from jax import lax
from jax.experimental import pallas as pl
from jax.experimental.pallas import tpu as pltpu

F32 = jnp.float32
BF16 = jnp.bfloat16
HIGHEST = lax.Precision.HIGHEST

NORM_EPS = 1e-6
R_GN_EPS = 64e-5
R_HEAD_DIM = 64
G_HEADS = 8
G_GATE_TAU = 16.0

LANES = 128
SUBLANES = 8
VMEM_LIMIT = 56 * 1024 * 1024

ROW_TILE = 1536
COL_TILE = 512
K_TILE = 1024
EW_ROWS = 256
SEQ_CHUNK = 64
GLA_SUB = 16
SMALL_W = 1024


def _round_up(x, m):
    return (x + m - 1) // m * m


def _pick_tile(n, target, mult=SUBLANES):
    best = None
    for t in range(mult, min(n, target) + 1, mult):
        if n % t == 0:
            best = t
    assert best is not None, (n, target)
    return best


def _params(sem):
    return pltpu.CompilerParams(dimension_semantics=sem, vmem_limit_bytes=VMEM_LIMIT)


def _rms(x, gain):
    ms = jnp.mean(x * x, axis=-1, keepdims=True)
    return x * lax.rsqrt(ms + NORM_EPS) * gain


def _softplus(x):
    return jnp.maximum(x, 0.0) + jnp.log1p(jnp.exp(-jnp.abs(x)))


def _norm_cast_body(x_ref, g_ref, o_ref):
    o_ref[...] = _rms(x_ref[...], g_ref[...]).astype(o_ref.dtype)


def norm_cast(x, gain):
    M, D = x.shape
    tm = _pick_tile(M, EW_ROWS)
    return pl.pallas_call(
        _norm_cast_body,
        grid=(M // tm,),
        in_specs=[pl.BlockSpec((tm, D), lambda i: (i, 0)),
                  pl.BlockSpec((1, D), lambda i: (0, 0))],
        out_specs=pl.BlockSpec((tm, D), lambda i: (i, 0)),
        out_shape=jax.ShapeDtypeStruct((M, D), BF16),
        compiler_params=_params(("parallel",)),
        name="norm_cast",
    )(x, gain.reshape(1, D))


def _resid_norm_body(h_ref, x_ref, gpost_ref, gpre_ref, h1_ref, u_ref):
    h1 = h_ref[...] + _rms(x_ref[...], gpost_ref[...])
    h1_ref[...] = h1
    u_ref[...] = _rms(h1, gpre_ref[...]).astype(u_ref.dtype)


def resid_norm(h, x, g_post, g_pre):
    M, D = h.shape
    tm = _pick_tile(M, EW_ROWS)
    row = pl.BlockSpec((tm, D), lambda i: (i, 0))
    vec = pl.BlockSpec((1, D), lambda i: (0, 0))
    return pl.pallas_call(
        _resid_norm_body,
        grid=(M // tm,),
        in_specs=[row, row, vec, vec],
        out_specs=[row, row],
        out_shape=[jax.ShapeDtypeStruct((M, D), F32), jax.ShapeDtypeStruct((M, D), BF16)],
        compiler_params=_params(("parallel",)),
        name="resid_norm",
    )(h, x, g_post.reshape(1, D), g_pre.reshape(1, D))


def _mm_body(*refs, act_of, n_acts, n_extra, nk, epilogue):
    n_w = len(act_of)
    a_refs = refs[:n_acts]
    w_refs = refs[n_acts:n_acts + n_w]
    e_refs = refs[n_acts + n_w:n_acts + n_w + n_extra]
    o_ref = refs[n_acts + n_w + n_extra]
    acc_refs = refs[n_acts + n_w + n_extra + 1:]
    acts = [a[...] for a in a_refs]
    dots = [jnp.dot(acts[ai], w[...], preferred_element_type=F32) for ai, w in zip(act_of, w_refs)]
    if nk == 1:
        o_ref[...] = epilogue(dots, [e[...] for e in e_refs]).astype(o_ref.dtype)
        return
    k = pl.program_id(2)

    @pl.when(k == 0)
    def _():
        for acc, d in zip(acc_refs, dots):
            acc[...] = d

    @pl.when(k > 0)
    def _():
        for acc, d in zip(acc_refs, dots):
            acc[...] += d

    @pl.when(k == nk - 1)
    def _():
        o_ref[...] = epilogue([acc[...] for acc in acc_refs], [e[...] for e in e_refs]).astype(o_ref.dtype)


def fused_mm(acts, weights, extras, epilogue, n_out, out_dtype, name, tk=None, tm_target=None):
    M, K = acts[0].shape
    tm = _pick_tile(M, ROW_TILE if tm_target is None else min(tm_target, ROW_TILE))
    tn = _pick_tile(n_out, COL_TILE, LANES)
    tk = K if tk is None else tk
    nk = K // tk
    assert K % tk == 0
    grid = (M // tm, n_out // tn, nk)
    in_specs = ([pl.BlockSpec((tm, tk), lambda i, j, k: (i, k)) for _ in acts]
                + [pl.BlockSpec((tk, tn), lambda i, j, k: (k, j)) for _ in weights]
                + [pl.BlockSpec((tm, tn), functools.partial(lambda i, j, k, off: (i, j + off), off=off))
                   for _, off in extras])
    scratch = [pltpu.VMEM((tm, tn), F32) for _ in weights] if nk > 1 else []
    body = functools.partial(_mm_body, act_of=tuple(a for a, _ in weights), n_acts=len(acts),
                             n_extra=len(extras), nk=nk, epilogue=epilogue)
    return pl.pallas_call(
        body,
        grid=grid,
        in_specs=in_specs,
        out_specs=pl.BlockSpec((tm, tn), lambda i, j, k: (i, j)),
        out_shape=jax.ShapeDtypeStruct((M, n_out), out_dtype),
        scratch_shapes=scratch,
        compiler_params=_params(("parallel", "parallel", "arbitrary")),
        name=name,
    )(*acts, *[w for _, w in weights], *[e for e, _ in extras])


def _ep_identity(dots, extras):
    return dots[0]


def _ep_merge(dots, extras):
    return jax.nn.sigmoid(extras[0]) * dots[0] + jax.nn.sigmoid(extras[1]) * dots[1]


def _ep_swiglu(dots, extras):
    return jax.nn.silu(dots[0]) * dots[1]


def _ep_ple(dots, extras):
    return extras[0] + jax.nn.sigmoid(dots[0]) * dots[1]


def _ple_body(pn_ref, wg_ref, p_ref, wp_ref, h_ref, o_ref):
    gate = jnp.dot(pn_ref[...], wg_ref[...], preferred_element_type=F32)
    proj = jnp.dot(p_ref[...], wp_ref[...], preferred_element_type=F32)
    o_ref[...] = h_ref[...] + jax.nn.sigmoid(gate) * proj


def ple_mm(pn, w_gate, p, w_proj, h):
    M, D = pn.shape
    P = p.shape[1]
    tm = _pick_tile(M, ROW_TILE)
    tn = _pick_tile(D, COL_TILE, LANES)
    return pl.pallas_call(
        _ple_body,
        grid=(M // tm, D // tn),
        in_specs=[pl.BlockSpec((tm, D), lambda i, j: (i, 0)),
                  pl.BlockSpec((D, tn), lambda i, j: (0, j)),
                  pl.BlockSpec((tm, P), lambda i, j: (i, 0)),
                  pl.BlockSpec((P, tn), lambda i, j: (0, j)),
                  pl.BlockSpec((tm, tn), lambda i, j: (i, j))],
        out_specs=pl.BlockSpec((tm, tn), lambda i, j: (i, j)),
        out_shape=jax.ShapeDtypeStruct((M, D), F32),
        compiler_params=_params(("parallel", "parallel")),
        name="ple_mm",
    )(pn, w_gate, p, w_proj, h)


def _shift_mix(x, prev_first, mu):
    rolled = pltpu.roll(x, 1, axis=0)
    row = lax.broadcasted_iota(jnp.int32, x.shape, 0)
    prev = jnp.where(row == 0, prev_first, rolled)
    return x + (prev - x) * mu


def _rwkv_prep_body(r_ref, k_ref, v_ref, sm_ref, rp_ref, kp_ref, vp_ref, smp_ref,
                    rs_ref, ks_ref, vs_ref, sms_ref,
                    mur_ref, muk_ref, muv_ref, musm_ref,
                    w0_ref, a0_ref, kkw_ref, kaw_ref, rkw_ref,
                    w2_ref, a2_ref, g2_ref, rep_ref,
                    nkk_ref, wr_ref, dec_ref, kka_ref, kmod_ref, c_ref, vmix_ref, g_ref,
                    rrep_ref, krep_ref, *, ld, la, lgp, n_kblk):
    j = pl.program_id(1)
    first = j == 0

    def prev_row(p_ref, s_ref):
        return jnp.where(first, s_ref[...], p_ref[SUBLANES - 1:SUBLANES, :])

    r = _shift_mix(r_ref[...], prev_row(rp_ref, rs_ref), mur_ref[...])
    k = _shift_mix(k_ref[...], prev_row(kp_ref, ks_ref), muk_ref[...])
    vmix_ref[...] = _shift_mix(v_ref[...], prev_row(vp_ref, vs_ref), muv_ref[...])
    sm = _shift_mix(sm_ref[...], prev_row(smp_ref, sms_ref), musm_ref[...])

    wd = jnp.tanh(sm[:, 0:ld]).astype(BF16)
    ad = sm[:, ld:ld + la].astype(BF16)
    gd = jax.nn.sigmoid(sm[:, ld + la:ld + la + lgp]).astype(BF16)
    g_ref[...] = jnp.dot(gd, g2_ref[...], preferred_element_type=F32)

    rep = rep_ref[...]
    grp = rep.shape[1]
    for b in range(r.shape[1] // LANES):
        rrep_ref[:, b * grp:(b + 1) * grp] = jnp.dot(
            r[:, b * LANES:(b + 1) * LANES], rep, precision=HIGHEST, preferred_element_type=F32)
        krep_ref[:, b * grp:(b + 1) * grp] = jnp.dot(
            k[:, b * LANES:(b + 1) * LANES], rep, precision=HIGHEST, preferred_element_type=F32)

    tc = r.shape[0]
    zero = jnp.zeros((tc, LANES), F32)

    def blk(b):
        return slice(b * LANES, (b + 1) * LANES)

    ss = zero
    for b in range(n_kblk):
        kk = krep_ref[:, blk(b)] * kkw_ref[:, blk(b)]
        ss = ss + kk * kk
    inv_norm = 1.0 / jnp.maximum(jnp.sqrt(ss), 1e-12)

    def main(b, carry):
        c1, c2, c3 = carry
        sl = blk(b)
        rr = rrep_ref[:, sl]
        kr = krep_ref[:, sl]
        lw = jnp.dot(wd, w2_ref[:, sl], preferred_element_type=F32)
        lav = jnp.dot(ad, a2_ref[:, sl], preferred_element_type=F32)
        w = -_softplus(-(w0_ref[:, sl] + lw)) - 0.5
        decay = jnp.exp(-jnp.exp(w))
        a = jax.nn.sigmoid(a0_ref[:, sl] + lav)
        kkn = kr * kkw_ref[:, sl] * inv_norm
        kmod = kr * (1.0 + (a - 1.0) * kaw_ref[:, sl])
        kka = kkn * a
        nkk_ref[:, sl] = -kkn
        wr_ref[:, sl] = decay * rr
        dec_ref[:, sl] = decay
        kka_ref[:, sl] = kka
        kmod_ref[:, sl] = kmod
        rk = rr * kmod
        return c1 + kka * rr, c2 + rk, c3 + rk * rkw_ref[:, sl]

    carry = (zero, zero, zero)
    for b in range(n_kblk):
        carry = main(b, carry)
    c1, c2, c3 = carry
    c_ref[:, 0:LANES] = c1
    c_ref[:, LANES:2 * LANES] = c2
    c_ref[:, 2 * LANES:3 * LANES] = c3


def rwkv_prep(proj, row_off, n_seq, T, shift, lw):
    RW = lw['mu_r'].shape[-1]
    rep_w = R_HEAD_DIM * LANES
    tc = min(SEQ_CHUNK, T)
    assert T % tc == 0 and row_off % tc == 0 and tc % SUBLANES == 0
    nch = T // tc
    rows = n_seq * T
    off_b = row_off // tc
    sm_col = lw['sm_off'] // SMALL_W
    ld, la, lgp = lw['ld'], lw['la'], lw['lgp']

    def tile(width, colblk):
        return pl.BlockSpec((tc, width), lambda s, j: (off_b + s * nch + j, colblk))

    def prev(width, colblk):
        per = tc // SUBLANES
        return pl.BlockSpec((SUBLANES, width),
                            lambda s, j: (jnp.maximum((off_b + s * nch + j) * per - 1, 0), colblk))

    def state(width):
        return pl.BlockSpec((None, 1, width), lambda s, j: (s, 0, 0))

    def const(shape):
        return pl.BlockSpec(shape, lambda s, j: tuple(0 for _ in shape))

    def out(width):
        return pl.BlockSpec((tc, width), lambda s, j: (s * nch + j, 0))

    in_specs = [tile(RW, 0), tile(RW, 1), tile(RW, 2), tile(SMALL_W, sm_col),
                prev(RW, 0), prev(RW, 1), prev(RW, 2), prev(SMALL_W, sm_col),
                state(RW), state(RW), state(RW), state(SMALL_W),
                const((1, RW)), const((1, RW)), const((1, RW)), const((1, SMALL_W)),
                const((1, rep_w)), const((1, rep_w)), const((1, rep_w)), const((1, rep_w)), const((1, rep_w)),
                const((ld, rep_w)), const((la, rep_w)), const((lgp, RW)), const(lw['rep'].shape)]
    out_specs = [out(rep_w)] * 5 + [out(3 * LANES), out(RW), out(RW)]
    out_shape = ([jax.ShapeDtypeStruct((rows, rep_w), F32)] * 5
                 + [jax.ShapeDtypeStruct((rows, 3 * LANES), F32),
                    jax.ShapeDtypeStruct((rows, RW), F32), jax.ShapeDtypeStruct((rows, RW), F32)])
    body = functools.partial(_rwkv_prep_body, ld=ld, la=la, lgp=lgp, n_kblk=R_HEAD_DIM)
    return pl.pallas_call(
        body,
        grid=(n_seq, nch),
        in_specs=in_specs,
        out_specs=out_specs,
        out_shape=out_shape,
        scratch_shapes=[pltpu.VMEM((tc, rep_w), F32), pltpu.VMEM((tc, rep_w), F32)],
        compiler_params=_params(("parallel", "arbitrary")),
        name="rwkv_prep",
    )(proj, proj, proj, proj, proj, proj, proj, proj,
      shift['r'], shift['k'], shift['v'], shift['sm'],
      lw['mu_r'], lw['mu_k'], lw['mu_v'], lw['mu_sm'],
      lw['w0'], lw['a0'], lw['k_k'], lw['k_a'], lw['r_k'],
      lw['w2'], lw['a2'], lw['g2'], lw['rep'])


def _rwkv_rec_body(nkk_ref, wr_ref, dec_ref, kka_ref, kmod_ref, v_ref, c_ref, s0_ref,
                   y_ref, sout_ref, s_ref, *, tc, n_grp, nch):
    j = pl.program_id(1)

    @pl.when(j == 0)
    def _():
        s_ref[...] = s0_ref[...]

    def step(t, carry):
        nkk = nkk_ref[t]
        wr = wr_ref[t]
        dec = dec_ref[t]
        kka = kka_ref[t]
        kmod = kmod_ref[t]
        crow = c_ref[pl.ds(t, 1), :]
        c1 = crow[:, 0:LANES]
        c2 = crow[:, LANES:2 * LANES]
        for g in range(n_grp):
            sg = s_ref[g]
            sa = jnp.sum(sg * nkk, axis=0, keepdims=True)
            yp = jnp.sum(sg * wr, axis=0, keepdims=True)
            vg = v_ref[t, g:g + 1, :]
            s_ref[g] = sg * dec + sa * kka + vg * kmod
            y_ref[t, g:g + 1, :] = yp + sa * c1 + vg * c2
        return carry

    lax.fori_loop(0, tc, step, 0)

    @pl.when(j == nch - 1)
    def _():
        sout_ref[...] = s_ref[...]


def rwkv_rec(nkk, wr, dec, kka, kmod, vmix, c, s0, n_seq, T):
    rows = n_seq * T
    n_grp = s0.shape[1]
    tc = min(SEQ_CHUNK, T)
    nch = T // tc
    k3 = lambda x: x.reshape(rows, R_HEAD_DIM, LANES)
    kspec = pl.BlockSpec((tc, R_HEAD_DIM, LANES), lambda s, j: (s * nch + j, 0, 0))
    vspec = pl.BlockSpec((tc, n_grp, LANES), lambda s, j: (s * nch + j, 0, 0))
    sspec = pl.BlockSpec((None, n_grp, R_HEAD_DIM, LANES), lambda s, j: (s, 0, 0, 0))
    body = functools.partial(_rwkv_rec_body, tc=tc, n_grp=n_grp, nch=nch)
    return pl.pallas_call(
        body,
        grid=(n_seq, nch),
        in_specs=[kspec] * 5 + [vspec, pl.BlockSpec((tc, 3 * LANES), lambda s, j: (s * nch + j, 0)), sspec],
        out_specs=[vspec, sspec],
        out_shape=[jax.ShapeDtypeStruct((rows, n_grp, LANES), F32),
                   jax.ShapeDtypeStruct(s0.shape, F32)],
        scratch_shapes=[pltpu.VMEM((n_grp, R_HEAD_DIM, LANES), F32)],
        compiler_params=_params(("parallel", "arbitrary")),
        name="rwkv_rec",
    )(k3(nkk), k3(wr), k3(dec), k3(kka), k3(kmod), vmix.reshape(rows, n_grp, LANES), c, s0)


def _rwkv_post_body(y_ref, v_ref, g_ref, c_ref, lnw_ref, lnb_ref, o_ref, *, n_grp, n_heads):
    def head_sum(x):
        s = x[:, 0:LANES]
        for g in range(1, n_grp):
            s = s + x[:, g * LANES:(g + 1) * LANES]
        shift = LANES // 2
        while shift >= n_heads:
            s = s + pltpu.roll(s, shift, axis=1)
            shift //= 2
        return jnp.concatenate([s] * n_grp, axis=1)

    y = y_ref[...]
    inv_n = 1.0 / R_HEAD_DIM
    mu = head_sum(y) * inv_n
    d = y - mu
    var = head_sum(d * d) * inv_n
    yln = d * lax.rsqrt(var + R_GN_EPS) * lnw_ref[...] + lnb_ref[...]
    c3 = jnp.concatenate([c_ref[:, 2 * LANES:3 * LANES]] * n_grp, axis=1)
    o_ref[...] = ((yln + c3 * v_ref[...]) * g_ref[...]).astype(o_ref.dtype)


def rwkv_post(y, vmix, g, c, ln_w, ln_b, n_heads):
    M, RW = y.shape
    tm = _pick_tile(M, EW_ROWS)
    row = pl.BlockSpec((tm, RW), lambda i: (i, 0))
    vec = pl.BlockSpec((1, RW), lambda i: (0, 0))
    body = functools.partial(_rwkv_post_body, n_grp=RW // LANES, n_heads=n_heads)
    return pl.pallas_call(
        body,
        grid=(M // tm,),
        in_specs=[row, row, row, pl.BlockSpec((tm, 3 * LANES), lambda i: (i, 0)), vec, vec],
        out_specs=row,
        out_shape=jax.ShapeDtypeStruct((M, RW), BF16),
        compiler_params=_params(("parallel",)),
        name="rwkv_post",
    )(y, vmix, g, c, ln_w, ln_b)


def _gla_body(q_ref, k_ref, v_ref, rg_ref, ag_ref, a2_ref, ab_ref, gn_ref, s0_ref,
              o_ref, sout_ref, st_ref, o_scr, *, C, nch, scale):
    c = pl.program_id(2)

    @pl.when(c == 0)
    def _():
        st_ref[...] = s0_ref[...]

    x = jnp.dot(ag_ref[...], a2_ref[...], precision=HIGHEST, preferred_element_type=F32) + ab_ref[...]
    log_a = -_softplus(-x) * (1.0 / G_GATE_TAU)
    ti = lax.broadcasted_iota(jnp.int32, (C, C), 0)
    si = lax.broadcasted_iota(jnp.int32, (C, C), 1)
    tri = (si <= ti).astype(F32)
    b = jnp.dot(tri, log_a, precision=HIGHEST, preferred_element_type=F32)
    b_last = b[C - 1:C, :]

    q = q_ref[...] * scale
    k = k_ref[...]
    v = v_ref[...]
    st = st_ref[...]

    nt = (((1,), (1,)), ((), ()))
    tn = (((0,), (0,)), ((), ()))
    o_scr[...] = lax.dot_general(q * jnp.exp(b), st, nt, precision=HIGHEST, preferred_element_type=F32)

    for blk in range(C // GLA_SUB):
        t0 = blk * GLA_SUB
        rows = slice(t0, t0 + GLA_SUB)
        qb, kb, bb, vb = q[rows], k[rows], b[rows], v[rows]
        acc = o_scr[rows, :]
        if blk > 0:
            b0 = bb[0:1, :]
            qd = qb * jnp.exp(bb - b0)
            kd = k[0:t0] * jnp.exp(b0 - b[0:t0])
            sc = lax.dot_general(qd, kd, nt, precision=HIGHEST, preferred_element_type=F32)
            acc = acc + jnp.dot(sc, v[0:t0], precision=HIGHEST, preferred_element_type=F32)
        rowi = lax.broadcasted_iota(jnp.int32, (GLA_SUB, LANES), 0)
        for s in range(GLA_SUB):
            e = jnp.where(rowi >= s, jnp.exp(bb - bb[s:s + 1, :]), 0.0)
            sc_s = jnp.sum(qb * e * kb[s:s + 1, :], axis=-1, keepdims=True)
            acc = acc + sc_s * vb[s:s + 1, :]
        o_scr[rows, :] = acc

    kdec = k * jnp.exp(b_last - b)
    st_ref[...] = st * jnp.exp(b_last) + lax.dot_general(v, kdec, tn, precision=HIGHEST,
                                                          preferred_element_type=F32)

    o = o_scr[...]
    ms = jnp.mean(o * o, axis=-1, keepdims=True)
    o_ref[...] = (o * lax.rsqrt(ms + NORM_EPS) * gn_ref[...] * jax.nn.silu(rg_ref[...])).astype(o_ref.dtype)

    @pl.when(c == nch - 1)
    def _():
        sout_ref[...] = st_ref[...]


def gla(proj, row_off, n_seq, T, s0t, lw, dims):
    GK, GV = dims['GK'], dims['GV']
    dk, dv = GK // G_HEADS, GV // G_HEADS
    C = min(SEQ_CHUNK, T)
    assert T % C == 0 and C % GLA_SUB == 0 and row_off % C == 0
    nch = T // C
    rows = n_seq * T
    off_b = row_off // C

    def tile(width, col0):
        cb = col0 // width
        assert col0 % width == 0
        return pl.BlockSpec((C, width), lambda s, h, c: (off_b + s * nch + c, cb + h))

    ag_cb = dims['o_ag'] // LANES
    in_specs = [tile(dk, dims['o_qg']), tile(dk, dims['o_kg']), tile(dv, dims['o_vg']), tile(dv, dims['o_rg']),
                pl.BlockSpec((C, LANES), lambda s, h, c: (off_b + s * nch + c, ag_cb)),
                pl.BlockSpec((LANES, dk), lambda s, h, c: (0, h)),
                pl.BlockSpec((1, dk), lambda s, h, c: (0, h)),
                pl.BlockSpec((1, dv), lambda s, h, c: (0, h)),
                pl.BlockSpec((None, None, dv, dk), lambda s, h, c: (s, h, 0, 0))]
    body = functools.partial(_gla_body, C=C, nch=nch, scale=float(dk) ** -0.5)
    return pl.pallas_call(
        body,
        grid=(n_seq, G_HEADS, nch),
        in_specs=in_specs,
        out_specs=[pl.BlockSpec((C, dv), lambda s, h, c: (s * nch + c, h)),
                   pl.BlockSpec((None, None, dv, dk), lambda s, h, c: (s, h, 0, 0))],
        out_shape=[jax.ShapeDtypeStruct((rows, GV), BF16),
                   jax.ShapeDtypeStruct(s0t.shape, F32)],
        scratch_shapes=[pltpu.VMEM((dv, dk), F32), pltpu.VMEM((C, dv), F32)],
        compiler_params=_params(("parallel", "parallel", "arbitrary")),
        name="gla",
    )(proj, proj, proj, proj, proj, lw['gla_a2'], lw['gla_ab'], lw['gla_norm'], s0t)


def _hx_to_xh(w, n_heads):
    sh = w.shape[:-1]
    x = w.shape[-1] // n_heads
    return jnp.swapaxes(w.reshape(*sh, n_heads, x), -1, -2).reshape(*sh, n_heads * x)


def _xh_to_hx(w, n_heads):
    sh = w.shape[:-1]
    x = w.shape[-1] // n_heads
    return jnp.swapaxes(w.reshape(*sh, x, n_heads), -1, -2).reshape(*sh, n_heads * x)


def _rep_kh(w, n_heads):
    sh = w.shape[:-1]
    vv = LANES // n_heads
    x = jnp.swapaxes(w.reshape(*sh, n_heads, R_HEAD_DIM), -1, -2)
    x = jnp.broadcast_to(x[..., :, None, :], (*sh, R_HEAD_DIM, vv, n_heads))
    return x.reshape(*sh, R_HEAD_DIM * LANES)


def _pad_last(x, width):
    return jnp.pad(x, [(0, 0)] * (x.ndim - 1) + [(0, width - x.shape[-1])])


def _split_r_cols(x, RW, ld, la, lg):
    offs = [0, RW, 2 * RW, 3 * RW, 3 * RW + ld, 3 * RW + ld + la, 3 * RW + ld + la + lg]
    return [x[..., offs[i]:offs[i + 1]] for i in range(6)]


def _small_block(wd, ad, gd, lgp, tail=None):
    parts = [wd, ad, _pad_last(gd, lgp)]
    if tail is not None:
        parts.append(tail)
    return _pad_last(jnp.concatenate(parts, axis=-1), SMALL_W)


def _state_to_kernel(s, n_heads):
    n = s.shape[0]
    vv = LANES // n_heads
    g = R_HEAD_DIM // vv
    x = s.reshape(n, n_heads, g, vv, R_HEAD_DIM)
    return x.transpose(0, 2, 4, 3, 1).reshape(n, g, R_HEAD_DIM, LANES)


def _state_from_kernel(s, n_heads):
    n, g = s.shape[0], s.shape[1]
    vv = LANES // n_heads
    x = s.reshape(n, g, R_HEAD_DIM, vv, n_heads)
    return x.transpose(0, 4, 1, 3, 2).reshape(n, n_heads, R_HEAD_DIM, R_HEAD_DIM)


def kernel(x_prompt, x_sample, state_shift, state_rwkv, state_gla, p_prompt, p_sample, norm_mix_pre, norm_mix_post, norm_ffn_pre, norm_ffn_post, norm_ple, w_in, rwkv_mu, rwkv_w0, rwkv_w2, rwkv_a0, rwkv_a2, rwkv_g2, rwkv_k_k, rwkv_k_a, rwkv_r_k, rwkv_ln_w, rwkv_ln_b, gla_a2, gla_ab, gla_norm, w_proj_rwkv, w_proj_gla, w_out, ffn_gate, ffn_up, ffn_down, ple_proj, ple_gate):
    Bp, Tp, D = x_prompt.shape
    Bs, Ts, _ = x_sample.shape
    L = w_in.shape[0]
    RW = rwkv_w0.shape[-1]
    H = RW // R_HEAD_DIM
    assert LANES % H == 0 and R_HEAD_DIM % (LANES // H) == 0
    ld, la, lg = rwkv_w2.shape[1], rwkv_a2.shape[1], rwkv_g2.shape[1]
    lgp = _round_up(lg, LANES)
    GK, GV, GR = gla_a2.shape[-1], gla_norm.shape[-1], gla_a2.shape[1]
    FF = ffn_gate.shape[-1]
    FFp = _round_up(FF, K_TILE)
    Mp, Ms = Bp * Tp, Bs * Ts
    M = Mp + Ms
    groups = [(0, Bp, Tp), (Mp, Bs, Ts)]

    o_vg = 3 * RW
    o_rg = o_vg + GV
    o_qg = o_rg + GV
    o_kg = o_qg + GK
    o_gr = o_kg + GK
    o_gg = o_gr + D
    o_sm = o_gg + D
    NP = o_sm + SMALL_W
    o_ag_in_sm = ld + la + lgp
    assert o_ag_in_sm + LANES <= SMALL_W and o_sm % SMALL_W == 0 and GR <= LANES
    dims = dict(GK=GK, GV=GV, o_qg=o_qg, o_kg=o_kg, o_vg=o_vg, o_rg=o_rg, o_ag=o_sm + o_ag_in_sm)

    r_cols = 3 * RW + ld + la + lg
    g_cols = 2 * GK + 2 * GV + GR
    w_r, w_k, w_v, w_wd, w_ad, w_gd = _split_r_cols(w_in[..., :r_cols], RW, ld, la, lg)
    w_g = w_in[..., r_cols:r_cols + g_cols]
    w_qg, w_kg = w_g[..., :GK], w_g[..., GK:2 * GK]
    w_vg, w_rg = w_g[..., 2 * GK:2 * GK + GV], w_g[..., 2 * GK + GV:2 * GK + 2 * GV]
    w_ag = w_g[..., 2 * GK + 2 * GV:]
    w_gate = w_in[..., r_cols + g_cols:]
    win = jnp.concatenate(
        [_hx_to_xh(w_r, H), _hx_to_xh(w_k, H), _hx_to_xh(w_v, H), w_vg, w_rg, w_qg, w_kg, w_gate,
         _small_block(w_wd, w_ad, w_gd, lgp, w_ag)], axis=-1).astype(BF16)
    assert win.shape[-1] == NP

    mu_r, mu_k, mu_v, mu_wd, mu_ad, mu_gd = _split_r_cols(rwkv_mu, RW, ld, la, lg)
    kq = LANES // H
    src = jnp.arange(LANES)
    dst = jnp.arange(kq * LANES)
    rep_mat = (src[:, None] == ((dst // LANES) * H + dst % H)[None, :]).astype(F32)

    def layer_weights(i):
        return dict(
            mu_r=_hx_to_xh(mu_r[i], H)[None], mu_k=_hx_to_xh(mu_k[i], H)[None], mu_v=_hx_to_xh(mu_v[i], H)[None],
            mu_sm=_small_block(mu_wd[i], mu_ad[i], mu_gd[i], lgp)[None],
            w0=_rep_kh(rwkv_w0[i], H)[None], a0=_rep_kh(rwkv_a0[i], H)[None],
            k_k=_rep_kh(rwkv_k_k[i], H)[None], k_a=_rep_kh(rwkv_k_a[i], H)[None],
            r_k=_rep_kh(rwkv_r_k[i].reshape(RW), H)[None],
            w2=_rep_kh(rwkv_w2[i], H).astype(BF16), a2=_rep_kh(rwkv_a2[i], H).astype(BF16),
            g2=jnp.pad(_hx_to_xh(rwkv_g2[i], H), ((0, lgp - lg), (0, 0))).astype(BF16),
            rep=rep_mat, sm_off=o_sm, ld=ld, la=la, lgp=lgp,
            ln_w=_hx_to_xh(rwkv_ln_w[i], H)[None], ln_b=_hx_to_xh(rwkv_ln_b[i], H)[None],
            gla_a2=jnp.pad(gla_a2[i], ((0, LANES - GR), (0, 0))), gla_ab=gla_ab[i][None],
            gla_norm=gla_norm[i][None],
            w_pr=jnp.swapaxes(w_proj_rwkv[i].reshape(H, R_HEAD_DIM, D), 0, 1).reshape(RW, D).astype(BF16),
            w_pg=w_proj_gla[i].astype(BF16), w_out=w_out[i].astype(BF16),
            ffn_gate=_pad_last(ffn_gate[i], FFp).astype(BF16), ffn_up=_pad_last(ffn_up[i], FFp).astype(BF16),
            ffn_down=jnp.pad(ffn_down[i], ((0, FFp - FF), (0, 0))).astype(BF16),
            ple_proj=ple_proj[i].astype(BF16), ple_gate=ple_gate[i].astype(BF16))

    def shift_state(s):
        s_r, s_k, s_v, s_wd, s_ad, s_gd = _split_r_cols(s, RW, ld, la, lg)
        return dict(r=_hx_to_xh(s_r, H)[:, None], k=_hx_to_xh(s_k, H)[:, None], v=_hx_to_xh(s_v, H)[:, None],
                    sm=_small_block(s_wd, s_ad, s_gd, lgp)[:, None])

    def shift_from_rows(rows):
        sm = rows[:, o_sm:o_sm + SMALL_W]
        return jnp.concatenate(
            [_xh_to_hx(rows[:, 0:RW], H), _xh_to_hx(rows[:, RW:2 * RW], H), _xh_to_hx(rows[:, 2 * RW:3 * RW], H),
             sm[:, 0:ld], sm[:, ld:ld + la], sm[:, ld + la:ld + la + lg]], axis=-1)

    h = jnp.concatenate([x_prompt.reshape(Mp, D), x_sample.reshape(Ms, D)], axis=0)
    p_all = jnp.concatenate([p_prompt.reshape(L, Mp, -1), p_sample.reshape(L, Ms, -1)], axis=1).astype(BF16)
    zeros_shift = jnp.zeros((Bp, r_cols), F32)
    zeros_rwkv = jnp.zeros((Bp, H, R_HEAD_DIM, R_HEAD_DIM), F32)
    zeros_gla = jnp.zeros((Bp, G_HEADS, GK // G_HEADS, GV // G_HEADS), F32)

    new_shift = [[], []]
    new_rwkv = [[], []]
    new_gla = [[], []]
    for i in range(L):
        lw = layer_weights(i)
        z = norm_cast(h, norm_mix_pre[i])
        proj = fused_mm([z], [(0, win[i])], [], _ep_identity, NP, F32, "in_proj")

        o_r_parts, o_g_parts = [], []
        states = [(zeros_shift, zeros_rwkv, zeros_gla), (state_shift[i], state_rwkv[i], state_gla[i])]
        for gi, (row_off, n_seq, T) in enumerate(groups):
            s_shift, s_rwkv, s_gla = states[gi]
            nkk, wr, dec, kka, kmod, c, vmix, g = rwkv_prep(proj, row_off, n_seq, T, shift_state(s_shift), lw)
            y, s_new = rwkv_rec(nkk, wr, dec, kka, kmod, vmix, c, _state_to_kernel(s_rwkv, H), n_seq, T)
            o_r_parts.append(rwkv_post(y.reshape(n_seq * T, RW), vmix, g, c, lw['ln_w'], lw['ln_b'], H))
            o_g, s_gla_new = gla(proj, row_off, n_seq, T, jnp.swapaxes(s_gla, -1, -2), lw, dims)
            o_g_parts.append(o_g)
            last_rows = proj[row_off:row_off + n_seq * T].reshape(n_seq, T, NP)[:, T - 1]
            new_shift[gi].append(shift_from_rows(last_rows))
            new_rwkv[gi].append(_state_from_kernel(s_new, H))
            new_gla[gi].append(jnp.swapaxes(s_gla_new, -1, -2))
        o_r = jnp.concatenate(o_r_parts, axis=0)
        o_g = jnp.concatenate(o_g_parts, axis=0)

        tn = _pick_tile(D, COL_TILE, LANES)
        merged = fused_mm([o_r, o_g], [(0, lw['w_pr']), (1, lw['w_pg'])],
                          [(proj, o_gr // tn), (proj, o_gg // tn)],
                          _ep_merge, D, BF16, "merge", tm_target=ROW_TILE // 2)
        mix = fused_mm([merged], [(0, lw['w_out'])], [], _ep_identity, D, F32, "out_proj")
        h1, u = resid_norm(h, mix, norm_mix_post[i], norm_ffn_pre[i])
        t = fused_mm([u], [(0, lw['ffn_gate']), (0, lw['ffn_up'])], [], _ep_swiglu, FFp, BF16, "ffn_up",
                     tm_target=ROW_TILE // 2)
        f = fused_mm([t], [(0, lw['ffn_down'])], [], _ep_identity, D, F32, "ffn_down", tk=K_TILE)
        h2, pn = resid_norm(h1, f, norm_ffn_post[i], norm_ple[i])
        h = ple_mm(pn, lw['ple_gate'], p_all[i], lw['ple_proj'], h2)

    y_prompt = h[:Mp].reshape(Bp, Tp, D)
    y_sample = h[Mp:].reshape(Bs, Ts, D)
    return (y_prompt, y_sample,
            jnp.stack(new_shift[0]), jnp.stack(new_rwkv[0]), jnp.stack(new_gla[0]),
            jnp.stack(new_shift[1]), jnp.stack(new_rwkv[1]), jnp.stack(new_gla[1]))
```

```python
import functools

import jax
import jax.numpy as jnp
from jax import lax
from jax.experimental import pallas as pl
from jax.experimental.pallas import tpu as pltpu

F32 = jnp.float32
BF16 = jnp.bfloat16
HIGHEST = lax.Precision.HIGHEST

NORM_EPS = 1e-6
R_GN_EPS = 64e-5
R_HEAD_DIM = 64
G_HEADS = 8
G_GATE_TAU = 16.0

LANES = 128
SUBLANES = 8
VMEM_LIMIT = 56 * 1024 * 1024

ROW_TILE = 1536
COL_TILE = 512
K_TILE = 1024
EW_ROWS = 256
SEQ_CHUNK = 64
GLA_SUB = 16
SMALL_W = 1024


def _round_up(x, m):
    return (x + m - 1) // m * m


def _pick_tile(n, target, mult=SUBLANES):
    best = None
    for t in range(mult, min(n, target) + 1, mult):
        if n % t == 0:
            best = t
    assert best is not None, (n, target)
    return best


def _params(sem):
    return pltpu.CompilerParams(dimension_semantics=sem, vmem_limit_bytes=VMEM_LIMIT)


def _rms(x, gain):
    ms = jnp.mean(x * x, axis=-1, keepdims=True)
    return x * lax.rsqrt(ms + NORM_EPS) * gain


def _softplus(x):
    return jnp.maximum(x, 0.0) + jnp.log(1.0 + jnp.exp(-jnp.abs(x)))


def _norm_cast_body(x_ref, g_ref, o_ref):
    o_ref[...] = _rms(x_ref[...], g_ref[...]).astype(o_ref.dtype)


def norm_cast(x, gain):
    M, D = x.shape
    tm = _pick_tile(M, EW_ROWS)
    return pl.pallas_call(
        _norm_cast_body,
        grid=(M // tm,),
        in_specs=[pl.BlockSpec((tm, D), lambda i: (i, 0)),
                  pl.BlockSpec((1, D), lambda i: (0, 0))],
        out_specs=pl.BlockSpec((tm, D), lambda i: (i, 0)),
        out_shape=jax.ShapeDtypeStruct((M, D), BF16),
        compiler_params=_params(("parallel",)),
        name="norm_cast",
    )(x, gain.reshape(1, D))


def _resid_norm_body(h_ref, x_ref, gpost_ref, gpre_ref, h1_ref, u_ref):
    h1 = h_ref[...] + _rms(x_ref[...], gpost_ref[...])
    h1_ref[...] = h1
    u_ref[...] = _rms(h1, gpre_ref[...]).astype(u_ref.dtype)


def resid_norm(h, x, g_post, g_pre):
    M, D = h.shape
    tm = _pick_tile(M, EW_ROWS)
    row = pl.BlockSpec((tm, D), lambda i: (i, 0))
    vec = pl.BlockSpec((1, D), lambda i: (0, 0))
    return pl.pallas_call(
        _resid_norm_body,
        grid=(M // tm,),
        in_specs=[row, row, vec, vec],
        out_specs=[row, row],
        out_shape=[jax.ShapeDtypeStruct((M, D), F32), jax.ShapeDtypeStruct((M, D), BF16)],
        compiler_params=_params(("parallel",)),
        name="resid_norm",
    )(h, x, g_post.reshape(1, D), g_pre.reshape(1, D))


def _mm_body(*refs, act_of, n_acts, n_extra, nk, epilogue):
    n_w = len(act_of)
    a_refs = refs[:n_acts]
    w_refs = refs[n_acts:n_acts + n_w]
    e_refs = refs[n_acts + n_w:n_acts + n_w + n_extra]
    o_ref = refs[n_acts + n_w + n_extra]
    acc_refs = refs[n_acts + n_w + n_extra + 1:]
    acts = [a[...] for a in a_refs]
    dots = [jnp.dot(acts[ai], w[...], preferred_element_type=F32) for ai, w in zip(act_of, w_refs)]
    if nk == 1:
        o_ref[...] = epilogue(dots, [e[...] for e in e_refs]).astype(o_ref.dtype)
        return
    k = pl.program_id(2)

    @pl.when(k == 0)
    def _():
        for acc, d in zip(acc_refs, dots):
            acc[...] = d

    @pl.when(k > 0)
    def _():
        for acc, d in zip(acc_refs, dots):
            acc[...] += d

    @pl.when(k == nk - 1)
    def _():
        o_ref[...] = epilogue([acc[...] for acc in acc_refs], [e[...] for e in e_refs]).astype(o_ref.dtype)


def fused_mm(acts, weights, extras, epilogue, n_out, out_dtype, name, tk=None, tm_target=None, tn=None):
    M, K = acts[0].shape
    tm = _pick_tile(M, ROW_TILE if tm_target is None else min(tm_target, ROW_TILE))
    tn = _pick_tile(n_out, COL_TILE, LANES) if tn is None else tn
    tk = K if tk is None else tk
    nk = K // tk
    assert K % tk == 0
    grid = (M // tm, pl.cdiv(n_out, tn), nk)
    in_specs = ([pl.BlockSpec((tm, tk), lambda i, j, k: (i, k)) for _ in acts]
                + [pl.BlockSpec((tk, tn), lambda i, j, k: (k, j)) for _ in weights]
                + [pl.BlockSpec((tm, tn), functools.partial(lambda i, j, k, off: (i, j + off), off=off))
                   for _, off in extras])
    scratch = [pltpu.VMEM((tm, tn), F32) for _ in weights] if nk > 1 else []
    body = functools.partial(_mm_body, act_of=tuple(a for a, _ in weights), n_acts=len(acts),
                             n_extra=len(extras), nk=nk, epilogue=epilogue)
    return pl.pallas_call(
        body,
        grid=grid,
        in_specs=in_specs,
        out_specs=pl.BlockSpec((tm, tn), lambda i, j, k: (i, j)),
        out_shape=jax.ShapeDtypeStruct((M, n_out), out_dtype),
        scratch_shapes=scratch,
        compiler_params=_params(("parallel", "parallel", "arbitrary")),
        name=name,
    )(*acts, *[w for _, w in weights], *[e for e, _ in extras])


def _ep_identity(dots, extras):
    return dots[0]


def _ep_merge(dots, extras):
    return jax.nn.sigmoid(extras[0]) * dots[0] + jax.nn.sigmoid(extras[1]) * dots[1]


def _ep_swiglu(dots, extras):
    return jax.nn.silu(dots[0]) * dots[1]


def _ep_ple(dots, extras):
    return extras[0] + jax.nn.sigmoid(dots[0]) * dots[1]


def _ple_body(pn_ref, wg_ref, p_ref, wp_ref, h_ref, o_ref):
    gate = jnp.dot(pn_ref[...], wg_ref[...], preferred_element_type=F32)
    proj = jnp.dot(p_ref[...], wp_ref[...], preferred_element_type=F32)
    o_ref[...] = h_ref[...] + jax.nn.sigmoid(gate) * proj


def ple_mm(pn, w_gate, p, w_proj, h):
    M, D = pn.shape
    P = p.shape[1]
    tm = _pick_tile(M, ROW_TILE)
    tn = _pick_tile(D, COL_TILE, LANES)
    return pl.pallas_call(
        _ple_body,
        grid=(M // tm, D // tn),
        in_specs=[pl.BlockSpec((tm, D), lambda i, j: (i, 0)),
                  pl.BlockSpec((D, tn), lambda i, j: (0, j)),
                  pl.BlockSpec((tm, P), lambda i, j: (i, 0)),
                  pl.BlockSpec((P, tn), lambda i, j: (0, j)),
                  pl.BlockSpec((tm, tn), lambda i, j: (i, j))],
        out_specs=pl.BlockSpec((tm, tn), lambda i, j: (i, j)),
        out_shape=jax.ShapeDtypeStruct((M, D), F32),
        compiler_params=_params(("parallel", "parallel")),
        name="ple_mm",
    )(pn, w_gate, p, w_proj, h)


def _shift_mix(x, prev_first, mu):
    rolled = pltpu.roll(x, 1, axis=0)
    row = lax.broadcasted_iota(jnp.int32, x.shape, 0)
    prev = jnp.where(row == 0, prev_first, rolled)
    return x + (prev - x) * mu


def _head_sum(x, n_heads):
    s = x[:, 0:LANES]
    for b in range(1, x.shape[1] // LANES):
        s = s + x[:, b * LANES:(b + 1) * LANES]
    shift = LANES // 2
    while shift >= n_heads:
        s = s + pltpu.roll(s, shift, axis=1)
        shift //= 2
    return s


def _tile_cols(x, n):
    return jnp.concatenate([x] * n, axis=1)


def _replicate(x, rep, out_ref):
    rows = x.shape[0]
    n_blk = x.shape[1] // LANES
    grp = rep.shape[1]
    xs = jnp.concatenate([x[:, b * LANES:(b + 1) * LANES] for b in range(n_blk)], axis=0)
    hi = xs.astype(BF16)
    rest = xs - hi.astype(F32)
    mid = rest.astype(BF16)
    lo = (rest - mid.astype(F32)).astype(BF16)
    y = jnp.dot(jnp.concatenate([hi, mid, lo], axis=1), rep, preferred_element_type=F32)
    for b in range(n_blk):
        out_ref[:, b * grp:(b + 1) * grp] = y[b * rows:(b + 1) * rows, :]


def _rwkv_prep_body(r_ref, k_ref, v_ref, sm_ref, rp_ref, kp_ref, vp_ref, smp_ref,
                    rs_ref, ks_ref, vs_ref, sms_ref,
                    mur_ref, muk_ref, muv_ref, musm_ref,
                    w0_ref, a0_ref, kkw_ref, kaw_ref, rkw_ref,
                    w2_ref, a2_ref, g2_ref, rep_ref,
                    nkk_ref, wr_ref, dec_ref, kka_ref, kmod_ref, c_ref, vmix_ref, g_ref,
                    *, ld, la, lgp, n_heads):
    j = pl.program_id(1)
    first = j == 0

    def prev_row(p_ref, s_ref):
        return jnp.where(first, s_ref[...], p_ref[SUBLANES - 1:SUBLANES, :])

    r = _shift_mix(r_ref[...], prev_row(rp_ref, rs_ref), mur_ref[...])
    k = _shift_mix(k_ref[...], prev_row(kp_ref, ks_ref), muk_ref[...])
    vmix_ref[...] = _shift_mix(v_ref[...], prev_row(vp_ref, vs_ref), muv_ref[...])
    sm = _shift_mix(sm_ref[...], prev_row(smp_ref, sms_ref), musm_ref[...])

    wd = jnp.tanh(sm[:, 0:ld]).astype(BF16)
    ad = sm[:, ld:ld + la].astype(BF16)
    gd = jax.nn.sigmoid(sm[:, ld + la:ld + la + lgp]).astype(BF16)
    g_ref[...] = jnp.dot(gd, g2_ref[...], preferred_element_type=F32)

    n_blk = r.shape[1] // LANES
    w = -_softplus(-(w0_ref[...] + jnp.dot(wd, w2_ref[...], preferred_element_type=F32))) - 0.5
    decay = jnp.exp(-jnp.exp(w))
    a = jax.nn.sigmoid(a0_ref[...] + jnp.dot(ad, a2_ref[...], preferred_element_type=F32))
    kk = k * kkw_ref[...]
    inv_norm = 1.0 / jnp.maximum(jnp.sqrt(_head_sum(kk * kk, n_heads)), 1e-12)
    kkn = kk * _tile_cols(inv_norm, n_blk)
    kmod = k * (1.0 + (a - 1.0) * kaw_ref[...])
    kka = kkn * a
    rk = r * kmod
    c_ref[:, 0:LANES] = _head_sum(kka * r, n_heads)
    c_ref[:, LANES:2 * LANES] = _head_sum(rk, n_heads)
    c_ref[:, 2 * LANES:3 * LANES] = _head_sum(rk * rkw_ref[...], n_heads)

    rep = rep_ref[...]
    _replicate(-kkn, rep, nkk_ref)
    _replicate(decay * r, rep, wr_ref)
    _replicate(decay, rep, dec_ref)
    _replicate(kka, rep, kka_ref)
    _replicate(kmod, rep, kmod_ref)


def rwkv_prep(proj, row_off, n_seq, T, shift, lw):
    RW = lw['mu_r'].shape[-1]
    rep_w = R_HEAD_DIM * LANES
    tc = min(SEQ_CHUNK, T)
    assert T % tc == 0 and row_off % tc == 0 and tc % SUBLANES == 0
    nch = T // tc
    rows = n_seq * T
    off_b = row_off // tc
    sm_col = lw['sm_off'] // SMALL_W
    ld, la, lgp = lw['ld'], lw['la'], lw['lgp']

    def tile(width, colblk):
        return pl.BlockSpec((tc, width), lambda s, j: (off_b + s * nch + j, colblk))

    def prev(width, colblk):
        per = tc // SUBLANES
        return pl.BlockSpec((SUBLANES, width),
                            lambda s, j: (jnp.maximum((off_b + s * nch + j) * per - 1, 0), colblk))

    def state(width):
        return pl.BlockSpec((None, 1, width), lambda s, j: (s, 0, 0))

    def const(shape):
        return pl.BlockSpec(shape, lambda s, j: tuple(0 for _ in shape))

    def out(width):
        return pl.BlockSpec((tc, width), lambda s, j: (s * nch + j, 0))

    in_specs = [tile(RW, 0), tile(RW, 1), tile(RW, 2), tile(SMALL_W, sm_col),
                prev(RW, 0), prev(RW, 1), prev(RW, 2), prev(SMALL_W, sm_col),
                state(RW), state(RW), state(RW), state(SMALL_W),
                const((1, RW)), const((1, RW)), const((1, RW)), const((1, SMALL_W)),
                const((1, RW)), const((1, RW)), const((1, RW)), const((1, RW)), const((1, RW)),
                const((ld, RW)), const((la, RW)), const((lgp, RW)), const(lw['rep'].shape)]
    out_specs = [out(rep_w)] * 5 + [out(3 * LANES), out(RW), out(RW)]
    out_shape = ([jax.ShapeDtypeStruct((rows, rep_w), F32)] * 5
                 + [jax.ShapeDtypeStruct((rows, 3 * LANES), F32),
                    jax.ShapeDtypeStruct((rows, RW), F32), jax.ShapeDtypeStruct((rows, RW), F32)])
    body = functools.partial(_rwkv_prep_body, ld=ld, la=la, lgp=lgp, n_heads=RW // R_HEAD_DIM)
    return pl.pallas_call(
        body,
        grid=(n_seq, nch),
        in_specs=in_specs,
        out_specs=out_specs,
        out_shape=out_shape,
        compiler_params=_params(("parallel", "arbitrary")),
        name="rwkv_prep",
    )(proj, proj, proj, proj, proj, proj, proj, proj,
      shift['r'], shift['k'], shift['v'], shift['sm'],
      lw['mu_r'], lw['mu_k'], lw['mu_v'], lw['mu_sm'],
      lw['w0'], lw['a0'], lw['k_k'], lw['k_a'], lw['r_k'],
      lw['w2'], lw['a2'], lw['g2'], lw['rep'])


def _rwkv_rec_body(nkk_ref, wr_ref, dec_ref, kka_ref, kmod_ref, v_ref, c_ref, s0_ref,
                   y_ref, sout_ref, s_ref, *, tc, n_grp, nch):
    j = pl.program_id(1)

    @pl.when(j == 0)
    def _():
        s_ref[...] = s0_ref[...]

    def step(t, carry):
        nkk = nkk_ref[t]
        wr = wr_ref[t]
        dec = dec_ref[t]
        kka = kka_ref[t]
        kmod = kmod_ref[t]
        crow = c_ref[pl.ds(t, 1), :]
        c1 = crow[:, 0:LANES]
        c2 = crow[:, LANES:2 * LANES]
        for g in range(n_grp):
            sg = s_ref[g]
            sa = jnp.sum(sg * nkk, axis=0, keepdims=True)
            yp = jnp.sum(sg * wr, axis=0, keepdims=True)
            vg = v_ref[t, g:g + 1, :]
            s_ref[g] = sg * dec + sa * kka + vg * kmod
            y_ref[t, g:g + 1, :] = yp + sa * c1 + vg * c2
        return carry

    lax.fori_loop(0, tc, step, 0)

    @pl.when(j == nch - 1)
    def _():
        sout_ref[...] = s_ref[...]


def rwkv_rec(nkk, wr, dec, kka, kmod, vmix, c, s0, n_seq, T):
    rows = n_seq * T
    n_grp = s0.shape[1]
    tc = min(SEQ_CHUNK, T)
    nch = T // tc
    k3 = lambda x: x.reshape(rows, R_HEAD_DIM, LANES)
    kspec = pl.BlockSpec((tc, R_HEAD_DIM, LANES), lambda s, j: (s * nch + j, 0, 0))
    vspec = pl.BlockSpec((tc, n_grp, LANES), lambda s, j: (s * nch + j, 0, 0))
    sspec = pl.BlockSpec((None, n_grp, R_HEAD_DIM, LANES), lambda s, j: (s, 0, 0, 0))
    body = functools.partial(_rwkv_rec_body, tc=tc, n_grp=n_grp, nch=nch)
    return pl.pallas_call(
        body,
        grid=(n_seq, nch),
        in_specs=[kspec] * 5 + [vspec, pl.BlockSpec((tc, 3 * LANES), lambda s, j: (s * nch + j, 0)), sspec],
        out_specs=[vspec, sspec],
        out_shape=[jax.ShapeDtypeStruct((rows, n_grp, LANES), F32),
                   jax.ShapeDtypeStruct(s0.shape, F32)],
        scratch_shapes=[pltpu.VMEM((n_grp, R_HEAD_DIM, LANES), F32)],
        compiler_params=_params(("parallel", "arbitrary")),
        name="rwkv_rec",
    )(k3(nkk), k3(wr), k3(dec), k3(kka), k3(kmod), vmix.reshape(rows, n_grp, LANES), c, s0)


def _rwkv_post_body(y_ref, v_ref, g_ref, c_ref, lnw_ref, lnb_ref, o_ref, *, n_grp, n_heads):
    def head_sum(x):
        return _tile_cols(_head_sum(x, n_heads), n_grp)

    y = y_ref[...]
    inv_n = 1.0 / R_HEAD_DIM
    mu = head_sum(y) * inv_n
    d = y - mu
    var = head_sum(d * d) * inv_n
    yln = d * lax.rsqrt(var + R_GN_EPS) * lnw_ref[...] + lnb_ref[...]
    c3 = jnp.concatenate([c_ref[:, 2 * LANES:3 * LANES]] * n_grp, axis=1)
    o_ref[...] = ((yln + c3 * v_ref[...]) * g_ref[...]).astype(o_ref.dtype)


def rwkv_post(y, vmix, g, c, ln_w, ln_b, n_heads):
    M, RW = y.shape
    tm = _pick_tile(M, EW_ROWS)
    row = pl.BlockSpec((tm, RW), lambda i: (i, 0))
    vec = pl.BlockSpec((1, RW), lambda i: (0, 0))
    body = functools.partial(_rwkv_post_body, n_grp=RW // LANES, n_heads=n_heads)
    return pl.pallas_call(
        body,
        grid=(M // tm,),
        in_specs=[row, row, row, pl.BlockSpec((tm, 3 * LANES), lambda i: (i, 0)), vec, vec],
        out_specs=row,
        out_shape=jax.ShapeDtypeStruct((M, RW), BF16),
        compiler_params=_params(("parallel",)),
        name="rwkv_post",
    )(y, vmix, g, c, ln_w, ln_b)


def _gla_body(q_ref, k_ref, v_ref, rg_ref, ag_ref, a2_ref, ab_ref, gn_ref, s0_ref,
              o_ref, sout_ref, st_ref, *, C, nch, n_heads, dk, dv):
    c = pl.program_id(1)

    @pl.when(c == 0)
    def _():
        st_ref[...] = s0_ref[...]

    x = jnp.dot(ag_ref[...], a2_ref[...], precision=HIGHEST, preferred_element_type=F32) + ab_ref[...]
    log_a = -_softplus(-x) * (1.0 / G_GATE_TAU)
    ti = lax.broadcasted_iota(jnp.int32, (C, C), 0)
    si = lax.broadcasted_iota(jnp.int32, (C, C), 1)
    tri = (si <= ti).astype(F32)
    b_all = jnp.dot(tri, log_a, precision=HIGHEST, preferred_element_type=F32)

    nt = (((1,), (1,)), ((), ()))
    tn = (((0,), (0,)), ((), ()))
    scale = float(dk) ** -0.5
    rowi = lax.broadcasted_iota(jnp.int32, (GLA_SUB, dk), 0)

    for h in range(n_heads):
        ksl = slice(h * dk, (h + 1) * dk)
        vsl = slice(h * dv, (h + 1) * dv)
        b = b_all[:, ksl]
        b_last = b[C - 1:C, :]
        q = q_ref[:, ksl] * scale
        k = k_ref[:, ksl]
        v = v_ref[:, vsl]
        vh = v.astype(BF16)
        st = st_ref[h]

        inter = lax.dot_general((q * jnp.exp(b)).astype(BF16), st.astype(BF16), nt,
                                preferred_element_type=F32)
        blocks = []
        for blk in range(C // GLA_SUB):
            t0 = blk * GLA_SUB
            rows = slice(t0, t0 + GLA_SUB)
            qb, kb, bb, vb = q[rows], k[rows], b[rows], v[rows]
            acc = inter[rows]
            if blk > 0:
                b0 = bb[0:1, :]
                qd = (qb * jnp.exp(bb - b0)).astype(BF16)
                kd = (k[0:t0] * jnp.exp(b0 - b[0:t0])).astype(BF16)
                sc = lax.dot_general(qd, kd, nt, preferred_element_type=F32)
                acc = acc + jnp.dot(sc.astype(BF16), vh[0:t0], preferred_element_type=F32)
            for s in range(GLA_SUB):
                e = jnp.where(rowi >= s, jnp.exp(bb - bb[s:s + 1, :]), 0.0)
                sc_s = jnp.sum(qb * e * kb[s:s + 1, :], axis=-1, keepdims=True)
                acc = acc + sc_s * vb[s:s + 1, :]
            blocks.append(acc)
        o = jnp.concatenate(blocks, axis=0) if len(blocks) > 1 else blocks[0]

        kdec = (k * jnp.exp(b_last - b)).astype(BF16)
        st_ref[h] = st * jnp.exp(b_last) + lax.dot_general(vh, kdec, tn, preferred_element_type=F32)

        ms = jnp.mean(o * o, axis=-1, keepdims=True)
        o_ref[:, vsl] = (o * lax.rsqrt(ms + NORM_EPS) * gn_ref[:, vsl]
                         * jax.nn.silu(rg_ref[:, vsl])).astype(o_ref.dtype)

    @pl.when(c == nch - 1)
    def _():
        sout_ref[...] = st_ref[...]


def gla(proj_g, proj_r, row_off, n_seq, T, s0t, lw, dims):
    GK, GV = dims['GK'], dims['GV']
    dk, dv = GK // G_HEADS, GV // G_HEADS
    C = min(SEQ_CHUNK, T)
    assert T % C == 0 and C % GLA_SUB == 0 and row_off % C == 0
    assert GV % GK == 0 and (2 * GK) % GV == 0
    nch = T // C
    rows = n_seq * T
    off_b = row_off // C

    def tile(width, col0):
        assert col0 % width == 0
        return pl.BlockSpec((C, width), lambda s, c: (off_b + s * nch + c, col0 // width))

    def const(shape):
        return pl.BlockSpec(shape, lambda s, c: tuple(0 for _ in shape))

    in_specs = [tile(GK, 0), tile(GK, GK), tile(GV, 2 * GK), tile(GV, 2 * GK + GV),
                tile(LANES, dims['o_ag']),
                const((LANES, GK)), const((1, GK)), const((1, GV)),
                pl.BlockSpec((None, G_HEADS, dv, dk), lambda s, c: (s, 0, 0, 0))]
    body = functools.partial(_gla_body, C=C, nch=nch, n_heads=G_HEADS, dk=dk, dv=dv)
    return pl.pallas_call(
        body,
        grid=(n_seq, nch),
        in_specs=in_specs,
        out_specs=[pl.BlockSpec((C, GV), lambda s, c: (s * nch + c, 0)),
                   pl.BlockSpec((None, G_HEADS, dv, dk), lambda s, c: (s, 0, 0, 0))],
        out_shape=[jax.ShapeDtypeStruct((rows, GV), BF16),
                   jax.ShapeDtypeStruct(s0t.shape, F32)],
        scratch_shapes=[pltpu.VMEM((G_HEADS, dv, dk), F32)],
        compiler_params=_params(("parallel", "arbitrary")),
        name="gla",
    )(proj_g, proj_g, proj_g, proj_g, proj_r, lw['gla_a2'], lw['gla_ab'], lw['gla_norm'], s0t)


def _hx_to_xh(w, n_heads):
    sh = w.shape[:-1]
    x = w.shape[-1] // n_heads
    return jnp.swapaxes(w.reshape(*sh, n_heads, x), -1, -2).reshape(*sh, n_heads * x)


def _xh_to_hx(w, n_heads):
    sh = w.shape[:-1]
    x = w.shape[-1] // n_heads
    return jnp.swapaxes(w.reshape(*sh, x, n_heads), -1, -2).reshape(*sh, n_heads * x)


def _rep_kh(w, n_heads):
    sh = w.shape[:-1]
    vv = LANES // n_heads
    x = jnp.swapaxes(w.reshape(*sh, n_heads, R_HEAD_DIM), -1, -2)
    x = jnp.broadcast_to(x[..., :, None, :], (*sh, R_HEAD_DIM, vv, n_heads))
    return x.reshape(*sh, R_HEAD_DIM * LANES)


def _pad_last(x, width):
    return jnp.pad(x, [(0, 0)] * (x.ndim - 1) + [(0, width - x.shape[-1])])


def _split_r_cols(x, RW, ld, la, lg):
    offs = [0, RW, 2 * RW, 3 * RW, 3 * RW + ld, 3 * RW + ld + la, 3 * RW + ld + la + lg]
    return [x[..., offs[i]:offs[i + 1]] for i in range(6)]


def _small_block(wd, ad, gd, lgp, tail=None):
    parts = [wd, ad, _pad_last(gd, lgp)]
    if tail is not None:
        parts.append(tail)
    return _pad_last(jnp.concatenate(parts, axis=-1), SMALL_W)


def _state_to_kernel(s, n_heads):
    n = s.shape[0]
    vv = LANES // n_heads
    g = R_HEAD_DIM // vv
    x = s.reshape(n, n_heads, g, vv, R_HEAD_DIM)
    return x.transpose(0, 2, 4, 3, 1).reshape(n, g, R_HEAD_DIM, LANES)


def _state_from_kernel(s, n_heads):
    n, g = s.shape[0], s.shape[1]
    vv = LANES // n_heads
    x = s.reshape(n, g, R_HEAD_DIM, vv, n_heads)
    return x.transpose(0, 4, 1, 3, 2).reshape(n, n_heads, R_HEAD_DIM, R_HEAD_DIM)


def kernel(x_prompt, x_sample, state_shift, state_rwkv, state_gla, p_prompt, p_sample, norm_mix_pre, norm_mix_post, norm_ffn_pre, norm_ffn_post, norm_ple, w_in, rwkv_mu, rwkv_w0, rwkv_w2, rwkv_a0, rwkv_a2, rwkv_g2, rwkv_k_k, rwkv_k_a, rwkv_r_k, rwkv_ln_w, rwkv_ln_b, gla_a2, gla_ab, gla_norm, w_proj_rwkv, w_proj_gla, w_out, ffn_gate, ffn_up, ffn_down, ple_proj, ple_gate):
    Bp, Tp, D = x_prompt.shape
    Bs, Ts, _ = x_sample.shape
    L = w_in.shape[0]
    RW = rwkv_w0.shape[-1]
    H = RW // R_HEAD_DIM
    assert LANES % H == 0 and R_HEAD_DIM % (LANES // H) == 0
    ld, la, lg = rwkv_w2.shape[1], rwkv_a2.shape[1], rwkv_g2.shape[1]
    lgp = _round_up(lg, LANES)
    GK, GV, GR = gla_a2.shape[-1], gla_norm.shape[-1], gla_a2.shape[1]
    FF = ffn_gate.shape[-1]
    Mp, Ms = Bp * Tp, Bs * Ts
    M = Mp + Ms
    groups = [(0, Bp, Tp), (Mp, Bs, Ts)]

    o_sm = 3 * RW
    NR = o_sm + SMALL_W
    NG = 2 * GK + 2 * GV
    o_ag_in_sm = ld + la + lgp
    assert o_ag_in_sm + LANES <= SMALL_W and o_sm % SMALL_W == 0 and GR <= LANES
    dims = dict(GK=GK, GV=GV, o_ag=o_sm + o_ag_in_sm)

    r_cols = 3 * RW + ld + la + lg
    g_cols = NG + GR
    w_r, w_k, w_v, w_wd, w_ad, w_gd = _split_r_cols(w_in[..., :r_cols], RW, ld, la, lg)
    w_ag = w_in[..., r_cols + NG:r_cols + g_cols]
    win_r = jnp.concatenate(
        [_hx_to_xh(w_r, H), _hx_to_xh(w_k, H), _hx_to_xh(w_v, H),
         _small_block(w_wd, w_ad, w_gd, lgp, w_ag)], axis=-1).astype(BF16)
    win_g = w_in[..., r_cols:r_cols + NG].astype(BF16)
    win_gate = w_in[..., r_cols + g_cols:].astype(BF16)
    assert win_r.shape[-1] == NR and win_gate.shape[-1] == 2 * D

    mu_r, mu_k, mu_v, mu_wd, mu_ad, mu_gd = _split_r_cols(rwkv_mu, RW, ld, la, lg)
    kq = LANES // H
    src = jnp.arange(LANES)
    dst = jnp.arange(kq * LANES)
    rep_mat = jnp.tile((src[:, None] == ((dst // LANES) * H + dst % H)[None, :]).astype(BF16), (3, 1))

    def layer_weights(i):
        return dict(
            mu_r=_hx_to_xh(mu_r[i], H)[None], mu_k=_hx_to_xh(mu_k[i], H)[None], mu_v=_hx_to_xh(mu_v[i], H)[None],
            mu_sm=_small_block(mu_wd[i], mu_ad[i], mu_gd[i], lgp)[None],
            w0=_hx_to_xh(rwkv_w0[i], H)[None], a0=_hx_to_xh(rwkv_a0[i], H)[None],
            k_k=_hx_to_xh(rwkv_k_k[i], H)[None], k_a=_hx_to_xh(rwkv_k_a[i], H)[None],
            r_k=_hx_to_xh(rwkv_r_k[i].reshape(RW), H)[None],
            w2=_hx_to_xh(rwkv_w2[i], H).astype(BF16), a2=_hx_to_xh(rwkv_a2[i], H).astype(BF16),
            g2=jnp.pad(_hx_to_xh(rwkv_g2[i], H), ((0, lgp - lg), (0, 0))).astype(BF16),
            rep=rep_mat, sm_off=o_sm, ld=ld, la=la, lgp=lgp,
            ln_w=_hx_to_xh(rwkv_ln_w[i], H)[None], ln_b=_hx_to_xh(rwkv_ln_b[i], H)[None],
            gla_a2=jnp.pad(gla_a2[i], ((0, LANES - GR), (0, 0))), gla_ab=gla_ab[i][None],
            gla_norm=gla_norm[i][None],
            w_pr=jnp.swapaxes(w_proj_rwkv[i].reshape(H, R_HEAD_DIM, D), 0, 1).reshape(RW, D).astype(BF16),
            w_pg=w_proj_gla[i].astype(BF16), w_out=w_out[i].astype(BF16),
            ffn_gate=ffn_gate[i].astype(BF16), ffn_up=ffn_up[i].astype(BF16), ffn_down=ffn_down[i].astype(BF16),
            ple_proj=ple_proj[i].astype(BF16), ple_gate=ple_gate[i].astype(BF16))

    def shift_state(s):
        s_r, s_k, s_v, s_wd, s_ad, s_gd = _split_r_cols(s, RW, ld, la, lg)
        return dict(r=_hx_to_xh(s_r, H)[:, None], k=_hx_to_xh(s_k, H)[:, None], v=_hx_to_xh(s_v, H)[:, None],
                    sm=_small_block(s_wd, s_ad, s_gd, lgp)[:, None])

    def shift_from_rows(rows):
        sm = rows[:, o_sm:o_sm + SMALL_W]
        return jnp.concatenate(
            [_xh_to_hx(rows[:, 0:RW], H), _xh_to_hx(rows[:, RW:2 * RW], H), _xh_to_hx(rows[:, 2 * RW:3 * RW], H),
             sm[:, 0:ld], sm[:, ld:ld + la], sm[:, ld + la:ld + la + lg]], axis=-1)

    h = jnp.concatenate([x_prompt.reshape(Mp, D), x_sample.reshape(Ms, D)], axis=0)
    p_all = jnp.concatenate([p_prompt.reshape(L, Mp, -1), p_sample.reshape(L, Ms, -1)], axis=1).astype(BF16)
    zeros_shift = jnp.zeros((Bp, r_cols), F32)
    zeros_rwkv = jnp.zeros((Bp, H, R_HEAD_DIM, R_HEAD_DIM), F32)
    zeros_gla = jnp.zeros((Bp, G_HEADS, GK // G_HEADS, GV // G_HEADS), F32)

    new_shift = [[], []]
    new_rwkv = [[], []]
    new_gla = [[], []]
    for i in range(L):
        lw = layer_weights(i)
        z = norm_cast(h, norm_mix_pre[i])
        proj_r = fused_mm([z], [(0, win_r[i])], [], _ep_identity, NR, F32, "in_proj_r")
        proj_g = fused_mm([z], [(0, win_g[i])], [], _ep_identity, NG, F32, "in_proj_g")
        proj_gate = fused_mm([z], [(0, win_gate[i])], [], _ep_identity, 2 * D, F32, "in_proj_gate")

        o_r_parts, o_g_parts = [], []
        states = [(zeros_shift, zeros_rwkv, zeros_gla), (state_shift[i], state_rwkv[i], state_gla[i])]
        for gi, (row_off, n_seq, T) in enumerate(groups):
            s_shift, s_rwkv, s_gla = states[gi]
            nkk, wr, dec, kka, kmod, c, vmix, g = rwkv_prep(proj_r, row_off, n_seq, T, shift_state(s_shift), lw)
            y, s_new = rwkv_rec(nkk, wr, dec, kka, kmod, vmix, c, _state_to_kernel(s_rwkv, H), n_seq, T)
            o_r_parts.append(rwkv_post(y.reshape(n_seq * T, RW), vmix, g, c, lw['ln_w'], lw['ln_b'], H))
            o_g, s_gla_new = gla(proj_g, proj_r, row_off, n_seq, T, jnp.swapaxes(s_gla, -1, -2), lw, dims)
            o_g_parts.append(o_g)
            last_rows = proj_r[row_off:row_off + n_seq * T].reshape(n_seq, T, NR)[:, T - 1]
            new_shift[gi].append(shift_from_rows(last_rows))
            new_rwkv[gi].append(_state_from_kernel(s_new, H))
            new_gla[gi].append(jnp.swapaxes(s_gla_new, -1, -2))
        o_r = jnp.concatenate(o_r_parts, axis=0)
        o_g = jnp.concatenate(o_g_parts, axis=0)

        tn = _pick_tile(D, COL_TILE, LANES)
        merged = fused_mm([o_r, o_g], [(0, lw['w_pr']), (1, lw['w_pg'])],
                          [(proj_gate, 0), (proj_gate, D // tn)],
                          _ep_merge, D, BF16, "merge", tm_target=ROW_TILE // 2)
        mix = fused_mm([merged], [(0, lw['w_out'])], [], _ep_identity, D, F32, "out_proj")
        h1, u = resid_norm(h, mix, norm_mix_post[i], norm_ffn_pre[i])
        t = fused_mm([u], [(0, lw['ffn_gate']), (0, lw['ffn_up'])], [], _ep_swiglu, FF, BF16, "ffn_up",
                     tm_target=ROW_TILE // 2, tn=COL_TILE)
        f = fused_mm([t], [(0, lw['ffn_down'])], [], _ep_identity, D, F32, "ffn_down",
                     tm_target=ROW_TILE // 2, tn=COL_TILE // 2)
        h2, pn = resid_norm(h1, f, norm_ffn_post[i], norm_ple[i])
        h = ple_mm(pn, lw['ple_gate'], p_all[i], lw['ple_proj'], h2)

    y_prompt = h[:Mp].reshape(Bp, Tp, D)
    y_sample = h[Mp:].reshape(Bs, Ts, D)
    return (y_prompt, y_sample,
            jnp.stack(new_shift[0]), jnp.stack(new_rwkv[0]), jnp.stack(new_gla[0]),
            jnp.stack(new_shift[1]), jnp.stack(new_rwkv[1]), jnp.stack(new_gla[1]))
```

```python
import functools

import jax
import jax.numpy as jnp
from jax import lax
from jax.experimental import pallas as pl
from jax.experimental.pallas import tpu as pltpu

F32 = jnp.float32
BF16 = jnp.bfloat16
HIGHEST = lax.Precision.HIGHEST

NORM_EPS = 1e-6
R_GN_EPS = 64e-5
R_HEAD_DIM = 64
G_HEADS = 8
G_GATE_TAU = 16.0

LANES = 128
SUBLANES = 8
VMEM_LIMIT = 56 * 1024 * 1024

ROW_TILE = 1536
COL_TILE = 512
EW_ROWS = 256
SEQ_CHUNK = 64
GLA_SUB = 16
SMALL_W = 1024

G_MIX_PRE, G_MIX_POST, G_FFN_PRE, G_FFN_POST, G_PLE = range(5)
V_MU_R, V_MU_K, V_MU_V, V_W0, V_A0, V_KK, V_KA, V_RK = range(8)


def _round_up(x, m):
    return (x + m - 1) // m * m


def _pick_tile(n, target, mult=SUBLANES):
    best = None
    for t in range(mult, min(n, target) + 1, mult):
        if n % t == 0:
            best = t
    assert best is not None, (n, target)
    return best


def _params(sem):
    return pltpu.CompilerParams(dimension_semantics=sem, vmem_limit_bytes=VMEM_LIMIT)


def _rms(x, gain):
    ms = jnp.mean(x * x, axis=-1, keepdims=True)
    return x * lax.rsqrt(ms + NORM_EPS) * gain


def _softplus(x):
    return jnp.maximum(x, 0.0) + jnp.log(1.0 + jnp.exp(-jnp.abs(x)))


def _gain_spec(D, layer, row):
    return pl.BlockSpec((None, None, 1, D), lambda *_: (layer, row, 0, 0))


def _norm_cast_body(x_ref, g_ref, o_ref):
    o_ref[...] = _rms(x_ref[...], g_ref[...]).astype(o_ref.dtype)


def norm_cast(x, gains, layer, row):
    M, D = x.shape
    tm = _pick_tile(M, EW_ROWS)
    return pl.pallas_call(
        _norm_cast_body,
        grid=(M // tm,),
        in_specs=[pl.BlockSpec((tm, D), lambda i: (i, 0)), _gain_spec(D, layer, row)],
        out_specs=pl.BlockSpec((tm, D), lambda i: (i, 0)),
        out_shape=jax.ShapeDtypeStruct((M, D), BF16),
        compiler_params=_params(("parallel",)),
        name="norm_cast",
    )(x, gains)


def _resid_norm_body(h_ref, x_ref, gpost_ref, gpre_ref, h1_ref, u_ref):
    h1 = h_ref[...] + _rms(x_ref[...], gpost_ref[...])
    h1_ref[...] = h1
    u_ref[...] = _rms(h1, gpre_ref[...]).astype(u_ref.dtype)


def resid_norm(h, x, gains, layer, row_post, row_pre):
    M, D = h.shape
    tm = _pick_tile(M, EW_ROWS)
    row = pl.BlockSpec((tm, D), lambda i: (i, 0))
    return pl.pallas_call(
        _resid_norm_body,
        grid=(M // tm,),
        in_specs=[row, row, _gain_spec(D, layer, row_post), _gain_spec(D, layer, row_pre)],
        out_specs=[row, row],
        out_shape=[jax.ShapeDtypeStruct((M, D), F32), jax.ShapeDtypeStruct((M, D), BF16)],
        compiler_params=_params(("parallel",)),
        name="resid_norm",
    )(h, x, gains, gains)


def _mm_body(*refs, act_of, n_acts, n_extra, epilogue):
    n_w = len(act_of)
    a_refs = refs[:n_acts]
    w_refs = refs[n_acts:n_acts + n_w]
    e_refs = refs[n_acts + n_w:n_acts + n_w + n_extra]
    o_ref = refs[n_acts + n_w + n_extra]
    acts = [a[...] for a in a_refs]
    dots = [jnp.dot(acts[ai], w[...], preferred_element_type=F32) for ai, w in zip(act_of, w_refs)]
    o_ref[...] = epilogue(dots, [e[...] for e in e_refs]).astype(o_ref.dtype)


def fused_mm(acts, weights, layer, extras, epilogue, n_out, out_dtype, name, tm_target=None, tn=None):
    M = acts[0].shape[0]
    tm = _pick_tile(M, ROW_TILE if tm_target is None else min(tm_target, ROW_TILE))
    tn = _pick_tile(n_out, COL_TILE, LANES) if tn is None else tn
    grid = (M // tm, pl.cdiv(n_out, tn))
    in_specs = ([pl.BlockSpec((tm, a.shape[1]), lambda i, j: (i, 0)) for a in acts]
                + [pl.BlockSpec((None, w.shape[1], tn), lambda i, j: (layer, 0, j)) for _, w in weights]
                + [pl.BlockSpec((tm, tn), functools.partial(lambda i, j, off: (i, j + off), off=off))
                   for _, off in extras])
    body = functools.partial(_mm_body, act_of=tuple(a for a, _ in weights), n_acts=len(acts),
                             n_extra=len(extras), epilogue=epilogue)
    return pl.pallas_call(
        body,
        grid=grid,
        in_specs=in_specs,
        out_specs=pl.BlockSpec((tm, tn), lambda i, j: (i, j)),
        out_shape=jax.ShapeDtypeStruct((M, n_out), out_dtype),
        compiler_params=_params(("parallel", "parallel")),
        name=name,
    )(*acts, *[w for _, w in weights], *[e for e, _ in extras])


def _ep_identity(dots, extras):
    return dots[0]


def _ep_merge(dots, extras):
    return jax.nn.sigmoid(extras[0]) * dots[0] + jax.nn.sigmoid(extras[1]) * dots[1]


def _ep_swiglu(dots, extras):
    return jax.nn.silu(dots[0]) * dots[1]


def _ep_ple(dots, extras):
    return extras[0] + jax.nn.sigmoid(dots[0]) * dots[1]


def _shift_mix(x, prev_first, mu):
    rolled = pltpu.roll(x, 1, axis=0)
    row = lax.broadcasted_iota(jnp.int32, x.shape, 0)
    prev = jnp.where(row == 0, prev_first, rolled)
    return x + (prev - x) * mu


def _head_sum(x, n_heads):
    s = x[:, 0:LANES]
    for b in range(1, x.shape[1] // LANES):
        s = s + x[:, b * LANES:(b + 1) * LANES]
    shift = LANES // 2
    while shift >= n_heads:
        s = s + pltpu.roll(s, shift, axis=1)
        shift //= 2
    return s


def _tile_cols(x, n):
    return jnp.concatenate([x] * n, axis=1)


def _replicate(x, rep, out_ref):
    rows = x.shape[0]
    n_blk = x.shape[1] // LANES
    grp = rep.shape[1]
    xs = jnp.concatenate([x[:, b * LANES:(b + 1) * LANES] for b in range(n_blk)], axis=0)
    hi = xs.astype(BF16)
    rest = xs - hi.astype(F32)
    mid = rest.astype(BF16)
    lo = (rest - mid.astype(F32)).astype(BF16)
    y = jnp.dot(jnp.concatenate([hi, mid, lo], axis=1), rep, preferred_element_type=F32)
    for b in range(n_blk):
        out_ref[:, b * grp:(b + 1) * grp] = y[b * rows:(b + 1) * rows, :]


def _rwkv_prep_body(r_ref, k_ref, v_ref, sm_ref, rp_ref, kp_ref, vp_ref, smp_ref,
                    rs_ref, ks_ref, vs_ref, sms_ref, vec_ref, musm_ref,
                    w2_ref, a2_ref, g2_ref, rep_ref,
                    nkk_ref, wr_ref, kka_ref, kmod_ref, pend_ref, c_ref, vmix_ref, g_ref,
                    *, ld, la, lgp, n_heads):
    j = pl.program_id(1)
    first = j == 0

    def prev_row(p_ref, s_ref):
        return jnp.where(first, s_ref[...], p_ref[SUBLANES - 1:SUBLANES, :])

    def vec(row):
        return vec_ref[row:row + 1, :]

    r = _shift_mix(r_ref[...], prev_row(rp_ref, rs_ref), vec(V_MU_R))
    k = _shift_mix(k_ref[...], prev_row(kp_ref, ks_ref), vec(V_MU_K))
    vmix_ref[...] = _shift_mix(v_ref[...], prev_row(vp_ref, vs_ref), vec(V_MU_V))
    sm = _shift_mix(sm_ref[...], prev_row(smp_ref, sms_ref), musm_ref[...])

    wd = jnp.tanh(sm[:, 0:ld]).astype(BF16)
    ad = sm[:, ld:ld + la].astype(BF16)
    gd = jax.nn.sigmoid(sm[:, ld + la:ld + la + lgp]).astype(BF16)
    g_ref[...] = jnp.dot(gd, g2_ref[...], preferred_element_type=F32)

    tc = r.shape[0]
    n_blk = r.shape[1] // LANES
    w = -_softplus(-(vec(V_W0) + jnp.dot(wd, w2_ref[...], preferred_element_type=F32))) - 0.5
    neg_log_decay = jnp.exp(w)
    a = jax.nn.sigmoid(vec(V_A0) + jnp.dot(ad, a2_ref[...], preferred_element_type=F32))
    kk = k * vec(V_KK)
    inv_norm = 1.0 / jnp.maximum(jnp.sqrt(_head_sum(kk * kk, n_heads)), 1e-12)
    kkn = kk * _tile_cols(inv_norm, n_blk)
    kmod = k * (1.0 + (a - 1.0) * vec(V_KA))
    kka = kkn * a
    rk = r * kmod
    c_ref[:, 0:LANES] = _head_sum(kka * r, n_heads)
    c_ref[:, LANES:2 * LANES] = _head_sum(rk, n_heads)
    c_ref[:, 2 * LANES:3 * LANES] = _head_sum(rk * vec(V_RK), n_heads)

    ti = lax.broadcasted_iota(jnp.int32, (tc, tc), 0)
    si = lax.broadcasted_iota(jnp.int32, (tc, tc), 1)
    tri = (si <= ti).astype(F32)
    log_p = -jnp.dot(tri, neg_log_decay, precision=HIGHEST, preferred_element_type=F32)
    p_incl = jnp.exp(log_p)
    p_prev = jnp.exp(log_p + neg_log_decay)
    inv_p = jnp.exp(-log_p)

    rep = rep_ref[...]
    _replicate(-kkn * p_prev, rep, nkk_ref)
    _replicate(r * p_incl, rep, wr_ref)
    _replicate(kka * inv_p, rep, kka_ref)
    _replicate(kmod * inv_p, rep, kmod_ref)
    _replicate(p_incl[tc - SUBLANES:tc, :], rep, pend_ref)


def rwkv_prep(proj_r, row_off, n_seq, T, shift, shift_layer, lw, layer):
    RW = lw['vec'].shape[-1]
    rep_w = R_HEAD_DIM * LANES
    tc = min(SEQ_CHUNK, T)
    assert T % tc == 0 and row_off % tc == 0 and tc % SUBLANES == 0
    nch = T // tc
    rows = n_seq * T
    off_b = row_off // tc
    sm_col = 3 * RW // SMALL_W
    ld, la, lgp = lw['ld'], lw['la'], lw['lgp']

    def tile(width, colblk):
        return pl.BlockSpec((tc, width), lambda s, j: (off_b + s * nch + j, colblk))

    def prev(width, colblk):
        per = tc // SUBLANES
        return pl.BlockSpec((SUBLANES, width),
                            lambda s, j: (jnp.maximum((off_b + s * nch + j) * per - 1, 0), colblk))

    def state(width, colblk):
        return pl.BlockSpec((None, None, 1, width), lambda s, j: (shift_layer, s, 0, colblk))

    def stacked(arr):
        return pl.BlockSpec((None,) + arr.shape[1:], lambda s, j: (layer,) + (0,) * (arr.ndim - 1))

    def out(width):
        return pl.BlockSpec((tc, width), lambda s, j: (s * nch + j, 0))

    in_specs = [tile(RW, 0), tile(RW, 1), tile(RW, 2), tile(SMALL_W, sm_col),
                prev(RW, 0), prev(RW, 1), prev(RW, 2), prev(SMALL_W, sm_col),
                state(RW, 0), state(RW, 1), state(RW, 2), state(SMALL_W, sm_col),
                stacked(lw['vec']), stacked(lw['mu_sm']),
                stacked(lw['w2']), stacked(lw['a2']), stacked(lw['g2']),
                pl.BlockSpec(lw['rep'].shape, lambda s, j: (0, 0))]
    out_specs = ([out(rep_w)] * 4
                 + [pl.BlockSpec((SUBLANES, rep_w), lambda s, j: (s * nch + j, 0)),
                    out(3 * LANES), out(RW), out(RW)])
    out_shape = ([jax.ShapeDtypeStruct((rows, rep_w), F32)] * 4
                 + [jax.ShapeDtypeStruct((n_seq * nch * SUBLANES, rep_w), F32),
                    jax.ShapeDtypeStruct((rows, 3 * LANES), F32),
                    jax.ShapeDtypeStruct((rows, RW), F32), jax.ShapeDtypeStruct((rows, RW), F32)])
    body = functools.partial(_rwkv_prep_body, ld=ld, la=la, lgp=lgp, n_heads=RW // R_HEAD_DIM)
    return pl.pallas_call(
        body,
        grid=(n_seq, nch),
        in_specs=in_specs,
        out_specs=out_specs,
        out_shape=out_shape,
        compiler_params=_params(("parallel", "arbitrary")),
        name="rwkv_prep",
    )(proj_r, proj_r, proj_r, proj_r, proj_r, proj_r, proj_r, proj_r,
      shift, shift, shift, shift, lw['vec'], lw['mu_sm'], lw['w2'], lw['a2'], lw['g2'], lw['rep'])


def _rwkv_rec_body(nkk_ref, wr_ref, kka_ref, kmod_ref, pend_ref, v_ref, c_ref, s0_ref,
                   y_ref, sout_ref, s_ref, *, tc, n_grp, nch):
    j = pl.program_id(1)

    @pl.when(j == 0)
    def _():
        s_ref[...] = s0_ref[...]

    def step(t, carry):
        nkk = nkk_ref[t]
        wr = wr_ref[t]
        kka = kka_ref[t]
        kmod = kmod_ref[t]
        crow = c_ref[pl.ds(t, 1), :]
        c1 = crow[:, 0:LANES]
        c2 = crow[:, LANES:2 * LANES]
        for g in range(n_grp):
            sg = s_ref[g]
            sa = jnp.sum(sg * nkk, axis=0, keepdims=True)
            yp = jnp.sum(sg * wr, axis=0, keepdims=True)
            vg = v_ref[t, g:g + 1, :]
            s_ref[g] = sg + sa * kka + vg * kmod
            y_ref[t, g:g + 1, :] = yp + sa * c1 + vg * c2
        return carry

    lax.fori_loop(0, tc, step, 0)

    p_end = pend_ref[SUBLANES - 1]
    for g in range(n_grp):
        s_ref[g] = s_ref[g] * p_end

    @pl.when(j == nch - 1)
    def _():
        sout_ref[...] = s_ref[...]


def rwkv_rec(nkk, wr, kka, kmod, pend, vmix, c, s0, s0_layer, n_seq, T):
    rows = n_seq * T
    n_grp = s0.shape[2]
    v_rows = n_grp
    tc = min(SEQ_CHUNK, T)
    nch = T // tc
    k3 = lambda x: x.reshape(x.shape[0], R_HEAD_DIM, LANES)
    kspec = pl.BlockSpec((tc, R_HEAD_DIM, LANES), lambda s, j: (s * nch + j, 0, 0))
    pspec = pl.BlockSpec((SUBLANES, R_HEAD_DIM, LANES), lambda s, j: (s * nch + j, 0, 0))
    vspec = pl.BlockSpec((tc, v_rows, LANES), lambda s, j: (s * nch + j, 0, 0))
    tile = (n_grp, R_HEAD_DIM, LANES)
    s_in = pl.BlockSpec((None, None) + tile, lambda s, j: (s0_layer, s, 0, 0, 0))
    s_out = pl.BlockSpec((None,) + tile, lambda s, j: (s, 0, 0, 0))
    body = functools.partial(_rwkv_rec_body, tc=tc, n_grp=n_grp, nch=nch)
    return pl.pallas_call(
        body,
        grid=(n_seq, nch),
        in_specs=[kspec] * 4 + [pspec, vspec, pl.BlockSpec((tc, 3 * LANES), lambda s, j: (s * nch + j, 0)), s_in],
        out_specs=[vspec, s_out],
        out_shape=[jax.ShapeDtypeStruct((rows, v_rows, LANES), F32),
                   jax.ShapeDtypeStruct(s0.shape[1:], F32)],
        scratch_shapes=[pltpu.VMEM(tile, F32)],
        compiler_params=_params(("parallel", "arbitrary")),
        name="rwkv_rec",
    )(k3(nkk), k3(wr), k3(kka), k3(kmod), k3(pend), vmix.reshape(rows, v_rows, LANES), c, s0)


def _rwkv_post_body(y_ref, v_ref, g_ref, c_ref, ln_ref, o_ref, *, n_grp, n_heads):
    def head_sum(x):
        return _tile_cols(_head_sum(x, n_heads), n_grp)

    y = y_ref[...]
    inv_n = 1.0 / R_HEAD_DIM
    mu = head_sum(y) * inv_n
    d = y - mu
    var = head_sum(d * d) * inv_n
    yln = d * lax.rsqrt(var + R_GN_EPS) * ln_ref[0:1, :] + ln_ref[1:2, :]
    c3 = _tile_cols(c_ref[:, 2 * LANES:3 * LANES], n_grp)
    o_ref[...] = ((yln + c3 * v_ref[...]) * g_ref[...]).astype(o_ref.dtype)


def rwkv_post(y, vmix, g, c, ln, layer, n_heads):
    M, RW = y.shape
    tm = _pick_tile(M, EW_ROWS)
    row = pl.BlockSpec((tm, RW), lambda i: (i, 0))
    body = functools.partial(_rwkv_post_body, n_grp=RW // LANES, n_heads=n_heads)
    return pl.pallas_call(
        body,
        grid=(M // tm,),
        in_specs=[row, row, row, pl.BlockSpec((tm, 3 * LANES), lambda i: (i, 0)),
                  pl.BlockSpec((None, 2, RW), lambda i: (layer, 0, 0))],
        out_specs=row,
        out_shape=jax.ShapeDtypeStruct((M, RW), BF16),
        compiler_params=_params(("parallel",)),
        name="rwkv_post",
    )(y, vmix, g, c, ln)


def _gla_body(q_ref, k_ref, v_ref, rg_ref, ag_ref, a2_ref, ab_ref, gn_ref, s0_ref,
              o_ref, sout_ref, st_ref, *, C, nch, n_heads, dk, dv):
    c = pl.program_id(1)

    @pl.when(c == 0)
    def _():
        st_ref[...] = s0_ref[...]

    x = jnp.dot(ag_ref[...], a2_ref[...], precision=HIGHEST, preferred_element_type=F32) + ab_ref[...]
    log_a = -_softplus(-x) * (1.0 / G_GATE_TAU)
    ti = lax.broadcasted_iota(jnp.int32, (C, C), 0)
    si = lax.broadcasted_iota(jnp.int32, (C, C), 1)
    tri = (si <= ti).astype(F32)
    b_all = jnp.dot(tri, log_a, precision=HIGHEST, preferred_element_type=F32)

    nt = (((1,), (1,)), ((), ()))
    tn = (((0,), (0,)), ((), ()))
    scale = float(dk) ** -0.5
    rowi = lax.broadcasted_iota(jnp.int32, (GLA_SUB, dk), 0)

    for h in range(n_heads):
        ksl = slice(h * dk, (h + 1) * dk)
        vsl = slice(h * dv, (h + 1) * dv)
        b = b_all[:, ksl]
        b_last = b[C - 1:C, :]
        q = q_ref[:, ksl] * scale
        k = k_ref[:, ksl]
        v = v_ref[:, vsl]
        vh = v.astype(BF16)
        st = st_ref[h]

        inter = lax.dot_general((q * jnp.exp(b)).astype(BF16), st.astype(BF16), nt,
                                preferred_element_type=F32)
        blocks = []
        for blk in range(C // GLA_SUB):
            t0 = blk * GLA_SUB
            rows = slice(t0, t0 + GLA_SUB)
            qb, kb, bb, vb = q[rows], k[rows], b[rows], v[rows]
            acc = inter[rows]
            if blk > 0:
                b0 = bb[0:1, :]
                qd = (qb * jnp.exp(bb - b0)).astype(BF16)
                kd = (k[0:t0] * jnp.exp(b0 - b[0:t0])).astype(BF16)
                sc = lax.dot_general(qd, kd, nt, preferred_element_type=F32)
                acc = acc + jnp.dot(sc.astype(BF16), vh[0:t0], preferred_element_type=F32)
            for s in range(GLA_SUB):
                e = jnp.where(rowi >= s, jnp.exp(bb - bb[s:s + 1, :]), 0.0)
                sc_s = jnp.sum(qb * e * kb[s:s + 1, :], axis=-1, keepdims=True)
                acc = acc + sc_s * vb[s:s + 1, :]
            blocks.append(acc)
        o = jnp.concatenate(blocks, axis=0) if len(blocks) > 1 else blocks[0]

        kdec = (k * jnp.exp(b_last - b)).astype(BF16)
        st_ref[h] = st * jnp.exp(b_last) + lax.dot_general(vh, kdec, tn, preferred_element_type=F32)

        ms = jnp.mean(o * o, axis=-1, keepdims=True)
        o_ref[:, vsl] = (o * lax.rsqrt(ms + NORM_EPS) * gn_ref[:, vsl]
                         * jax.nn.silu(rg_ref[:, vsl])).astype(o_ref.dtype)

    @pl.when(c == nch - 1)
    def _():
        sout_ref[...] = st_ref[...]


def gla(proj_g, proj_r, row_off, n_seq, T, s0t, s0_layer, lw, layer, o_ag):
    GK, GV = lw['gla_ab'].shape[-1], lw['gla_norm'].shape[-1]
    dk, dv = GK // G_HEADS, GV // G_HEADS
    C = min(SEQ_CHUNK, T)
    assert T % C == 0 and C % GLA_SUB == 0 and row_off % C == 0
    assert GV % GK == 0 and (2 * GK) % GV == 0 and o_ag % LANES == 0
    nch = T // C
    rows = n_seq * T
    off_b = row_off // C

    def tile(width, col0):
        assert col0 % width == 0
        return pl.BlockSpec((C, width), lambda s, c: (off_b + s * nch + c, col0 // width))

    def stacked(arr):
        return pl.BlockSpec((None,) + arr.shape[1:], lambda s, c: (layer,) + (0,) * (arr.ndim - 1))

    in_specs = [tile(GK, 0), tile(GK, GK), tile(GV, 2 * GK), tile(GV, 2 * GK + GV),
                tile(LANES, o_ag),
                stacked(lw['gla_a2']), stacked(lw['gla_ab']), stacked(lw['gla_norm']),
                pl.BlockSpec((None, None, G_HEADS, dv, dk), lambda s, c: (s0_layer, s, 0, 0, 0))]
    body = functools.partial(_gla_body, C=C, nch=nch, n_heads=G_HEADS, dk=dk, dv=dv)
    return pl.pallas_call(
        body,
        grid=(n_seq, nch),
        in_specs=in_specs,
        out_specs=[pl.BlockSpec((C, GV), lambda s, c: (s * nch + c, 0)),
                   pl.BlockSpec((None, G_HEADS, dv, dk), lambda s, c: (s, 0, 0, 0))],
        out_shape=[jax.ShapeDtypeStruct((rows, GV), BF16),
                   jax.ShapeDtypeStruct(s0t.shape[1:], F32)],
        scratch_shapes=[pltpu.VMEM((G_HEADS, dv, dk), F32)],
        compiler_params=_params(("parallel", "arbitrary")),
        name="gla",
    )(proj_g, proj_g, proj_g, proj_g, proj_r, lw['gla_a2'], lw['gla_ab'], lw['gla_norm'], s0t)


def _hx_to_xh(w, n_heads):
    sh = w.shape[:-1]
    x = w.shape[-1] // n_heads
    return jnp.swapaxes(w.reshape(*sh, n_heads, x), -1, -2).reshape(*sh, n_heads * x)


def _xh_to_hx(w, n_heads):
    sh = w.shape[:-1]
    x = w.shape[-1] // n_heads
    return jnp.swapaxes(w.reshape(*sh, x, n_heads), -1, -2).reshape(*sh, n_heads * x)


def _pad_last(x, width):
    return jnp.pad(x, [(0, 0)] * (x.ndim - 1) + [(0, width - x.shape[-1])])


def _split_r_cols(x, RW, ld, la, lg):
    offs = [0, RW, 2 * RW, 3 * RW, 3 * RW + ld, 3 * RW + ld + la, 3 * RW + ld + la + lg]
    return [x[..., offs[i]:offs[i + 1]] for i in range(6)]


def _small_block(wd, ad, gd, lgp, tail=None):
    parts = [wd, ad, _pad_last(gd, lgp)]
    if tail is not None:
        parts.append(tail)
    return _pad_last(jnp.concatenate(parts, axis=-1), SMALL_W)


def _r_cols_to_kernel(x, RW, ld, la, lg, lgp, n_heads, tail=None):
    x_r, x_k, x_v, x_wd, x_ad, x_gd = _split_r_cols(x, RW, ld, la, lg)
    return jnp.concatenate([_hx_to_xh(x_r, n_heads), _hx_to_xh(x_k, n_heads), _hx_to_xh(x_v, n_heads),
                            _small_block(x_wd, x_ad, x_gd, lgp, tail)], axis=-1)


def _r_cols_from_kernel(x, RW, ld, la, lg, n_heads):
    sm = x[..., 3 * RW:]
    return jnp.concatenate(
        [_xh_to_hx(x[..., 0:RW], n_heads), _xh_to_hx(x[..., RW:2 * RW], n_heads),
         _xh_to_hx(x[..., 2 * RW:3 * RW], n_heads),
         sm[..., 0:ld], sm[..., ld:ld + la], sm[..., ld + la:ld + la + lg]], axis=-1)


def _state_to_kernel(s, n_heads):
    lead = s.shape[:-3]
    n = len(lead)
    vv = LANES // n_heads
    g = R_HEAD_DIM // vv
    x = s.reshape(*lead, n_heads, g, vv, R_HEAD_DIM)
    x = x.transpose(*range(n), n + 1, n + 3, n + 2, n)
    return x.reshape(*lead, g, R_HEAD_DIM, LANES)


def _state_from_kernel(s, n_heads):
    lead = s.shape[:-3]
    n = len(lead)
    g = s.shape[-3]
    vv = LANES // n_heads
    x = s.reshape(*lead, g, R_HEAD_DIM, vv, n_heads)
    x = x.transpose(*range(n), n + 3, n, n + 2, n + 1)
    return x.reshape(*lead, n_heads, R_HEAD_DIM, R_HEAD_DIM)


def kernel(x_prompt, x_sample, state_shift, state_rwkv, state_gla, p_prompt, p_sample, norm_mix_pre, norm_mix_post, norm_ffn_pre, norm_ffn_post, norm_ple, w_in, rwkv_mu, rwkv_w0, rwkv_w2, rwkv_a0, rwkv_a2, rwkv_g2, rwkv_k_k, rwkv_k_a, rwkv_r_k, rwkv_ln_w, rwkv_ln_b, gla_a2, gla_ab, gla_norm, w_proj_rwkv, w_proj_gla, w_out, ffn_gate, ffn_up, ffn_down, ple_proj, ple_gate):
    Bp, Tp, D = x_prompt.shape
    Bs, Ts, _ = x_sample.shape
    L = w_in.shape[0]
    RW = rwkv_w0.shape[-1]
    H = RW // R_HEAD_DIM
    assert LANES % H == 0 and R_HEAD_DIM % (LANES // H) == 0
    ld, la, lg = rwkv_w2.shape[1], rwkv_a2.shape[1], rwkv_g2.shape[1]
    lgp = _round_up(lg, LANES)
    GK, GV, GR = gla_a2.shape[-1], gla_norm.shape[-1], gla_a2.shape[1]
    FF = ffn_gate.shape[-1]
    Mp, Ms = Bp * Tp, Bs * Ts
    groups = [(0, Bp, Tp), (Mp, Bs, Ts)]

    NR = 3 * RW + SMALL_W
    NG = 2 * GK + 2 * GV
    o_ag = 3 * RW + ld + la + lgp
    assert o_ag + LANES <= NR and (3 * RW) % SMALL_W == 0 and GR <= LANES
    r_cols = 3 * RW + ld + la + lg
    g_cols = NG + GR
    to_kernel = functools.partial(_r_cols_to_kernel, RW=RW, ld=ld, la=la, lg=lg, lgp=lgp, n_heads=H)
    win_r = to_kernel(w_in[..., :r_cols], tail=w_in[..., r_cols + NG:r_cols + g_cols]).astype(BF16)
    win_g = w_in[..., r_cols:r_cols + NG].astype(BF16)
    win_gate = w_in[..., r_cols + g_cols:].astype(BF16)
    assert win_r.shape[-1] == NR and win_gate.shape[-1] == 2 * D

    mu_k = to_kernel(rwkv_mu)
    kq = LANES // H
    src = jnp.arange(LANES)
    dst = jnp.arange(kq * LANES)
    rep_mat = jnp.tile((src[:, None] == ((dst // LANES) * H + dst % H)[None, :]).astype(BF16), (3, 1))
    lw = dict(
        vec=jnp.stack([mu_k[:, 0:RW], mu_k[:, RW:2 * RW], mu_k[:, 2 * RW:3 * RW],
                       _hx_to_xh(rwkv_w0, H), _hx_to_xh(rwkv_a0, H), _hx_to_xh(rwkv_k_k, H),
                       _hx_to_xh(rwkv_k_a, H), _hx_to_xh(rwkv_r_k.reshape(L, RW), H)], axis=1),
        mu_sm=mu_k[:, None, 3 * RW:],
        w2=_hx_to_xh(rwkv_w2, H).astype(BF16), a2=_hx_to_xh(rwkv_a2, H).astype(BF16),
        g2=jnp.pad(_hx_to_xh(rwkv_g2, H), ((0, 0), (0, lgp - lg), (0, 0))).astype(BF16),
        rep=rep_mat, ld=ld, la=la, lgp=lgp,
        ln=jnp.stack([_hx_to_xh(rwkv_ln_w, H), _hx_to_xh(rwkv_ln_b, H)], axis=1),
        gla_a2=jnp.pad(gla_a2, ((0, 0), (0, LANES - GR), (0, 0))), gla_ab=gla_ab[:, None], gla_norm=gla_norm[:, None])
    gains = jnp.stack([norm_mix_pre, norm_mix_post, norm_ffn_pre, norm_ffn_post, norm_ple], axis=1)[:, :, None]
    w_pr = jnp.swapaxes(w_proj_rwkv.reshape(L, H, R_HEAD_DIM, D), 1, 2).reshape(L, RW, D).astype(BF16)
    w_pg = w_proj_gla.astype(BF16)
    w_o = w_out.astype(BF16)
    w_fg, w_fu, w_fd = ffn_gate.astype(BF16), ffn_up.astype(BF16), ffn_down.astype(BF16)
    w_plg, w_plp = ple_gate.astype(BF16), ple_proj.astype(BF16)

    shift_states = [jnp.zeros((1, Bp, 1, NR), F32), to_kernel(state_shift)[:, :, None]]
    rwkv_states = [jnp.zeros((1, Bp, R_HEAD_DIM // (LANES // H), R_HEAD_DIM, LANES), F32),
                   _state_to_kernel(state_rwkv, H)]
    gla_states = [jnp.zeros((1, Bp, G_HEADS, GV // G_HEADS, GK // G_HEADS), F32), jnp.swapaxes(state_gla, -1, -2)]

    h = jnp.concatenate([x_prompt.reshape(Mp, D), x_sample.reshape(Ms, D)], axis=0)
    p_all = jnp.concatenate([p_prompt.reshape(L, Mp, -1), p_sample.reshape(L, Ms, -1)], axis=1).astype(BF16)

    new_shift = [[], []]
    new_rwkv = [[], []]
    new_gla = [[], []]
    tn = _pick_tile(D, COL_TILE, LANES)
    for i in range(L):
        z = norm_cast(h, gains, i, G_MIX_PRE)
        proj_r = fused_mm([z], [(0, win_r)], i, [], _ep_identity, NR, F32, "in_proj_r")
        proj_g = fused_mm([z], [(0, win_g)], i, [], _ep_identity, NG, F32, "in_proj_g")
        proj_gate = fused_mm([z], [(0, win_gate)], i, [], _ep_identity, 2 * D, F32, "in_proj_gate")

        o_r_parts, o_g_parts = [], []
        for gi, (row_off, n_seq, T) in enumerate(groups):
            s_layer = i if gi == 1 else 0
            nkk, wr, kka, kmod, pend, c, vmix, g = rwkv_prep(
                proj_r, row_off, n_seq, T, shift_states[gi], s_layer, lw, i)
            y, s_new = rwkv_rec(nkk, wr, kka, kmod, pend, vmix, c, rwkv_states[gi], s_layer, n_seq, T)
            o_r_parts.append(rwkv_post(y.reshape(n_seq * T, RW), vmix, g, c, lw['ln'], i, H))
            o_g, s_gla_new = gla(proj_g, proj_r, row_off, n_seq, T, gla_states[gi], s_layer, lw, i, o_ag)
            o_g_parts.append(o_g)
            new_shift[gi].append(proj_r[row_off + T - 1:row_off + n_seq * T:T])
            new_rwkv[gi].append(s_new)
            new_gla[gi].append(s_gla_new)
        o_r = jnp.concatenate(o_r_parts, axis=0)
        o_g = jnp.concatenate(o_g_parts, axis=0)

        merged = fused_mm([o_r, o_g], [(0, w_pr), (1, w_pg)], i, [(proj_gate, 0), (proj_gate, D // tn)],
                          _ep_merge, D, BF16, "merge", tm_target=ROW_TILE // 2)
        mix = fused_mm([merged], [(0, w_o)], i, [], _ep_identity, D, F32, "out_proj")
        h1, u = resid_norm(h, mix, gains, i, G_MIX_POST, G_FFN_PRE)
        t = fused_mm([u], [(0, w_fg), (0, w_fu)], i, [], _ep_swiglu, FF, BF16, "ffn_up",
                     tm_target=ROW_TILE // 2, tn=COL_TILE)
        f = fused_mm([t], [(0, w_fd)], i, [], _ep_identity, D, F32, "ffn_down",
                     tm_target=ROW_TILE // 2, tn=COL_TILE // 2)
        h2, pn = resid_norm(h1, f, gains, i, G_FFN_POST, G_PLE)
        h = fused_mm([pn, p_all[i]], [(0, w_plg), (1, w_plp)], i, [(h2, 0)], _ep_ple, D, F32, "ple_mm")

    from_kernel = functools.partial(_r_cols_from_kernel, RW=RW, ld=ld, la=la, lg=lg, n_heads=H)
    y_prompt = h[:Mp].reshape(Bp, Tp, D)
    y_sample = h[Mp:].reshape(Bs, Ts, D)
    return (y_prompt, y_sample,
            from_kernel(jnp.stack(new_shift[0])), _state_from_kernel(jnp.stack(new_rwkv[0]), H),
            jnp.swapaxes(jnp.stack(new_gla[0]), -1, -2),
            from_kernel(jnp.stack(new_shift[1])), _state_from_kernel(jnp.stack(new_rwkv[1]), H),
            jnp.swapaxes(jnp.stack(new_gla[1]), -1, -2))
```

```python
import functools

import jax
import jax.numpy as jnp
from jax import lax
from jax.experimental import pallas as pl
from jax.experimental.pallas import tpu as pltpu

F32 = jnp.float32
BF16 = jnp.bfloat16
HIGHEST = lax.Precision.HIGHEST

NORM_EPS = 1e-6
R_GN_EPS = 64e-5
R_HEAD_DIM = 64
G_HEADS = 8
G_GATE_TAU = 16.0

LANES = 128
SUBLANES = 8
VMEM_LIMIT = 56 * 1024 * 1024

ROW_TILE = 1536
COL_TILE = 512
EW_ROWS = 256
CAST_TILE_BYTES = 8 * 1024 * 1024
SEQ_CHUNK = 64
GLA_SUB = 16
SMALL_W = 1024

G_MIX_PRE, G_MIX_POST, G_FFN_PRE, G_FFN_POST, G_PLE = range(5)
V_MU_R, V_MU_K, V_MU_V, V_W0, V_A0, V_KK, V_KA, V_RK = range(8)


def _round_up(x, m):
    return (x + m - 1) // m * m


def _pick_tile(n, target, mult=SUBLANES):
    best = None
    for t in range(mult, min(n, target) + 1, mult):
        if n % t == 0:
            best = t
    assert best is not None, (n, target)
    return best


def _params(sem):
    return pltpu.CompilerParams(dimension_semantics=sem, vmem_limit_bytes=VMEM_LIMIT)


def _rms(x, gain):
    ms = jnp.mean(x * x, axis=-1, keepdims=True)
    return x * lax.rsqrt(ms + NORM_EPS) * gain


def _softplus(x):
    return jnp.maximum(x, 0.0) + jnp.log(1.0 + jnp.exp(-jnp.abs(x)))


def _gain_spec(D, layer, row):
    return pl.BlockSpec((None, None, 1, D), lambda *_: (layer, row, 0, 0))


def _norm_cast_body(x_ref, g_ref, o_ref):
    o_ref[...] = _rms(x_ref[...], g_ref[...]).astype(o_ref.dtype)


def norm_cast(x, gains, layer, row):
    M, D = x.shape
    tm = _pick_tile(M, EW_ROWS)
    return pl.pallas_call(
        _norm_cast_body,
        grid=(M // tm,),
        in_specs=[pl.BlockSpec((tm, D), lambda i: (i, 0)), _gain_spec(D, layer, row)],
        out_specs=pl.BlockSpec((tm, D), lambda i: (i, 0)),
        out_shape=jax.ShapeDtypeStruct((M, D), BF16),
        compiler_params=_params(("parallel",)),
        name="norm_cast",
    )(x, gains)


def _resid_norm_body(h_ref, x_ref, gpost_ref, gpre_ref, h1_ref, u_ref):
    h1 = h_ref[...] + _rms(x_ref[...], gpost_ref[...])
    h1_ref[...] = h1
    u_ref[...] = _rms(h1, gpre_ref[...]).astype(u_ref.dtype)


def resid_norm(h, x, gains, layer, row_post, row_pre):
    M, D = h.shape
    tm = _pick_tile(M, EW_ROWS)
    row = pl.BlockSpec((tm, D), lambda i: (i, 0))
    return pl.pallas_call(
        _resid_norm_body,
        grid=(M // tm,),
        in_specs=[row, row, _gain_spec(D, layer, row_post), _gain_spec(D, layer, row_pre)],
        out_specs=[row, row],
        out_shape=[jax.ShapeDtypeStruct((M, D), F32), jax.ShapeDtypeStruct((M, D), BF16)],
        compiler_params=_params(("parallel",)),
        name="resid_norm",
    )(h, x, gains, gains)


def _cast_body(x_ref, o_ref):
    o_ref[...] = x_ref[...].astype(o_ref.dtype)


def cast_bf16(w):
    L, K, N = w.shape
    rows = L * K
    tr = _pick_tile(rows, max(SUBLANES, CAST_TILE_BYTES // (4 * N) // SUBLANES * SUBLANES))
    out = pl.pallas_call(
        _cast_body,
        grid=(rows // tr,),
        in_specs=[pl.BlockSpec((tr, N), lambda i: (i, 0))],
        out_specs=pl.BlockSpec((tr, N), lambda i: (i, 0)),
        out_shape=jax.ShapeDtypeStruct((rows, N), BF16),
        compiler_params=_params(("parallel",)),
        name="cast_bf16",
    )(w.reshape(rows, N))
    return out.reshape(L, K, N)


def _mm_body(*refs, act_of, n_acts, n_extra, epilogue):
    n_w = len(act_of)
    a_refs = refs[:n_acts]
    w_refs = refs[n_acts:n_acts + n_w]
    e_refs = refs[n_acts + n_w:n_acts + n_w + n_extra]
    o_ref = refs[n_acts + n_w + n_extra]
    acts = [a[...] for a in a_refs]
    dots = [jnp.dot(acts[ai], w[...], preferred_element_type=F32) for ai, w in zip(act_of, w_refs)]
    o_ref[...] = epilogue(dots, [e[...] for e in e_refs]).astype(o_ref.dtype)


def fused_mm(acts, weights, layer, extras, epilogue, n_out, out_dtype, name, tm_target=None, tn=None):
    M = acts[0].shape[0]
    tm = _pick_tile(M, ROW_TILE if tm_target is None else min(tm_target, ROW_TILE))
    tn = _pick_tile(n_out, COL_TILE, LANES) if tn is None else tn
    grid = (M // tm, pl.cdiv(n_out, tn))
    in_specs = ([pl.BlockSpec((tm, a.shape[1]), lambda i, j: (i, 0)) for a in acts]
                + [pl.BlockSpec((None, w.shape[1], tn), lambda i, j: (layer, 0, j)) for _, w in weights]
                + [pl.BlockSpec((tm, tn), functools.partial(lambda i, j, off: (i, j + off), off=off))
                   for _, off in extras])
    body = functools.partial(_mm_body, act_of=tuple(a for a, _ in weights), n_acts=len(acts),
                             n_extra=len(extras), epilogue=epilogue)
    return pl.pallas_call(
        body,
        grid=grid,
        in_specs=in_specs,
        out_specs=pl.BlockSpec((tm, tn), lambda i, j: (i, j)),
        out_shape=jax.ShapeDtypeStruct((M, n_out), out_dtype),
        compiler_params=_params(("parallel", "parallel")),
        name=name,
    )(*acts, *[w for _, w in weights], *[e for e, _ in extras])


def _ep_identity(dots, extras):
    return dots[0]


def _ep_merge(dots, extras):
    return jax.nn.sigmoid(extras[0]) * dots[0] + jax.nn.sigmoid(extras[1]) * dots[1]


def _ep_swiglu(dots, extras):
    return jax.nn.silu(dots[0]) * dots[1]


def _ep_ple(dots, extras):
    return extras[0] + jax.nn.sigmoid(dots[0]) * dots[1]


def _shift_mix(x, prev_first, mu):
    rolled = pltpu.roll(x, 1, axis=0)
    row = lax.broadcasted_iota(jnp.int32, x.shape, 0)
    prev = jnp.where(row == 0, prev_first, rolled)
    return x + (prev - x) * mu


def _head_sum(x, n_heads):
    s = x[:, 0:LANES]
    for b in range(1, x.shape[1] // LANES):
        s = s + x[:, b * LANES:(b + 1) * LANES]
    shift = LANES // 2
    while shift >= n_heads:
        s = s + pltpu.roll(s, shift, axis=1)
        shift //= 2
    return s


def _tile_cols(x, n):
    return jnp.concatenate([x] * n, axis=1)


def _replicate(x, rep, out_ref):
    rows = x.shape[0]
    n_blk = x.shape[1] // LANES
    grp = rep.shape[1]
    xs = jnp.concatenate([x[:, b * LANES:(b + 1) * LANES] for b in range(n_blk)], axis=0)
    hi = xs.astype(BF16)
    rest = xs - hi.astype(F32)
    mid = rest.astype(BF16)
    lo = (rest - mid.astype(F32)).astype(BF16)
    y = jnp.dot(jnp.concatenate([hi, mid, lo], axis=1), rep, preferred_element_type=F32)
    for b in range(n_blk):
        out_ref[:, b * grp:(b + 1) * grp] = y[b * rows:(b + 1) * rows, :]


def _rwkv_prep_body(r_ref, k_ref, v_ref, sm_ref, rp_ref, kp_ref, vp_ref, smp_ref,
                    rs_ref, ks_ref, vs_ref, sms_ref, vec_ref, musm_ref,
                    w2_ref, a2_ref, g2_ref, rep_ref,
                    nkk_ref, wr_ref, kka_ref, kmod_ref, pend_ref, c_ref, vmix_ref, g_ref,
                    *, ld, la, lgp, n_heads):
    j = pl.program_id(1)
    first = j == 0

    def prev_row(p_ref, s_ref):
        return jnp.where(first, s_ref[...], p_ref[SUBLANES - 1:SUBLANES, :])

    def vec(row):
        return vec_ref[row:row + 1, :]

    r = _shift_mix(r_ref[...], prev_row(rp_ref, rs_ref), vec(V_MU_R))
    k = _shift_mix(k_ref[...], prev_row(kp_ref, ks_ref), vec(V_MU_K))
    vmix_ref[...] = _shift_mix(v_ref[...], prev_row(vp_ref, vs_ref), vec(V_MU_V))
    sm = _shift_mix(sm_ref[...], prev_row(smp_ref, sms_ref), musm_ref[...])

    wd = jnp.tanh(sm[:, 0:ld]).astype(BF16)
    ad = sm[:, ld:ld + la].astype(BF16)
    gd = jax.nn.sigmoid(sm[:, ld + la:ld + la + lgp]).astype(BF16)
    g_ref[...] = jnp.dot(gd, g2_ref[...], preferred_element_type=F32)

    tc = r.shape[0]
    n_blk = r.shape[1] // LANES
    w = -_softplus(-(vec(V_W0) + jnp.dot(wd, w2_ref[...], preferred_element_type=F32))) - 0.5
    neg_log_decay = jnp.exp(w)
    a = jax.nn.sigmoid(vec(V_A0) + jnp.dot(ad, a2_ref[...], preferred_element_type=F32))
    kk = k * vec(V_KK)
    inv_norm = 1.0 / jnp.maximum(jnp.sqrt(_head_sum(kk * kk, n_heads)), 1e-12)
    kkn = kk * _tile_cols(inv_norm, n_blk)
    kmod = k * (1.0 + (a - 1.0) * vec(V_KA))
    kka = kkn * a
    rk = r * kmod
    c_ref[:, 0:LANES] = _head_sum(kka * r, n_heads)
    c_ref[:, LANES:2 * LANES] = _head_sum(rk, n_heads)
    c_ref[:, 2 * LANES:3 * LANES] = _head_sum(rk * vec(V_RK), n_heads)

    ti = lax.broadcasted_iota(jnp.int32, (tc, tc), 0)
    si = lax.broadcasted_iota(jnp.int32, (tc, tc), 1)
    tri = (si <= ti).astype(F32)
    log_p = -jnp.dot(tri, neg_log_decay, precision=HIGHEST, preferred_element_type=F32)
    p_incl = jnp.exp(log_p)
    p_prev = jnp.exp(log_p + neg_log_decay)
    inv_p = jnp.exp(-log_p)

    rep = rep_ref[...]
    _replicate(-kkn * p_prev, rep, nkk_ref)
    _replicate(r * p_incl, rep, wr_ref)
    _replicate(kka * inv_p, rep, kka_ref)
    _replicate(kmod * inv_p, rep, kmod_ref)
    _replicate(p_incl[tc - SUBLANES:tc, :], rep, pend_ref)


def rwkv_prep(proj_r, row_off, n_seq, T, shift, shift_layer, lw, layer):
    RW = lw['vec'].shape[-1]
    rep_w = R_HEAD_DIM * LANES
    tc = min(SEQ_CHUNK, T)
    assert T % tc == 0 and row_off % tc == 0 and tc % SUBLANES == 0
    nch = T // tc
    rows = n_seq * T
    off_b = row_off // tc
    sm_col = 3 * RW // SMALL_W
    ld, la, lgp = lw['ld'], lw['la'], lw['lgp']

    def tile(width, colblk):
        return pl.BlockSpec((tc, width), lambda s, j: (off_b + s * nch + j, colblk))

    def prev(width, colblk):
        per = tc // SUBLANES
        return pl.BlockSpec((SUBLANES, width),
                            lambda s, j: (jnp.maximum((off_b + s * nch + j) * per - 1, 0), colblk))

    def state(width, colblk):
        return pl.BlockSpec((None, None, 1, width), lambda s, j: (shift_layer, s, 0, colblk))

    def stacked(arr):
        return pl.BlockSpec((None,) + arr.shape[1:], lambda s, j: (layer,) + (0,) * (arr.ndim - 1))

    def out(width):
        return pl.BlockSpec((tc, width), lambda s, j: (s * nch + j, 0))

    in_specs = [tile(RW, 0), tile(RW, 1), tile(RW, 2), tile(SMALL_W, sm_col),
                prev(RW, 0), prev(RW, 1), prev(RW, 2), prev(SMALL_W, sm_col),
                state(RW, 0), state(RW, 1), state(RW, 2), state(SMALL_W, sm_col),
                stacked(lw['vec']), stacked(lw['mu_sm']),
                stacked(lw['w2']), stacked(lw['a2']), stacked(lw['g2']),
                pl.BlockSpec(lw['rep'].shape, lambda s, j: (0, 0))]
    out_specs = ([out(rep_w)] * 4
                 + [pl.BlockSpec((SUBLANES, rep_w), lambda s, j: (s * nch + j, 0)),
                    out(3 * LANES), out(RW), out(RW)])
    out_shape = ([jax.ShapeDtypeStruct((rows, rep_w), F32)] * 4
                 + [jax.ShapeDtypeStruct((n_seq * nch * SUBLANES, rep_w), F32),
                    jax.ShapeDtypeStruct((rows, 3 * LANES), F32),
                    jax.ShapeDtypeStruct((rows, RW), F32), jax.ShapeDtypeStruct((rows, RW), F32)])
    body = functools.partial(_rwkv_prep_body, ld=ld, la=la, lgp=lgp, n_heads=RW // R_HEAD_DIM)
    return pl.pallas_call(
        body,
        grid=(n_seq, nch),
        in_specs=in_specs,
        out_specs=out_specs,
        out_shape=out_shape,
        compiler_params=_params(("parallel", "arbitrary")),
        name="rwkv_prep",
    )(proj_r, proj_r, proj_r, proj_r, proj_r, proj_r, proj_r, proj_r,
      shift, shift, shift, shift, lw['vec'], lw['mu_sm'], lw['w2'], lw['a2'], lw['g2'], lw['rep'])


def _rwkv_rec_body(nkk_ref, wr_ref, kka_ref, kmod_ref, pend_ref, v_ref, c_ref, s0_ref,
                   y_ref, sout_ref, s_ref, *, tc, n_grp, nch):
    j = pl.program_id(1)

    @pl.when(j == 0)
    def _():
        s_ref[...] = s0_ref[...]

    def row(ref, t, k):
        return jnp.broadcast_to(ref[t, pl.ds(k, 1), :], (SUBLANES, LANES))

    def reduce(t, tiles):
        acc = [[None, None] for _ in range(2 * n_grp)]
        for k in range(R_HEAD_DIM):
            nkk = row(nkk_ref, t, k)
            wr = row(wr_ref, t, k)
            for g, sg in enumerate(tiles(k)):
                for slot, term in ((g, sg * nkk), (n_grp + g, sg * wr)):
                    prev = acc[slot][k % 2]
                    acc[slot][k % 2] = term if prev is None else prev + term
        return tuple(a[0] + a[1] for a in acc)

    def step(t, carry):
        sa, yp = carry[:n_grp], carry[n_grp:]
        crow = c_ref[pl.ds(t, 1), :]
        c1 = crow[:, 0:LANES]
        c2 = crow[:, LANES:2 * LANES]
        vg = [v_ref[t, g * SUBLANES:(g + 1) * SUBLANES, :] for g in range(n_grp)]
        for g in range(n_grp):
            y_ref[t, g * SUBLANES:(g + 1) * SUBLANES, :] = yp[g] + sa[g] * c1 + vg[g] * c2

        def updated(k):
            kka = row(kka_ref, t, k)
            kmod = row(kmod_ref, t, k)
            new = [s_ref[g, k] + sa[g] * kka + vg[g] * kmod for g in range(n_grp)]
            for g in range(n_grp):
                s_ref[g, k] = new[g]
            return new

        return reduce(jnp.minimum(t + 1, tc - 1), updated)

    carry0 = reduce(0, lambda k: [s_ref[g, k] for g in range(n_grp)])
    lax.fori_loop(0, tc, step, carry0)

    for k in range(R_HEAD_DIM):
        p_end = row(pend_ref, SUBLANES - 1, k)
        for g in range(n_grp):
            s_ref[g, k] = s_ref[g, k] * p_end

    @pl.when(j == nch - 1)
    def _():
        sout_ref[...] = s_ref[...]


def rwkv_rec(nkk, wr, kka, kmod, pend, vmix, c, s0, s0_layer, n_seq, T):
    rows = n_seq * T
    n_grp = s0.shape[2]
    v_rows = n_grp * SUBLANES
    tc = min(SEQ_CHUNK, T)
    nch = T // tc
    k3 = lambda x: x.reshape(x.shape[0], R_HEAD_DIM, LANES)
    kspec = pl.BlockSpec((tc, R_HEAD_DIM, LANES), lambda s, j: (s * nch + j, 0, 0))
    pspec = pl.BlockSpec((SUBLANES, R_HEAD_DIM, LANES), lambda s, j: (s * nch + j, 0, 0))
    vspec = pl.BlockSpec((tc, v_rows, LANES), lambda s, j: (s * nch + j, 0, 0))
    tile = (n_grp, R_HEAD_DIM, SUBLANES, LANES)
    s_in = pl.BlockSpec((None, None) + tile, lambda s, j: (s0_layer, s, 0, 0, 0, 0))
    s_out = pl.BlockSpec((None,) + tile, lambda s, j: (s, 0, 0, 0, 0))
    body = functools.partial(_rwkv_rec_body, tc=tc, n_grp=n_grp, nch=nch)
    return pl.pallas_call(
        body,
        grid=(n_seq, nch),
        in_specs=[kspec] * 4 + [pspec, vspec, pl.BlockSpec((tc, 3 * LANES), lambda s, j: (s * nch + j, 0)), s_in],
        out_specs=[vspec, s_out],
        out_shape=[jax.ShapeDtypeStruct((rows, v_rows, LANES), F32),
                   jax.ShapeDtypeStruct(s0.shape[1:], F32)],
        scratch_shapes=[pltpu.VMEM(tile, F32)],
        compiler_params=_params(("parallel", "arbitrary")),
        name="rwkv_rec",
    )(k3(nkk), k3(wr), k3(kka), k3(kmod), k3(pend), vmix.reshape(rows, v_rows, LANES), c, s0)


def _rwkv_post_body(y_ref, v_ref, g_ref, c_ref, ln_ref, o_ref, *, n_grp, n_heads):
    def head_sum(x):
        return _tile_cols(_head_sum(x, n_heads), n_grp)

    y = y_ref[...]
    inv_n = 1.0 / R_HEAD_DIM
    mu = head_sum(y) * inv_n
    d = y - mu
    var = head_sum(d * d) * inv_n
    yln = d * lax.rsqrt(var + R_GN_EPS) * ln_ref[0:1, :] + ln_ref[1:2, :]
    c3 = _tile_cols(c_ref[:, 2 * LANES:3 * LANES], n_grp)
    o_ref[...] = ((yln + c3 * v_ref[...]) * g_ref[...]).astype(o_ref.dtype)


def rwkv_post(y, vmix, g, c, ln, layer, n_heads):
    M, RW = y.shape
    tm = _pick_tile(M, EW_ROWS)
    row = pl.BlockSpec((tm, RW), lambda i: (i, 0))
    body = functools.partial(_rwkv_post_body, n_grp=RW // LANES, n_heads=n_heads)
    return pl.pallas_call(
        body,
        grid=(M // tm,),
        in_specs=[row, row, row, pl.BlockSpec((tm, 3 * LANES), lambda i: (i, 0)),
                  pl.BlockSpec((None, 2, RW), lambda i: (layer, 0, 0))],
        out_specs=row,
        out_shape=jax.ShapeDtypeStruct((M, RW), BF16),
        compiler_params=_params(("parallel",)),
        name="rwkv_post",
    )(y, vmix, g, c, ln)


def _gla_body(q_ref, k_ref, v_ref, rg_ref, ag_ref, a2_ref, ab_ref, gn_ref, s0_ref,
              o_ref, sout_ref, st_ref, *, C, nch, n_heads, dk, dv):
    c = pl.program_id(1)

    @pl.when(c == 0)
    def _():
        st_ref[...] = s0_ref[...]

    x = jnp.dot(ag_ref[...], a2_ref[...], precision=HIGHEST, preferred_element_type=F32) + ab_ref[...]
    log_a = -_softplus(-x) * (1.0 / G_GATE_TAU)
    ti = lax.broadcasted_iota(jnp.int32, (C, C), 0)
    si = lax.broadcasted_iota(jnp.int32, (C, C), 1)
    tri = (si <= ti).astype(F32)
    b_all = jnp.dot(tri, log_a, precision=HIGHEST, preferred_element_type=F32)

    nt = (((1,), (1,)), ((), ()))
    tn = (((0,), (0,)), ((), ()))
    scale = float(dk) ** -0.5
    rowi = lax.broadcasted_iota(jnp.int32, (GLA_SUB, dk), 0)

    for h in range(n_heads):
        ksl = slice(h * dk, (h + 1) * dk)
        vsl = slice(h * dv, (h + 1) * dv)
        b = b_all[:, ksl]
        b_last = b[C - 1:C, :]
        q = q_ref[:, ksl] * scale
        k = k_ref[:, ksl]
        v = v_ref[:, vsl]
        vh = v.astype(BF16)
        st = st_ref[h]

        inter = lax.dot_general((q * jnp.exp(b)).astype(BF16), st.astype(BF16), nt,
                                preferred_element_type=F32)
        blocks = []
        for blk in range(C // GLA_SUB):
            t0 = blk * GLA_SUB
            rows = slice(t0, t0 + GLA_SUB)
            qb, kb, bb, vb = q[rows], k[rows], b[rows], v[rows]
            acc = inter[rows]
            if blk > 0:
                b0 = bb[0:1, :]
                qd = (qb * jnp.exp(bb - b0)).astype(BF16)
                kd = (k[0:t0] * jnp.exp(b0 - b[0:t0])).astype(BF16)
                sc = lax.dot_general(qd, kd, nt, preferred_element_type=F32)
                acc = acc + jnp.dot(sc.astype(BF16), vh[0:t0], preferred_element_type=F32)
            for s in range(GLA_SUB):
                e = jnp.where(rowi >= s, jnp.exp(bb - bb[s:s + 1, :]), 0.0)
                sc_s = jnp.sum(qb * e * kb[s:s + 1, :], axis=-1, keepdims=True)
                acc = acc + sc_s * vb[s:s + 1, :]
            blocks.append(acc)
        o = jnp.concatenate(blocks, axis=0) if len(blocks) > 1 else blocks[0]

        kdec = (k * jnp.exp(b_last - b)).astype(BF16)
        st_ref[h] = st * jnp.exp(b_last) + lax.dot_general(vh, kdec, tn, preferred_element_type=F32)

        ms = jnp.mean(o * o, axis=-1, keepdims=True)
        o_ref[:, vsl] = (o * lax.rsqrt(ms + NORM_EPS) * gn_ref[:, vsl]
                         * jax.nn.silu(rg_ref[:, vsl])).astype(o_ref.dtype)

    @pl.when(c == nch - 1)
    def _():
        sout_ref[...] = st_ref[...]


def gla(proj_g, proj_r, row_off, n_seq, T, s0t, s0_layer, lw, layer, o_ag):
    GK, GV = lw['gla_ab'].shape[-1], lw['gla_norm'].shape[-1]
    dk, dv = GK // G_HEADS, GV // G_HEADS
    C = min(SEQ_CHUNK, T)
    assert T % C == 0 and C % GLA_SUB == 0 and row_off % C == 0
    assert GV % GK == 0 and (2 * GK) % GV == 0 and o_ag % LANES == 0
    nch = T // C
    rows = n_seq * T
    off_b = row_off // C

    def tile(width, col0):
        assert col0 % width == 0
        return pl.BlockSpec((C, width), lambda s, c: (off_b + s * nch + c, col0 // width))

    def stacked(arr):
        return pl.BlockSpec((None,) + arr.shape[1:], lambda s, c: (layer,) + (0,) * (arr.ndim - 1))

    in_specs = [tile(GK, 0), tile(GK, GK), tile(GV, 2 * GK), tile(GV, 2 * GK + GV),
                tile(LANES, o_ag),
                stacked(lw['gla_a2']), stacked(lw['gla_ab']), stacked(lw['gla_norm']),
                pl.BlockSpec((None, None, G_HEADS, dv, dk), lambda s, c: (s0_layer, s, 0, 0, 0))]
    body = functools.partial(_gla_body, C=C, nch=nch, n_heads=G_HEADS, dk=dk, dv=dv)
    return pl.pallas_call(
        body,
        grid=(n_seq, nch),
        in_specs=in_specs,
        out_specs=[pl.BlockSpec((C, GV), lambda s, c: (s * nch + c, 0)),
                   pl.BlockSpec((None, G_HEADS, dv, dk), lambda s, c: (s, 0, 0, 0))],
        out_shape=[jax.ShapeDtypeStruct((rows, GV), BF16),
                   jax.ShapeDtypeStruct(s0t.shape[1:], F32)],
        scratch_shapes=[pltpu.VMEM((G_HEADS, dv, dk), F32)],
        compiler_params=_params(("parallel", "arbitrary")),
        name="gla",
    )(proj_g, proj_g, proj_g, proj_g, proj_r, lw['gla_a2'], lw['gla_ab'], lw['gla_norm'], s0t)


def _hx_to_xh(w, n_heads):
    sh = w.shape[:-1]
    x = w.shape[-1] // n_heads
    return jnp.swapaxes(w.reshape(*sh, n_heads, x), -1, -2).reshape(*sh, n_heads * x)


def _xh_to_hx(w, n_heads):
    sh = w.shape[:-1]
    x = w.shape[-1] // n_heads
    return jnp.swapaxes(w.reshape(*sh, x, n_heads), -1, -2).reshape(*sh, n_heads * x)


def _pad_last(x, width):
    return jnp.pad(x, [(0, 0)] * (x.ndim - 1) + [(0, width - x.shape[-1])])


def _split_r_cols(x, RW, ld, la, lg):
    offs = [0, RW, 2 * RW, 3 * RW, 3 * RW + ld, 3 * RW + ld + la, 3 * RW + ld + la + lg]
    return [x[..., offs[i]:offs[i + 1]] for i in range(6)]


def _small_block(wd, ad, gd, lgp, tail=None):
    parts = [wd, ad, _pad_last(gd, lgp)]
    if tail is not None:
        parts.append(tail)
    return _pad_last(jnp.concatenate(parts, axis=-1), SMALL_W)


def _r_cols_to_kernel(x, RW, ld, la, lg, lgp, n_heads, tail=None):
    x_r, x_k, x_v, x_wd, x_ad, x_gd = _split_r_cols(x, RW, ld, la, lg)
    return jnp.concatenate([_hx_to_xh(x_r, n_heads), _hx_to_xh(x_k, n_heads), _hx_to_xh(x_v, n_heads),
                            _small_block(x_wd, x_ad, x_gd, lgp, tail)], axis=-1)


def _r_cols_from_kernel(x, RW, ld, la, lg, n_heads):
    sm = x[..., 3 * RW:]
    return jnp.concatenate(
        [_xh_to_hx(x[..., 0:RW], n_heads), _xh_to_hx(x[..., RW:2 * RW], n_heads),
         _xh_to_hx(x[..., 2 * RW:3 * RW], n_heads),
         sm[..., 0:ld], sm[..., ld:ld + la], sm[..., ld + la:ld + la + lg]], axis=-1)


def _state_to_kernel(s, n_heads):
    lead = s.shape[:-3]
    n = len(lead)
    vv = LANES // n_heads
    g = R_HEAD_DIM // (SUBLANES * vv)
    x = s.reshape(*lead, n_heads, g, SUBLANES, vv, R_HEAD_DIM)
    x = x.transpose(*range(n), n + 1, n + 4, n + 2, n + 3, n)
    return x.reshape(*lead, g, R_HEAD_DIM, SUBLANES, LANES)


def _state_from_kernel(s, n_heads):
    lead = s.shape[:-4]
    n = len(lead)
    g = s.shape[-4]
    vv = LANES // n_heads
    x = s.reshape(*lead, g, R_HEAD_DIM, SUBLANES, vv, n_heads)
    x = x.transpose(*range(n), n + 4, n, n + 2, n + 3, n + 1)
    return x.reshape(*lead, n_heads, R_HEAD_DIM, R_HEAD_DIM)


def kernel(x_prompt, x_sample, state_shift, state_rwkv, state_gla, p_prompt, p_sample, norm_mix_pre, norm_mix_post, norm_ffn_pre, norm_ffn_post, norm_ple, w_in, rwkv_mu, rwkv_w0, rwkv_w2, rwkv_a0, rwkv_a2, rwkv_g2, rwkv_k_k, rwkv_k_a, rwkv_r_k, rwkv_ln_w, rwkv_ln_b, gla_a2, gla_ab, gla_norm, w_proj_rwkv, w_proj_gla, w_out, ffn_gate, ffn_up, ffn_down, ple_proj, ple_gate):
    Bp, Tp, D = x_prompt.shape
    Bs, Ts, _ = x_sample.shape
    L = w_in.shape[0]
    RW = rwkv_w0.shape[-1]
    H = RW // R_HEAD_DIM
    assert LANES % H == 0 and R_HEAD_DIM % (SUBLANES * (LANES // H)) == 0
    ld, la, lg = rwkv_w2.shape[1], rwkv_a2.shape[1], rwkv_g2.shape[1]
    lgp = _round_up(lg, LANES)
    GK, GV, GR = gla_a2.shape[-1], gla_norm.shape[-1], gla_a2.shape[1]
    FF = ffn_gate.shape[-1]
    Mp, Ms = Bp * Tp, Bs * Ts
    groups = [(0, Bp, Tp), (Mp, Bs, Ts)]

    NR = 3 * RW + SMALL_W
    NG = 2 * GK + 2 * GV
    o_ag = 3 * RW + ld + la + lgp
    assert o_ag + LANES <= NR and (3 * RW) % SMALL_W == 0 and GR <= LANES
    r_cols = 3 * RW + ld + la + lg
    g_cols = NG + GR
    to_kernel = functools.partial(_r_cols_to_kernel, RW=RW, ld=ld, la=la, lg=lg, lgp=lgp, n_heads=H)
    win_r = to_kernel(w_in[..., :r_cols], tail=w_in[..., r_cols + NG:r_cols + g_cols]).astype(BF16)
    win_g = w_in[..., r_cols:r_cols + NG].astype(BF16)
    win_gate = w_in[..., r_cols + g_cols:].astype(BF16)
    assert win_r.shape[-1] == NR and win_gate.shape[-1] == 2 * D

    mu_k = to_kernel(rwkv_mu)
    kq = LANES // H
    src = jnp.arange(LANES)
    dst = jnp.arange(kq * LANES)
    rep_mat = jnp.tile((src[:, None] == ((dst // LANES) * H + dst % H)[None, :]).astype(BF16), (3, 1))
    lw = dict(
        vec=jnp.stack([mu_k[:, 0:RW], mu_k[:, RW:2 * RW], mu_k[:, 2 * RW:3 * RW],
                       _hx_to_xh(rwkv_w0, H), _hx_to_xh(rwkv_a0, H), _hx_to_xh(rwkv_k_k, H),
                       _hx_to_xh(rwkv_k_a, H), _hx_to_xh(rwkv_r_k.reshape(L, RW), H)], axis=1),
        mu_sm=mu_k[:, None, 3 * RW:],
        w2=_hx_to_xh(rwkv_w2, H).astype(BF16), a2=_hx_to_xh(rwkv_a2, H).astype(BF16),
        g2=jnp.pad(_hx_to_xh(rwkv_g2, H), ((0, 0), (0, lgp - lg), (0, 0))).astype(BF16),
        rep=rep_mat, ld=ld, la=la, lgp=lgp,
        ln=jnp.stack([_hx_to_xh(rwkv_ln_w, H), _hx_to_xh(rwkv_ln_b, H)], axis=1),
        gla_a2=jnp.pad(gla_a2, ((0, 0), (0, LANES - GR), (0, 0))), gla_ab=gla_ab[:, None], gla_norm=gla_norm[:, None])
    gains = jnp.stack([norm_mix_pre, norm_mix_post, norm_ffn_pre, norm_ffn_post, norm_ple], axis=1)[:, :, None]
    w_pr = jnp.swapaxes(w_proj_rwkv.reshape(L, H, R_HEAD_DIM, D), 1, 2).reshape(L, RW, D).astype(BF16)
    w_pg = cast_bf16(w_proj_gla)
    w_o = cast_bf16(w_out)
    w_fg, w_fu, w_fd = cast_bf16(ffn_gate), cast_bf16(ffn_up), cast_bf16(ffn_down)
    w_plg, w_plp = cast_bf16(ple_gate), ple_proj.astype(BF16)

    shift_states = [jnp.zeros((1, Bp, 1, NR), F32), to_kernel(state_shift)[:, :, None]]
    n_vgrp = R_HEAD_DIM // (SUBLANES * (LANES // H))
    rwkv_states = [jnp.zeros((1, Bp, n_vgrp, R_HEAD_DIM, SUBLANES, LANES), F32),
                   _state_to_kernel(state_rwkv, H)]
    gla_states = [jnp.zeros((1, Bp, G_HEADS, GV // G_HEADS, GK // G_HEADS), F32), jnp.swapaxes(state_gla, -1, -2)]

    h = jnp.concatenate([x_prompt.reshape(Mp, D), x_sample.reshape(Ms, D)], axis=0)
    p_all = jnp.concatenate([p_prompt.reshape(L, Mp, -1), p_sample.reshape(L, Ms, -1)], axis=1).astype(BF16)

    new_shift = [[], []]
    new_rwkv = [[], []]
    new_gla = [[], []]
    tn = _pick_tile(D, COL_TILE, LANES)
    for i in range(L):
        z = norm_cast(h, gains, i, G_MIX_PRE)
        proj_r = fused_mm([z], [(0, win_r)], i, [], _ep_identity, NR, F32, "in_proj_r")
        proj_g = fused_mm([z], [(0, win_g)], i, [], _ep_identity, NG, F32, "in_proj_g")
        proj_gate = fused_mm([z], [(0, win_gate)], i, [], _ep_identity, 2 * D, F32, "in_proj_gate")

        o_r_parts, o_g_parts = [], []
        for gi, (row_off, n_seq, T) in enumerate(groups):
            s_layer = i if gi == 1 else 0
            nkk, wr, kka, kmod, pend, c, vmix, g = rwkv_prep(
                proj_r, row_off, n_seq, T, shift_states[gi], s_layer, lw, i)
            y, s_new = rwkv_rec(nkk, wr, kka, kmod, pend, vmix, c, rwkv_states[gi], s_layer, n_seq, T)
            o_r_parts.append(rwkv_post(y.reshape(n_seq * T, RW), vmix, g, c, lw['ln'], i, H))
            o_g, s_gla_new = gla(proj_g, proj_r, row_off, n_seq, T, gla_states[gi], s_layer, lw, i, o_ag)
            o_g_parts.append(o_g)
            new_shift[gi].append(proj_r[row_off + T - 1:row_off + n_seq * T:T])
            new_rwkv[gi].append(s_new)
            new_gla[gi].append(s_gla_new)
        o_r = jnp.concatenate(o_r_parts, axis=0)
        o_g = jnp.concatenate(o_g_parts, axis=0)

        merged = fused_mm([o_r, o_g], [(0, w_pr), (1, w_pg)], i, [(proj_gate, 0), (proj_gate, D // tn)],
                          _ep_merge, D, BF16, "merge", tm_target=ROW_TILE // 2)
        mix = fused_mm([merged], [(0, w_o)], i, [], _ep_identity, D, F32, "out_proj")
        h1, u = resid_norm(h, mix, gains, i, G_MIX_POST, G_FFN_PRE)
        t = fused_mm([u], [(0, w_fg), (0, w_fu)], i, [], _ep_swiglu, FF, BF16, "ffn_up",
                     tm_target=ROW_TILE // 2, tn=COL_TILE)
        f = fused_mm([t], [(0, w_fd)], i, [], _ep_identity, D, F32, "ffn_down",
                     tm_target=ROW_TILE // 2, tn=COL_TILE // 2)
        h2, pn = resid_norm(h1, f, gains, i, G_FFN_POST, G_PLE)
        h = fused_mm([pn, p_all[i]], [(0, w_plg), (1, w_plp)], i, [(h2, 0)], _ep_ple, D, F32, "ple_mm")

    from_kernel = functools.partial(_r_cols_from_kernel, RW=RW, ld=ld, la=la, lg=lg, n_heads=H)
    y_prompt = h[:Mp].reshape(Bp, Tp, D)
    y_sample = h[Mp:].reshape(Bs, Ts, D)
    return (y_prompt, y_sample,
            from_kernel(jnp.stack(new_shift[0])), _state_from_kernel(jnp.stack(new_rwkv[0]), H),
            jnp.swapaxes(jnp.stack(new_gla[0]), -1, -2),
            from_kernel(jnp.stack(new_shift[1])), _state_from_kernel(jnp.stack(new_rwkv[1]), H),
            jnp.swapaxes(jnp.stack(new_gla[1]), -1, -2))
```

```python
import functools

import jax
import jax.numpy as jnp
from jax import lax
from jax.experimental import pallas as pl
from jax.experimental.pallas import tpu as pltpu

F32 = jnp.float32
BF16 = jnp.bfloat16
HIGHEST = lax.Precision.HIGHEST

NORM_EPS = 1e-6
R_GN_EPS = 64e-5
R_HEAD_DIM = 64
G_HEADS = 8
G_GATE_TAU = 16.0

LANES = 128
SUBLANES = 8
VMEM_LIMIT = 56 * 1024 * 1024

ROW_TILE = 1536
COL_TILE = 512
EW_ROWS = 256
CAST_TILE_BYTES = 8 * 1024 * 1024
SEQ_CHUNK = 64
GLA_SUB = 16
SMALL_W = 1024

G_MIX_PRE, G_MIX_POST, G_FFN_PRE, G_FFN_POST, G_PLE = range(5)
V_MU_R, V_MU_K, V_MU_V, V_W0, V_A0, V_KK, V_KA, V_RK = range(8)


def _round_up(x, m):
    return (x + m - 1) // m * m


def _pick_tile(n, target, mult=SUBLANES):
    best = None
    for t in range(mult, min(n, target) + 1, mult):
        if n % t == 0:
            best = t
    assert best is not None, (n, target)
    return best


def _params(sem):
    return pltpu.CompilerParams(dimension_semantics=sem, vmem_limit_bytes=VMEM_LIMIT)


def _rms(x, gain):
    ms = jnp.mean(x * x, axis=-1, keepdims=True)
    return x * lax.rsqrt(ms + NORM_EPS) * gain


def _softplus(x):
    return jnp.maximum(x, 0.0) + jnp.log(1.0 + jnp.exp(-jnp.abs(x)))


def _gain_spec(D, layer, row):
    return pl.BlockSpec((None, None, 1, D), lambda *_: (layer, row, 0, 0))


def _norm_cast_body(x_ref, g_ref, o_ref):
    o_ref[...] = _rms(x_ref[...], g_ref[...]).astype(o_ref.dtype)


def norm_cast(x, gains, layer, row):
    M, D = x.shape
    tm = _pick_tile(M, EW_ROWS)
    return pl.pallas_call(
        _norm_cast_body,
        grid=(M // tm,),
        in_specs=[pl.BlockSpec((tm, D), lambda i: (i, 0)), _gain_spec(D, layer, row)],
        out_specs=pl.BlockSpec((tm, D), lambda i: (i, 0)),
        out_shape=jax.ShapeDtypeStruct((M, D), BF16),
        compiler_params=_params(("parallel",)),
        name="norm_cast",
    )(x, gains)


def _resid_norm_body(h_ref, x_ref, gpost_ref, gpre_ref, h1_ref, u_ref):
    h1 = h_ref[...] + _rms(x_ref[...], gpost_ref[...])
    h1_ref[...] = h1
    u_ref[...] = _rms(h1, gpre_ref[...]).astype(u_ref.dtype)


def resid_norm(h, x, gains, layer, row_post, row_pre):
    M, D = h.shape
    tm = _pick_tile(M, EW_ROWS)
    row = pl.BlockSpec((tm, D), lambda i: (i, 0))
    return pl.pallas_call(
        _resid_norm_body,
        grid=(M // tm,),
        in_specs=[row, row, _gain_spec(D, layer, row_post), _gain_spec(D, layer, row_pre)],
        out_specs=[row, row],
        out_shape=[jax.ShapeDtypeStruct((M, D), F32), jax.ShapeDtypeStruct((M, D), BF16)],
        compiler_params=_params(("parallel",)),
        name="resid_norm",
    )(h, x, gains, gains)


def _cast_body(x_ref, o_ref):
    o_ref[...] = x_ref[...].astype(o_ref.dtype)


def cast_bf16(w):
    L, K, N = w.shape
    rows = L * K
    tr = _pick_tile(rows, max(SUBLANES, CAST_TILE_BYTES // (4 * N) // SUBLANES * SUBLANES))
    out = pl.pallas_call(
        _cast_body,
        grid=(rows // tr,),
        in_specs=[pl.BlockSpec((tr, N), lambda i: (i, 0))],
        out_specs=pl.BlockSpec((tr, N), lambda i: (i, 0)),
        out_shape=jax.ShapeDtypeStruct((rows, N), BF16),
        compiler_params=_params(("parallel",)),
        name="cast_bf16",
    )(w.reshape(rows, N))
    return out.reshape(L, K, N)


def _mm_body(*refs, act_of, n_acts, n_extra, epilogue):
    n_w = len(act_of)
    a_refs = refs[:n_acts]
    w_refs = refs[n_acts:n_acts + n_w]
    e_refs = refs[n_acts + n_w:n_acts + n_w + n_extra]
    o_ref = refs[n_acts + n_w + n_extra]
    acts = [a[...] for a in a_refs]
    dots = [jnp.dot(acts[ai], w[...].astype(BF16), preferred_element_type=F32) for ai, w in zip(act_of, w_refs)]
    o_ref[...] = epilogue(dots, [e[...] for e in e_refs]).astype(o_ref.dtype)


def fused_mm(acts, weights, layer, extras, epilogue, n_out, out_dtype, name, tm_target=None, tn=None):
    M = acts[0].shape[0]
    tm = _pick_tile(M, ROW_TILE if tm_target is None else min(tm_target, ROW_TILE))
    tn = _pick_tile(n_out, COL_TILE, LANES) if tn is None else tn
    grid = (M // tm, pl.cdiv(n_out, tn))
    in_specs = ([pl.BlockSpec((tm, a.shape[1]), lambda i, j: (i, 0)) for a in acts]
                + [pl.BlockSpec((None, w.shape[1], tn), lambda i, j: (layer, 0, j)) for _, w in weights]
                + [pl.BlockSpec((tm, tn), functools.partial(lambda i, j, off: (i, j + off), off=off))
                   for _, off in extras])
    body = functools.partial(_mm_body, act_of=tuple(a for a, _ in weights), n_acts=len(acts),
                             n_extra=len(extras), epilogue=epilogue)
    return pl.pallas_call(
        body,
        grid=grid,
        in_specs=in_specs,
        out_specs=pl.BlockSpec((tm, tn), lambda i, j: (i, j)),
        out_shape=jax.ShapeDtypeStruct((M, n_out), out_dtype),
        compiler_params=_params(("parallel", "parallel")),
        name=name,
    )(*acts, *[w for _, w in weights], *[e for e, _ in extras])


def _ep_identity(dots, extras):
    return dots[0]


def _ep_sigmoid(dots, extras):
    return jax.nn.sigmoid(dots[0])


def _ep_merge(dots, extras):
    return extras[0] * dots[0] + extras[1] * dots[1]


def _ep_swiglu(dots, extras):
    return jax.nn.silu(dots[0]) * dots[1]


def _ep_ple(dots, extras):
    return extras[0] + jax.nn.sigmoid(dots[0]) * dots[1]


def _shift_mix(x, prev_first, mu):
    rolled = pltpu.roll(x, 1, axis=0)
    row = lax.broadcasted_iota(jnp.int32, x.shape, 0)
    prev = jnp.where(row == 0, prev_first, rolled)
    return x + (prev - x) * mu


def _head_sum(x, n_heads):
    s = x[:, 0:LANES]
    for b in range(1, x.shape[1] // LANES):
        s = s + x[:, b * LANES:(b + 1) * LANES]
    shift = LANES // 2
    while shift >= n_heads:
        s = s + pltpu.roll(s, shift, axis=1)
        shift //= 2
    return s


def _tile_cols(x, n):
    return jnp.concatenate([x] * n, axis=1)


def _replicate(x, rep, out_ref):
    rows = x.shape[0]
    n_blk = x.shape[1] // LANES
    grp = rep.shape[1]
    xs = jnp.concatenate([x[:, b * LANES:(b + 1) * LANES] for b in range(n_blk)], axis=0)
    hi = xs.astype(BF16)
    rest = xs - hi.astype(F32)
    mid = rest.astype(BF16)
    lo = (rest - mid.astype(F32)).astype(BF16)
    y = jnp.dot(jnp.concatenate([hi, mid, lo], axis=1), rep, preferred_element_type=F32)
    for b in range(n_blk):
        out_ref[:, b * grp:(b + 1) * grp] = y[b * rows:(b + 1) * rows, :]


def _rwkv_prep_body(r_ref, k_ref, v_ref, sm_ref, rp_ref, kp_ref, vp_ref, smp_ref,
                    rs_ref, ks_ref, vs_ref, sms_ref, vec_ref, musm_ref,
                    w2_ref, a2_ref, g2_ref, rep_ref,
                    nkk_ref, wr_ref, kka_ref, kmod_ref, pend_ref, c_ref, vmix_ref, g_ref,
                    *, ld, la, lgp, n_heads):
    j = pl.program_id(1)
    first = j == 0

    def prev_row(p_ref, s_ref):
        return jnp.where(first, s_ref[...], p_ref[SUBLANES - 1:SUBLANES, :])

    def vec(row):
        return vec_ref[row:row + 1, :]

    r = _shift_mix(r_ref[...], prev_row(rp_ref, rs_ref), vec(V_MU_R))
    k = _shift_mix(k_ref[...], prev_row(kp_ref, ks_ref), vec(V_MU_K))
    vmix_ref[...] = _shift_mix(v_ref[...], prev_row(vp_ref, vs_ref), vec(V_MU_V))
    sm = _shift_mix(sm_ref[...], prev_row(smp_ref, sms_ref), musm_ref[...])

    wd = jnp.tanh(sm[:, 0:ld]).astype(BF16)
    ad = sm[:, ld:ld + la].astype(BF16)
    gd = jax.nn.sigmoid(sm[:, ld + la:ld + la + lgp]).astype(BF16)
    g_ref[...] = jnp.dot(gd, g2_ref[...], preferred_element_type=F32)

    tc = r.shape[0]
    n_blk = r.shape[1] // LANES
    w = -_softplus(-(vec(V_W0) + jnp.dot(wd, w2_ref[...], preferred_element_type=F32))) - 0.5
    neg_log_decay = jnp.exp(w)
    a = jax.nn.sigmoid(vec(V_A0) + jnp.dot(ad, a2_ref[...], preferred_element_type=F32))
    kk = k * vec(V_KK)
    inv_norm = 1.0 / jnp.maximum(jnp.sqrt(_head_sum(kk * kk, n_heads)), 1e-12)
    kkn = kk * _tile_cols(inv_norm, n_blk)
    kmod = k * (1.0 + (a - 1.0) * vec(V_KA))
    kka = kkn * a
    rk = r * kmod
    c_ref[:, 0:LANES] = _head_sum(kka * r, n_heads)
    c_ref[:, LANES:2 * LANES] = _head_sum(rk, n_heads)
    c_ref[:, 2 * LANES:3 * LANES] = _head_sum(rk * vec(V_RK), n_heads)

    ti = lax.broadcasted_iota(jnp.int32, (tc, tc), 0)
    si = lax.broadcasted_iota(jnp.int32, (tc, tc), 1)
    tri = (si <= ti).astype(F32)
    log_p = -jnp.dot(tri, neg_log_decay, precision=HIGHEST, preferred_element_type=F32)
    p_incl = jnp.exp(log_p)
    p_prev = jnp.exp(log_p + neg_log_decay)
    inv_p = jnp.exp(-log_p)

    rep = rep_ref[...]
    _replicate(-kkn * p_prev, rep, nkk_ref)
    _replicate(r * p_incl, rep, wr_ref)
    _replicate(kka * inv_p, rep, kka_ref)
    _replicate(kmod * inv_p, rep, kmod_ref)
    _replicate(p_incl[tc - SUBLANES:tc, :], rep, pend_ref)


def rwkv_prep(proj_r, row_off, n_seq, T, shift, shift_layer, lw, layer):
    RW = lw['vec'].shape[-1]
    rep_w = R_HEAD_DIM * LANES
    tc = min(SEQ_CHUNK, T)
    assert T % tc == 0 and row_off % tc == 0 and tc % SUBLANES == 0
    nch = T // tc
    rows = n_seq * T
    off_b = row_off // tc
    sm_col = 3 * RW // SMALL_W
    ld, la, lgp = lw['ld'], lw['la'], lw['lgp']

    def tile(width, colblk):
        return pl.BlockSpec((tc, width), lambda s, j: (off_b + s * nch + j, colblk))

    def prev(width, colblk):
        per = tc // SUBLANES
        return pl.BlockSpec((SUBLANES, width),
                            lambda s, j: (jnp.maximum((off_b + s * nch + j) * per - 1, 0), colblk))

    def state(width, colblk):
        return pl.BlockSpec((None, None, 1, width), lambda s, j: (shift_layer, s, 0, colblk))

    def stacked(arr):
        return pl.BlockSpec((None,) + arr.shape[1:], lambda s, j: (layer,) + (0,) * (arr.ndim - 1))

    def out(width):
        return pl.BlockSpec((tc, width), lambda s, j: (s * nch + j, 0))

    in_specs = [tile(RW, 0), tile(RW, 1), tile(RW, 2), tile(SMALL_W, sm_col),
                prev(RW, 0), prev(RW, 1), prev(RW, 2), prev(SMALL_W, sm_col),
                state(RW, 0), state(RW, 1), state(RW, 2), state(SMALL_W, sm_col),
                stacked(lw['vec']), stacked(lw['mu_sm']),
                stacked(lw['w2']), stacked(lw['a2']), stacked(lw['g2']),
                pl.BlockSpec(lw['rep'].shape, lambda s, j: (0, 0))]
    out_specs = ([out(rep_w)] * 4
                 + [pl.BlockSpec((SUBLANES, rep_w), lambda s, j: (s * nch + j, 0)),
                    out(3 * LANES), out(RW), out(RW)])
    out_shape = ([jax.ShapeDtypeStruct((rows, rep_w), F32)] * 4
                 + [jax.ShapeDtypeStruct((n_seq * nch * SUBLANES, rep_w), F32),
                    jax.ShapeDtypeStruct((rows, 3 * LANES), F32),
                    jax.ShapeDtypeStruct((rows, RW), F32), jax.ShapeDtypeStruct((rows, RW), F32)])
    body = functools.partial(_rwkv_prep_body, ld=ld, la=la, lgp=lgp, n_heads=RW // R_HEAD_DIM)
    return pl.pallas_call(
        body,
        grid=(n_seq, nch),
        in_specs=in_specs,
        out_specs=out_specs,
        out_shape=out_shape,
        compiler_params=_params(("parallel", "arbitrary")),
        name="rwkv_prep",
    )(proj_r, proj_r, proj_r, proj_r, proj_r, proj_r, proj_r, proj_r,
      shift, shift, shift, shift, lw['vec'], lw['mu_sm'], lw['w2'], lw['a2'], lw['g2'], lw['rep'])


def _rwkv_rec_body(nkk_ref, wr_ref, kka_ref, kmod_ref, pend_ref, v_ref, c_ref, s0_ref,
                   y_ref, sout_ref, s_ref, *, tc, n_grp, nch):
    j = pl.program_id(1)

    @pl.when(j == 0)
    def _():
        s_ref[...] = s0_ref[...]

    def row(ref, t, k):
        return jnp.broadcast_to(ref[t, pl.ds(k, 1), :], (SUBLANES, LANES))

    def reduce(t, tiles):
        acc = [[None, None] for _ in range(2 * n_grp)]
        for k in range(R_HEAD_DIM):
            nkk = row(nkk_ref, t, k)
            wr = row(wr_ref, t, k)
            for g, sg in enumerate(tiles(k)):
                for slot, term in ((g, sg * nkk), (n_grp + g, sg * wr)):
                    prev = acc[slot][k % 2]
                    acc[slot][k % 2] = term if prev is None else prev + term
        return tuple(a[0] + a[1] for a in acc)

    def step(t, carry):
        sa, yp = carry[:n_grp], carry[n_grp:]
        crow = c_ref[pl.ds(t, 1), :]
        c1 = crow[:, 0:LANES]
        c2 = crow[:, LANES:2 * LANES]
        vg = [v_ref[t, g * SUBLANES:(g + 1) * SUBLANES, :] for g in range(n_grp)]
        for g in range(n_grp):
            y_ref[t, g * SUBLANES:(g + 1) * SUBLANES, :] = yp[g] + sa[g] * c1 + vg[g] * c2

        def updated(k):
            kka = row(kka_ref, t, k)
            kmod = row(kmod_ref, t, k)
            new = [s_ref[g, k] + sa[g] * kka + vg[g] * kmod for g in range(n_grp)]
            for g in range(n_grp):
                s_ref[g, k] = new[g]
            return new

        return reduce(jnp.minimum(t + 1, tc - 1), updated)

    carry0 = reduce(0, lambda k: [s_ref[g, k] for g in range(n_grp)])
    lax.fori_loop(0, tc, step, carry0)

    for k in range(R_HEAD_DIM):
        p_end = row(pend_ref, SUBLANES - 1, k)
        for g in range(n_grp):
            s_ref[g, k] = s_ref[g, k] * p_end

    @pl.when(j == nch - 1)
    def _():
        sout_ref[...] = s_ref[...]


def rwkv_rec(nkk, wr, kka, kmod, pend, vmix, c, s0, s0_layer, n_seq, T):
    rows = n_seq * T
    n_grp = s0.shape[2]
    v_rows = n_grp * SUBLANES
    tc = min(SEQ_CHUNK, T)
    nch = T // tc
    k3 = lambda x: x.reshape(x.shape[0], R_HEAD_DIM, LANES)
    kspec = pl.BlockSpec((tc, R_HEAD_DIM, LANES), lambda s, j: (s * nch + j, 0, 0))
    pspec = pl.BlockSpec((SUBLANES, R_HEAD_DIM, LANES), lambda s, j: (s * nch + j, 0, 0))
    vspec = pl.BlockSpec((tc, v_rows, LANES), lambda s, j: (s * nch + j, 0, 0))
    tile = (n_grp, R_HEAD_DIM, SUBLANES, LANES)
    s_in = pl.BlockSpec((None, None) + tile, lambda s, j: (s0_layer, s, 0, 0, 0, 0))
    s_out = pl.BlockSpec((None,) + tile, lambda s, j: (s, 0, 0, 0, 0))
    body = functools.partial(_rwkv_rec_body, tc=tc, n_grp=n_grp, nch=nch)
    return pl.pallas_call(
        body,
        grid=(n_seq, nch),
        in_specs=[kspec] * 4 + [pspec, vspec, pl.BlockSpec((tc, 3 * LANES), lambda s, j: (s * nch + j, 0)), s_in],
        out_specs=[vspec, s_out],
        out_shape=[jax.ShapeDtypeStruct((rows, v_rows, LANES), F32),
                   jax.ShapeDtypeStruct(s0.shape[1:], F32)],
        scratch_shapes=[pltpu.VMEM(tile, F32)],
        compiler_params=_params(("parallel", "arbitrary")),
        name="rwkv_rec",
    )(k3(nkk), k3(wr), k3(kka), k3(kmod), k3(pend), vmix.reshape(rows, v_rows, LANES), c, s0)


def _rwkv_post_body(y_ref, v_ref, g_ref, c_ref, ln_ref, o_ref, *, n_grp, n_heads):
    def head_sum(x):
        return _tile_cols(_head_sum(x, n_heads), n_grp)

    y = y_ref[...]
    inv_n = 1.0 / R_HEAD_DIM
    mu = head_sum(y) * inv_n
    d = y - mu
    var = head_sum(d * d) * inv_n
    yln = d * lax.rsqrt(var + R_GN_EPS) * ln_ref[0:1, :] + ln_ref[1:2, :]
    c3 = _tile_cols(c_ref[:, 2 * LANES:3 * LANES], n_grp)
    o_ref[...] = ((yln + c3 * v_ref[...]) * g_ref[...]).astype(o_ref.dtype)


def rwkv_post(y, vmix, g, c, ln, layer, n_heads):
    M, RW = y.shape
    tm = _pick_tile(M, EW_ROWS)
    row = pl.BlockSpec((tm, RW), lambda i: (i, 0))
    body = functools.partial(_rwkv_post_body, n_grp=RW // LANES, n_heads=n_heads)
    return pl.pallas_call(
        body,
        grid=(M // tm,),
        in_specs=[row, row, row, pl.BlockSpec((tm, 3 * LANES), lambda i: (i, 0)),
                  pl.BlockSpec((None, 2, RW), lambda i: (layer, 0, 0))],
        out_specs=row,
        out_shape=jax.ShapeDtypeStruct((M, RW), BF16),
        compiler_params=_params(("parallel",)),
        name="rwkv_post",
    )(y, vmix, g, c, ln)


def _gla_body(q_ref, k_ref, v_ref, rg_ref, ag_ref, a2_ref, ab_ref, gn_ref, s0_ref,
              o_ref, sout_ref, st_ref, *, C, nch, n_heads, dk, dv):
    c = pl.program_id(1)

    @pl.when(c == 0)
    def _():
        st_ref[...] = s0_ref[...]

    x = jnp.dot(ag_ref[...], a2_ref[...], precision=HIGHEST, preferred_element_type=F32) + ab_ref[...]
    log_a = -_softplus(-x) * (1.0 / G_GATE_TAU)
    ti = lax.broadcasted_iota(jnp.int32, (C, C), 0)
    si = lax.broadcasted_iota(jnp.int32, (C, C), 1)
    tri = (si <= ti).astype(F32)
    b_all = jnp.dot(tri, log_a, precision=HIGHEST, preferred_element_type=F32)

    nt = (((1,), (1,)), ((), ()))
    tn = (((0,), (0,)), ((), ()))
    scale = float(dk) ** -0.5
    rowi = lax.broadcasted_iota(jnp.int32, (GLA_SUB, dk), 0)
    lane = lax.broadcasted_iota(jnp.int32, (GLA_SUB, LANES), 1)

    for h in range(n_heads):
        ksl = slice(h * dk, (h + 1) * dk)
        vsl = slice(h * dv, (h + 1) * dv)
        b = b_all[:, ksl]
        b_last = b[C - 1:C, :]
        q = q_ref[:, ksl] * scale
        k = k_ref[:, ksl]
        v = v_ref[:, vsl]
        vh = v.astype(BF16)
        st = st_ref[h]

        inter = lax.dot_general((q * jnp.exp(b)).astype(BF16), st.astype(BF16), nt,
                                preferred_element_type=F32)
        blocks = []
        for blk in range(C // GLA_SUB):
            t0 = blk * GLA_SUB
            rows = slice(t0, t0 + GLA_SUB)
            qb, kb, bb = q[rows], k[rows], b[rows]
            acc = inter[rows]
            if blk > 0:
                b0 = bb[0:1, :]
                qd = (qb * jnp.exp(bb - b0)).astype(BF16)
                kd = (k[0:t0] * jnp.exp(b0 - b[0:t0])).astype(BF16)
                sc = lax.dot_general(qd, kd, nt, preferred_element_type=F32)
                acc = acc + jnp.dot(sc.astype(BF16), vh[0:t0], preferred_element_type=F32)
            sc_d = jnp.zeros((GLA_SUB, LANES), F32)
            for s in range(GLA_SUB):
                e = jnp.where(rowi >= s, jnp.exp(bb - bb[s:s + 1, :]), 0.0)
                sc_s = jnp.sum(qb * e * kb[s:s + 1, :], axis=-1, keepdims=True)
                sc_d = jnp.where(lane == s, sc_s, sc_d)
            acc = acc + jnp.dot(sc_d[:, 0:GLA_SUB].astype(BF16), vh[rows], preferred_element_type=F32)
            blocks.append(acc)
        o = jnp.concatenate(blocks, axis=0) if len(blocks) > 1 else blocks[0]

        kdec = (k * jnp.exp(b_last - b)).astype(BF16)
        st_ref[h] = st * jnp.exp(b_last) + lax.dot_general(vh, kdec, tn, preferred_element_type=F32)

        ms = jnp.mean(o * o, axis=-1, keepdims=True)
        o_ref[:, vsl] = (o * lax.rsqrt(ms + NORM_EPS) * gn_ref[:, vsl]
                         * jax.nn.silu(rg_ref[:, vsl])).astype(o_ref.dtype)

    @pl.when(c == nch - 1)
    def _():
        sout_ref[...] = st_ref[...]


def gla(proj_g, proj_r, row_off, n_seq, T, s0t, s0_layer, lw, layer, o_ag):
    GK, GV = lw['gla_ab'].shape[-1], lw['gla_norm'].shape[-1]
    dk, dv = GK // G_HEADS, GV // G_HEADS
    C = min(SEQ_CHUNK, T)
    assert T % C == 0 and C % GLA_SUB == 0 and row_off % C == 0
    assert GV % GK == 0 and (2 * GK) % GV == 0 and o_ag % LANES == 0
    nch = T // C
    rows = n_seq * T
    off_b = row_off // C

    def tile(width, col0):
        assert col0 % width == 0
        return pl.BlockSpec((C, width), lambda s, c: (off_b + s * nch + c, col0 // width))

    def stacked(arr):
        return pl.BlockSpec((None,) + arr.shape[1:], lambda s, c: (layer,) + (0,) * (arr.ndim - 1))

    in_specs = [tile(GK, 0), tile(GK, GK), tile(GV, 2 * GK), tile(GV, 2 * GK + GV),
                tile(LANES, o_ag),
                stacked(lw['gla_a2']), stacked(lw['gla_ab']), stacked(lw['gla_norm']),
                pl.BlockSpec((None, None, G_HEADS, dv, dk), lambda s, c: (s0_layer, s, 0, 0, 0))]
    body = functools.partial(_gla_body, C=C, nch=nch, n_heads=G_HEADS, dk=dk, dv=dv)
    return pl.pallas_call(
        body,
        grid=(n_seq, nch),
        in_specs=in_specs,
        out_specs=[pl.BlockSpec((C, GV), lambda s, c: (s * nch + c, 0)),
                   pl.BlockSpec((None, G_HEADS, dv, dk), lambda s, c: (s, 0, 0, 0))],
        out_shape=[jax.ShapeDtypeStruct((rows, GV), BF16),
                   jax.ShapeDtypeStruct(s0t.shape[1:], F32)],
        scratch_shapes=[pltpu.VMEM((G_HEADS, dv, dk), F32)],
        compiler_params=_params(("parallel", "arbitrary")),
        name="gla",
    )(proj_g, proj_g, proj_g, proj_g, proj_r, lw['gla_a2'], lw['gla_ab'], lw['gla_norm'], s0t)


def _hx_to_xh(w, n_heads):
    sh = w.shape[:-1]
    x = w.shape[-1] // n_heads
    return jnp.swapaxes(w.reshape(*sh, n_heads, x), -1, -2).reshape(*sh, n_heads * x)


def _xh_to_hx(w, n_heads):
    sh = w.shape[:-1]
    x = w.shape[-1] // n_heads
    return jnp.swapaxes(w.reshape(*sh, x, n_heads), -1, -2).reshape(*sh, n_heads * x)


def _pad_last(x, width):
    return jnp.pad(x, [(0, 0)] * (x.ndim - 1) + [(0, width - x.shape[-1])])


def _split_r_cols(x, RW, ld, la, lg):
    offs = [0, RW, 2 * RW, 3 * RW, 3 * RW + ld, 3 * RW + ld + la, 3 * RW + ld + la + lg]
    return [x[..., offs[i]:offs[i + 1]] for i in range(6)]


def _small_block(wd, ad, gd, lgp, tail=None):
    parts = [wd, ad, _pad_last(gd, lgp)]
    if tail is not None:
        parts.append(tail)
    return _pad_last(jnp.concatenate(parts, axis=-1), SMALL_W)


def _r_cols_to_kernel(x, RW, ld, la, lg, lgp, n_heads, tail=None):
    x_r, x_k, x_v, x_wd, x_ad, x_gd = _split_r_cols(x, RW, ld, la, lg)
    return jnp.concatenate([_hx_to_xh(x_r, n_heads), _hx_to_xh(x_k, n_heads), _hx_to_xh(x_v, n_heads),
                            _small_block(x_wd, x_ad, x_gd, lgp, tail)], axis=-1)


def _r_cols_from_kernel(x, RW, ld, la, lg, n_heads):
    sm = x[..., 3 * RW:]
    return jnp.concatenate(
        [_xh_to_hx(x[..., 0:RW], n_heads), _xh_to_hx(x[..., RW:2 * RW], n_heads),
         _xh_to_hx(x[..., 2 * RW:3 * RW], n_heads),
         sm[..., 0:ld], sm[..., ld:ld + la], sm[..., ld + la:ld + la + lg]], axis=-1)


def _state_to_kernel(s, n_heads):
    lead = s.shape[:-3]
    n = len(lead)
    vv = LANES // n_heads
    g = R_HEAD_DIM // (SUBLANES * vv)
    x = s.reshape(*lead, n_heads, g, SUBLANES, vv, R_HEAD_DIM)
    x = x.transpose(*range(n), n + 1, n + 4, n + 2, n + 3, n)
    return x.reshape(*lead, g, R_HEAD_DIM, SUBLANES, LANES)


def _state_from_kernel(s, n_heads):
    lead = s.shape[:-4]
    n = len(lead)
    g = s.shape[-4]
    vv = LANES // n_heads
    x = s.reshape(*lead, g, R_HEAD_DIM, SUBLANES, vv, n_heads)
    x = x.transpose(*range(n), n + 4, n, n + 2, n + 3, n + 1)
    return x.reshape(*lead, n_heads, R_HEAD_DIM, R_HEAD_DIM)


def kernel(x_prompt, x_sample, state_shift, state_rwkv, state_gla, p_prompt, p_sample, norm_mix_pre, norm_mix_post, norm_ffn_pre, norm_ffn_post, norm_ple, w_in, rwkv_mu, rwkv_w0, rwkv_w2, rwkv_a0, rwkv_a2, rwkv_g2, rwkv_k_k, rwkv_k_a, rwkv_r_k, rwkv_ln_w, rwkv_ln_b, gla_a2, gla_ab, gla_norm, w_proj_rwkv, w_proj_gla, w_out, ffn_gate, ffn_up, ffn_down, ple_proj, ple_gate):
    Bp, Tp, D = x_prompt.shape
    Bs, Ts, _ = x_sample.shape
    L = w_in.shape[0]
    RW = rwkv_w0.shape[-1]
    H = RW // R_HEAD_DIM
    assert LANES % H == 0 and R_HEAD_DIM % (SUBLANES * (LANES // H)) == 0
    ld, la, lg = rwkv_w2.shape[1], rwkv_a2.shape[1], rwkv_g2.shape[1]
    lgp = _round_up(lg, LANES)
    GK, GV, GR = gla_a2.shape[-1], gla_norm.shape[-1], gla_a2.shape[1]
    FF = ffn_gate.shape[-1]
    Mp, Ms = Bp * Tp, Bs * Ts
    groups = [(0, Bp, Tp), (Mp, Bs, Ts)]

    NR = 3 * RW + SMALL_W
    NG = 2 * GK + 2 * GV
    o_ag = 3 * RW + ld + la + lgp
    assert o_ag + LANES <= NR and (3 * RW) % SMALL_W == 0 and GR <= LANES
    r_cols = 3 * RW + ld + la + lg
    g_cols = NG + GR
    to_kernel = functools.partial(_r_cols_to_kernel, RW=RW, ld=ld, la=la, lg=lg, lgp=lgp, n_heads=H)
    win_r = to_kernel(w_in[..., :r_cols], tail=w_in[..., r_cols + NG:r_cols + g_cols]).astype(BF16)
    win_g = w_in[..., r_cols:r_cols + NG].astype(BF16)
    win_gate = w_in[..., r_cols + g_cols:].astype(BF16)
    assert win_r.shape[-1] == NR and win_gate.shape[-1] == 2 * D

    mu_k = to_kernel(rwkv_mu)
    kq = LANES // H
    src = jnp.arange(LANES)
    dst = jnp.arange(kq * LANES)
    rep_mat = jnp.tile((src[:, None] == ((dst // LANES) * H + dst % H)[None, :]).astype(BF16), (3, 1))
    lw = dict(
        vec=jnp.stack([mu_k[:, 0:RW], mu_k[:, RW:2 * RW], mu_k[:, 2 * RW:3 * RW],
                       _hx_to_xh(rwkv_w0, H), _hx_to_xh(rwkv_a0, H), _hx_to_xh(rwkv_k_k, H),
                       _hx_to_xh(rwkv_k_a, H), _hx_to_xh(rwkv_r_k.reshape(L, RW), H)], axis=1),
        mu_sm=mu_k[:, None, 3 * RW:],
        w2=_hx_to_xh(rwkv_w2, H).astype(BF16), a2=_hx_to_xh(rwkv_a2, H).astype(BF16),
        g2=jnp.pad(_hx_to_xh(rwkv_g2, H), ((0, 0), (0, lgp - lg), (0, 0))).astype(BF16),
        rep=rep_mat, ld=ld, la=la, lgp=lgp,
        ln=jnp.stack([_hx_to_xh(rwkv_ln_w, H), _hx_to_xh(rwkv_ln_b, H)], axis=1),
        gla_a2=jnp.pad(gla_a2, ((0, 0), (0, LANES - GR), (0, 0))), gla_ab=gla_ab[:, None], gla_norm=gla_norm[:, None])
    gains = jnp.stack([norm_mix_pre, norm_mix_post, norm_ffn_pre, norm_ffn_post, norm_ple], axis=1)[:, :, None]
    w_pr = jnp.swapaxes(w_proj_rwkv.reshape(L, H, R_HEAD_DIM, D), 1, 2).reshape(L, RW, D).astype(BF16)
    w_pg, w_o, w_fg, w_fu, w_plg, w_plp = w_proj_gla, w_out, ffn_gate, ffn_up, ple_gate, ple_proj
    w_fd = cast_bf16(ffn_down)

    shift_states = [jnp.zeros((1, Bp, 1, NR), F32), to_kernel(state_shift)[:, :, None]]
    n_vgrp = R_HEAD_DIM // (SUBLANES * (LANES // H))
    rwkv_states = [jnp.zeros((1, Bp, n_vgrp, R_HEAD_DIM, SUBLANES, LANES), F32),
                   _state_to_kernel(state_rwkv, H)]
    gla_states = [jnp.zeros((1, Bp, G_HEADS, GV // G_HEADS, GK // G_HEADS), F32), jnp.swapaxes(state_gla, -1, -2)]

    h = jnp.concatenate([x_prompt.reshape(Mp, D), x_sample.reshape(Ms, D)], axis=0)
    p_all = jnp.concatenate([p_prompt.reshape(L, Mp, -1), p_sample.reshape(L, Ms, -1)], axis=1).astype(BF16)

    new_shift = [[], []]
    new_rwkv = [[], []]
    new_gla = [[], []]
    tn = _pick_tile(D, COL_TILE, LANES)
    for i in range(L):
        z = norm_cast(h, gains, i, G_MIX_PRE)
        proj_r = fused_mm([z], [(0, win_r)], i, [], _ep_identity, NR, F32, "in_proj_r")
        proj_g = fused_mm([z], [(0, win_g)], i, [], _ep_identity, NG, F32, "in_proj_g")
        proj_gate = fused_mm([z], [(0, win_gate)], i, [], _ep_sigmoid, 2 * D, F32, "in_proj_gate")

        o_r_parts, o_g_parts = [], []
        for gi, (row_off, n_seq, T) in enumerate(groups):
            s_layer = i if gi == 1 else 0
            nkk, wr, kka, kmod, pend, c, vmix, g = rwkv_prep(
                proj_r, row_off, n_seq, T, shift_states[gi], s_layer, lw, i)
            y, s_new = rwkv_rec(nkk, wr, kka, kmod, pend, vmix, c, rwkv_states[gi], s_layer, n_seq, T)
            o_r_parts.append(rwkv_post(y.reshape(n_seq * T, RW), vmix, g, c, lw['ln'], i, H))
            o_g, s_gla_new = gla(proj_g, proj_r, row_off, n_seq, T, gla_states[gi], s_layer, lw, i, o_ag)
            o_g_parts.append(o_g)
            new_shift[gi].append(proj_r[row_off + T - 1:row_off + n_seq * T:T])
            new_rwkv[gi].append(s_new)
            new_gla[gi].append(s_gla_new)
        o_r = jnp.concatenate(o_r_parts, axis=0)
        o_g = jnp.concatenate(o_g_parts, axis=0)

        merged = fused_mm([o_r, o_g], [(0, w_pr), (1, w_pg)], i, [(proj_gate, 0), (proj_gate, D // tn)],
                          _ep_merge, D, BF16, "merge", tm_target=ROW_TILE // 2)
        mix = fused_mm([merged], [(0, w_o)], i, [], _ep_identity, D, F32, "out_proj")
        h1, u = resid_norm(h, mix, gains, i, G_MIX_POST, G_FFN_PRE)
        t = fused_mm([u], [(0, w_fg), (0, w_fu)], i, [], _ep_swiglu, FF, BF16, "ffn_up", tn=COL_TILE // 2)
        f = fused_mm([t], [(0, w_fd)], i, [], _ep_identity, D, F32, "ffn_down",
                     tm_target=ROW_TILE // 2, tn=COL_TILE // 2)
        h2, pn = resid_norm(h1, f, gains, i, G_FFN_POST, G_PLE)
        h = fused_mm([pn, p_all[i]], [(0, w_plg), (1, w_plp)], i, [(h2, 0)], _ep_ple, D, F32, "ple_mm",
                     tm_target=ROW_TILE // 2)

    from_kernel = functools.partial(_r_cols_from_kernel, RW=RW, ld=ld, la=la, lg=lg, n_heads=H)
    y_prompt = h[:Mp].reshape(Bp, Tp, D)
    y_sample = h[Mp:].reshape(Bs, Ts, D)
    return (y_prompt, y_sample,
            from_kernel(jnp.stack(new_shift[0])), _state_from_kernel(jnp.stack(new_rwkv[0]), H),
            jnp.swapaxes(jnp.stack(new_gla[0]), -1, -2),
            from_kernel(jnp.stack(new_shift[1])), _state_from_kernel(jnp.stack(new_rwkv[1]), H),
            jnp.swapaxes(jnp.stack(new_gla[1]), -1, -2))
```

```python
import functools

import jax
import jax.numpy as jnp
from jax import lax
from jax.experimental import pallas as pl
from jax.experimental.pallas import tpu as pltpu

F32 = jnp.float32
BF16 = jnp.bfloat16
HIGHEST = lax.Precision.HIGHEST

NORM_EPS = 1e-6
R_GN_EPS = 64e-5
R_HEAD_DIM = 64
G_HEADS = 8
G_GATE_TAU = 16.0

LANES = 128
SUBLANES = 8
VMEM_LIMIT = 56 * 1024 * 1024

ROW_TILE = 1536
COL_TILE = 512
EW_ROWS = 256
CAST_TILE_BYTES = 8 * 1024 * 1024
SEQ_CHUNK = 64
GLA_SUB = 16
MM_PIECES = 8
SMALL_W = 1024

G_MIX_PRE, G_MIX_POST, G_FFN_PRE, G_FFN_POST, G_PLE = range(5)
V_MU_R, V_MU_K, V_MU_V, V_W0, V_A0, V_KK, V_KA, V_RK = range(8)


def _round_up(x, m):
    return (x + m - 1) // m * m


def _pick_tile(n, target, mult=SUBLANES):
    best = None
    for t in range(mult, min(n, target) + 1, mult):
        if n % t == 0:
            best = t
    assert best is not None, (n, target)
    return best


def _params(sem):
    return pltpu.CompilerParams(dimension_semantics=sem, vmem_limit_bytes=VMEM_LIMIT)


def _rms(x, gain):
    ms = jnp.mean(x * x, axis=-1, keepdims=True)
    return x * lax.rsqrt(ms + NORM_EPS) * gain


def _softplus(x):
    return jnp.maximum(x, 0.0) + jnp.log(1.0 + jnp.exp(-jnp.abs(x)))


def _gain_spec(D, layer, row):
    return pl.BlockSpec((None, None, 1, D), lambda *_: (layer, row, 0, 0))


def _norm_cast_body(x_ref, g_ref, o_ref, os_ref):
    z = _rms(x_ref[...], g_ref[...]).astype(o_ref.dtype)
    o_ref[...] = z
    kp = os_ref.shape[2]
    for p in range(os_ref.shape[0]):
        os_ref[p] = z[:, p * kp:(p + 1) * kp]


def norm_cast(x, gains, layer, row):
    M, D = x.shape
    tm = _pick_tile(M, EW_ROWS)
    kp = D // MM_PIECES
    return pl.pallas_call(
        _norm_cast_body,
        grid=(M // tm,),
        in_specs=[pl.BlockSpec((tm, D), lambda i: (i, 0)), _gain_spec(D, layer, row)],
        out_specs=[pl.BlockSpec((tm, D), lambda i: (i, 0)),
                   pl.BlockSpec((MM_PIECES, tm, kp), lambda i: (0, i, 0))],
        out_shape=[jax.ShapeDtypeStruct((M, D), BF16), jax.ShapeDtypeStruct((MM_PIECES, M, kp), BF16)],
        compiler_params=_params(("parallel",)),
        name="norm_cast",
    )(x, gains)


def _resid_norm_body(h_ref, x_ref, gpost_ref, gpre_ref, h1_ref, u_ref):
    h1 = h_ref[...] + _rms(x_ref[...], gpost_ref[...])
    h1_ref[...] = h1
    u_ref[...] = _rms(h1, gpre_ref[...]).astype(u_ref.dtype)


def resid_norm(h, x, gains, layer, row_post, row_pre):
    M, D = h.shape
    tm = _pick_tile(M, EW_ROWS)
    row = pl.BlockSpec((tm, D), lambda i: (i, 0))
    return pl.pallas_call(
        _resid_norm_body,
        grid=(M // tm,),
        in_specs=[row, row, _gain_spec(D, layer, row_post), _gain_spec(D, layer, row_pre)],
        out_specs=[row, row],
        out_shape=[jax.ShapeDtypeStruct((M, D), F32), jax.ShapeDtypeStruct((M, D), BF16)],
        compiler_params=_params(("parallel",)),
        name="resid_norm",
    )(h, x, gains, gains)


def _cast_body(x_ref, o_ref):
    o_ref[...] = x_ref[...].astype(o_ref.dtype)


def cast_bf16(w):
    L, K, N = w.shape
    rows = L * K
    tr = _pick_tile(rows, max(SUBLANES, CAST_TILE_BYTES // (4 * N) // SUBLANES * SUBLANES))
    out = pl.pallas_call(
        _cast_body,
        grid=(rows // tr,),
        in_specs=[pl.BlockSpec((tr, N), lambda i: (i, 0))],
        out_specs=pl.BlockSpec((tr, N), lambda i: (i, 0)),
        out_shape=jax.ShapeDtypeStruct((rows, N), BF16),
        compiler_params=_params(("parallel",)),
        name="cast_bf16",
    )(w.reshape(rows, N))
    return out.reshape(L, K, N)


def _mm_body(*refs, act_of, n_acts, n_extra, epilogue):
    n_w = len(act_of)
    a_refs = refs[:n_acts]
    w_refs = refs[n_acts:n_acts + n_w]
    e_refs = refs[n_acts + n_w:n_acts + n_w + n_extra]
    o_ref = refs[n_acts + n_w + n_extra]
    acts = [a[...] for a in a_refs]
    dots = [jnp.dot(acts[ai], w[...].astype(BF16), preferred_element_type=F32) for ai, w in zip(act_of, w_refs)]
    o_ref[...] = epilogue(dots, [e[...] for e in e_refs]).astype(o_ref.dtype)


def fused_mm(acts, weights, layer, extras, epilogue, n_out, out_dtype, name, tm_target=None, tn=None):
    M = acts[0].shape[0]
    tm = _pick_tile(M, ROW_TILE if tm_target is None else min(tm_target, ROW_TILE))
    tn = _pick_tile(n_out, COL_TILE, LANES) if tn is None else tn
    grid = (M // tm, pl.cdiv(n_out, tn))
    in_specs = ([pl.BlockSpec((tm, a.shape[1]), lambda i, j: (i, 0)) for a in acts]
                + [pl.BlockSpec((None, w.shape[1], tn), lambda i, j: (layer, 0, j)) for _, w in weights]
                + [pl.BlockSpec((tm, tn), functools.partial(lambda i, j, off: (i, j + off), off=off))
                   for _, off in extras])
    body = functools.partial(_mm_body, act_of=tuple(a for a, _ in weights), n_acts=len(acts),
                             n_extra=len(extras), epilogue=epilogue)
    return pl.pallas_call(
        body,
        grid=grid,
        in_specs=in_specs,
        out_specs=pl.BlockSpec((tm, tn), lambda i, j: (i, j)),
        out_shape=jax.ShapeDtypeStruct((M, n_out), out_dtype),
        compiler_params=_params(("parallel", "parallel")),
        name=name,
    )(*acts, *[w for _, w in weights], *[e for e, _ in extras])


def _ep_identity(dots, extras):
    return dots[0]


def _ep_merge(dots, extras):
    return jax.nn.sigmoid(extras[0]) * dots[0] + jax.nn.sigmoid(extras[1]) * dots[1]


def _ep_swiglu(dots, extras):
    return jax.nn.silu(dots[0]) * dots[1]


def _ep_ple(dots, extras):
    return extras[0] + jax.nn.sigmoid(dots[0]) * dots[1]


def _shift_mix(x, prev_first, mu):
    rolled = pltpu.roll(x, 1, axis=0)
    row = lax.broadcasted_iota(jnp.int32, x.shape, 0)
    prev = jnp.where(row == 0, prev_first, rolled)
    return x + (prev - x) * mu


def _head_sum(x, n_heads):
    s = x[:, 0:LANES]
    for b in range(1, x.shape[1] // LANES):
        s = s + x[:, b * LANES:(b + 1) * LANES]
    shift = LANES // 2
    while shift >= n_heads:
        s = s + pltpu.roll(s, shift, axis=1)
        shift //= 2
    return s


def _tile_cols(x, n):
    return jnp.concatenate([x] * n, axis=1)


def _replicate(x, rep, out_ref):
    rows = x.shape[0]
    n_blk = x.shape[1] // LANES
    grp = rep.shape[1]
    xs = jnp.concatenate([x[:, b * LANES:(b + 1) * LANES] for b in range(n_blk)], axis=0)
    hi = xs.astype(BF16)
    rest = xs - hi.astype(F32)
    mid = rest.astype(BF16)
    lo = (rest - mid.astype(F32)).astype(BF16)
    y = jnp.dot(jnp.concatenate([hi, mid, lo], axis=1), rep, preferred_element_type=F32)
    for b in range(n_blk):
        out_ref[:, b * grp:(b + 1) * grp] = y[b * rows:(b + 1) * rows, :]


def _rwkv_prep_body(r_ref, k_ref, v_ref, sm_ref, rp_ref, kp_ref, vp_ref, smp_ref,
                    rs_ref, ks_ref, vs_ref, sms_ref, vec_ref, musm_ref,
                    w2_ref, a2_ref, g2_ref, rep_ref,
                    nkk_ref, wr_ref, kka_ref, kmod_ref, pend_ref, c_ref, vmix_ref, g_ref,
                    *, ld, la, lgp, n_heads):
    j = pl.program_id(1)
    first = j == 0

    def prev_row(p_ref, s_ref):
        return jnp.where(first, s_ref[...], p_ref[SUBLANES - 1:SUBLANES, :])

    def vec(row):
        return vec_ref[row:row + 1, :]

    r = _shift_mix(r_ref[...], prev_row(rp_ref, rs_ref), vec(V_MU_R))
    k = _shift_mix(k_ref[...], prev_row(kp_ref, ks_ref), vec(V_MU_K))
    vmix_ref[...] = _shift_mix(v_ref[...], prev_row(vp_ref, vs_ref), vec(V_MU_V))
    sm = _shift_mix(sm_ref[...], prev_row(smp_ref, sms_ref), musm_ref[...])

    wd = jnp.tanh(sm[:, 0:ld]).astype(BF16)
    ad = sm[:, ld:ld + la].astype(BF16)
    gd = jax.nn.sigmoid(sm[:, ld + la:ld + la + lgp]).astype(BF16)
    g_ref[...] = jnp.dot(gd, g2_ref[...], preferred_element_type=F32)

    tc = r.shape[0]
    n_blk = r.shape[1] // LANES
    w = -_softplus(-(vec(V_W0) + jnp.dot(wd, w2_ref[...], preferred_element_type=F32))) - 0.5
    neg_log_decay = jnp.exp(w)
    a = jax.nn.sigmoid(vec(V_A0) + jnp.dot(ad, a2_ref[...], preferred_element_type=F32))
    kk = k * vec(V_KK)
    inv_norm = 1.0 / jnp.maximum(jnp.sqrt(_head_sum(kk * kk, n_heads)), 1e-12)
    kkn = kk * _tile_cols(inv_norm, n_blk)
    kmod = k * (1.0 + (a - 1.0) * vec(V_KA))
    kka = kkn * a
    rk = r * kmod
    c_ref[:, 0:LANES] = _head_sum(kka * r, n_heads)
    c_ref[:, LANES:2 * LANES] = _head_sum(rk, n_heads)
    c_ref[:, 2 * LANES:3 * LANES] = _head_sum(rk * vec(V_RK), n_heads)

    ti = lax.broadcasted_iota(jnp.int32, (tc, tc), 0)
    si = lax.broadcasted_iota(jnp.int32, (tc, tc), 1)
    tri = (si <= ti).astype(F32)
    log_p = -jnp.dot(tri, neg_log_decay, precision=HIGHEST, preferred_element_type=F32)
    p_incl = jnp.exp(log_p)
    p_prev = jnp.exp(log_p + neg_log_decay)
    inv_p = jnp.exp(-log_p)

    rep = rep_ref[...]
    _replicate(-kkn * p_prev, rep, nkk_ref)
    _replicate(r * p_incl, rep, wr_ref)
    _replicate(kka * inv_p, rep, kka_ref)
    _replicate(kmod * inv_p, rep, kmod_ref)
    _replicate(p_incl[tc - SUBLANES:tc, :], rep, pend_ref)


def rwkv_prep(proj_r, row_off, n_seq, T, shift, shift_layer, lw, layer):
    RW = lw['vec'].shape[-1]
    rep_w = R_HEAD_DIM * LANES
    tc = min(SEQ_CHUNK, T)
    assert T % tc == 0 and row_off % tc == 0 and tc % SUBLANES == 0
    nch = T // tc
    rows = n_seq * T
    off_b = row_off // tc
    sm_col = 3 * RW // SMALL_W
    ld, la, lgp = lw['ld'], lw['la'], lw['lgp']

    def tile(width, colblk):
        return pl.BlockSpec((tc, width), lambda s, j: (off_b + s * nch + j, colblk))

    def prev(width, colblk):
        per = tc // SUBLANES
        return pl.BlockSpec((SUBLANES, width),
                            lambda s, j: (jnp.maximum((off_b + s * nch + j) * per - 1, 0), colblk))

    def state(width, colblk):
        return pl.BlockSpec((None, None, 1, width), lambda s, j: (shift_layer, s, 0, colblk))

    def stacked(arr):
        return pl.BlockSpec((None,) + arr.shape[1:], lambda s, j: (layer,) + (0,) * (arr.ndim - 1))

    def out(width):
        return pl.BlockSpec((tc, width), lambda s, j: (s * nch + j, 0))

    in_specs = [tile(RW, 0), tile(RW, 1), tile(RW, 2), tile(SMALL_W, sm_col),
                prev(RW, 0), prev(RW, 1), prev(RW, 2), prev(SMALL_W, sm_col),
                state(RW, 0), state(RW, 1), state(RW, 2), state(SMALL_W, sm_col),
                stacked(lw['vec']), stacked(lw['mu_sm']),
                stacked(lw['w2']), stacked(lw['a2']), stacked(lw['g2']),
                pl.BlockSpec(lw['rep'].shape, lambda s, j: (0, 0))]
    out_specs = ([out(rep_w)] * 4
                 + [pl.BlockSpec((SUBLANES, rep_w), lambda s, j: (s * nch + j, 0)),
                    out(3 * LANES), out(RW), out(RW)])
    out_shape = ([jax.ShapeDtypeStruct((rows, rep_w), F32)] * 4
                 + [jax.ShapeDtypeStruct((n_seq * nch * SUBLANES, rep_w), F32),
                    jax.ShapeDtypeStruct((rows, 3 * LANES), F32),
                    jax.ShapeDtypeStruct((rows, RW), F32), jax.ShapeDtypeStruct((rows, RW), F32)])
    body = functools.partial(_rwkv_prep_body, ld=ld, la=la, lgp=lgp, n_heads=RW // R_HEAD_DIM)
    return pl.pallas_call(
        body,
        grid=(n_seq, nch),
        in_specs=in_specs,
        out_specs=out_specs,
        out_shape=out_shape,
        compiler_params=_params(("parallel", "arbitrary")),
        name="rwkv_prep",
    )(proj_r, proj_r, proj_r, proj_r, proj_r, proj_r, proj_r, proj_r,
      shift, shift, shift, shift, lw['vec'], lw['mu_sm'], lw['w2'], lw['a2'], lw['g2'], lw['rep'])


def _rwkv_rec_body(*refs, tc, n_grp, nch, mm_pieces):
    if mm_pieces:
        (nkk_ref, wr_ref, kka_ref, kmod_ref, pend_ref, v_ref, c_ref, s0_ref, a_ref, w_ref,
         y_ref, sout_ref, mm_ref, s_ref) = refs
    else:
        (nkk_ref, wr_ref, kka_ref, kmod_ref, pend_ref, v_ref, c_ref, s0_ref,
         y_ref, sout_ref, s_ref) = refs
    j = pl.program_id(1)

    @pl.when(j == 0)
    def _():
        s_ref[...] = s0_ref[...]

    def row(ref, t, k):
        return jnp.broadcast_to(ref[t, pl.ds(k, 1), :], (SUBLANES, LANES))

    def reduce(t, tiles):
        acc = [[None, None] for _ in range(2 * n_grp)]
        for k in range(R_HEAD_DIM):
            nkk = row(nkk_ref, t, k)
            wr = row(wr_ref, t, k)
            for g, sg in enumerate(tiles(k)):
                for slot, term in ((g, sg * nkk), (n_grp + g, sg * wr)):
                    prev = acc[slot][k % 2]
                    acc[slot][k % 2] = term if prev is None else prev + term
        return tuple(a[0] + a[1] for a in acc)

    def step(t, carry):
        sa, yp = carry[:n_grp], carry[n_grp:]
        crow = c_ref[pl.ds(t, 1), :]
        c1 = crow[:, 0:LANES]
        c2 = crow[:, LANES:2 * LANES]
        vg = [v_ref[t, g * SUBLANES:(g + 1) * SUBLANES, :] for g in range(n_grp)]
        for g in range(n_grp):
            y_ref[t, g * SUBLANES:(g + 1) * SUBLANES, :] = yp[g] + sa[g] * c1 + vg[g] * c2

        def updated(k):
            kka = row(kka_ref, t, k)
            kmod = row(kmod_ref, t, k)
            new = [s_ref[g, k] + sa[g] * kka + vg[g] * kmod for g in range(n_grp)]
            for g in range(n_grp):
                s_ref[g, k] = new[g]
            return new

        return reduce(jnp.minimum(t + 1, tc - 1), updated)

    carry0 = reduce(0, lambda k: [s_ref[g, k] for g in range(n_grp)])
    if mm_pieces:
        per = tc // mm_pieces
        kslice = w_ref.shape[0] // mm_pieces

        def steps_and_piece(it, carry, first=False):
            rows = pl.ds(0 if first else pl.multiple_of(it * kslice, kslice), kslice)
            part = jnp.dot(a_ref[it], w_ref[rows, :], preferred_element_type=F32)
            mm_ref[...] = part if first else mm_ref[...] + part
            for u in range(per):
                carry = step(it * per + u, carry)
            return carry

        lax.fori_loop(1, mm_pieces, steps_and_piece, steps_and_piece(0, carry0, first=True))
    else:
        lax.fori_loop(0, tc, step, carry0)

    for k in range(R_HEAD_DIM):
        p_end = row(pend_ref, SUBLANES - 1, k)
        for g in range(n_grp):
            s_ref[g, k] = s_ref[g, k] * p_end

    @pl.when(j == nch - 1)
    def _():
        sout_ref[...] = s_ref[...]


def _mm_tiling(n_steps, M, N):
    for n_col in range(min(n_steps, N // COL_TILE), 0, -1):
        if n_steps % n_col or N % n_col or (N // n_col) % LANES:
            continue
        n_row = n_steps // n_col
        if M % n_row == 0 and (M // n_row) % SUBLANES == 0:
            return n_row, n_col
    return None


def rwkv_rec(nkk, wr, kka, kmod, pend, vmix, c, s0, s0_layer, n_seq, T, mm=None):
    rows = n_seq * T
    n_grp = s0.shape[2]
    v_rows = n_grp * SUBLANES
    tc = min(SEQ_CHUNK, T)
    nch = T // tc
    k3 = lambda x: x.reshape(x.shape[0], R_HEAD_DIM, LANES)
    kspec = pl.BlockSpec((tc, R_HEAD_DIM, LANES), lambda s, j: (s * nch + j, 0, 0))
    pspec = pl.BlockSpec((SUBLANES, R_HEAD_DIM, LANES), lambda s, j: (s * nch + j, 0, 0))
    vspec = pl.BlockSpec((tc, v_rows, LANES), lambda s, j: (s * nch + j, 0, 0))
    tile = (n_grp, R_HEAD_DIM, SUBLANES, LANES)
    s_in = pl.BlockSpec((None, None) + tile, lambda s, j: (s0_layer, s, 0, 0, 0, 0))
    s_out = pl.BlockSpec((None,) + tile, lambda s, j: (s, 0, 0, 0, 0))
    in_specs = [kspec] * 4 + [pspec, vspec, pl.BlockSpec((tc, 3 * LANES), lambda s, j: (s * nch + j, 0)), s_in]
    out_specs = [vspec, s_out]
    out_shape = [jax.ShapeDtypeStruct((rows, v_rows, LANES), F32), jax.ShapeDtypeStruct(s0.shape[1:], F32)]
    scratch = [pltpu.VMEM(tile, F32)]
    operands = [k3(nkk), k3(wr), k3(kka), k3(kmod), k3(pend), vmix.reshape(rows, v_rows, LANES), c, s0]
    mm_pieces = 0
    if mm is not None:
        a_split, w, layer = mm
        mm_pieces, M, k_piece = a_split.shape
        N = w.shape[2]
        n_row, n_col = _mm_tiling(n_seq * nch, M, N)
        tm, tn = M // n_row, N // n_col
        assert tc % mm_pieces == 0 and w.shape[1] == mm_pieces * k_piece
        in_specs += [pl.BlockSpec((mm_pieces, tm, k_piece), lambda s, j: (0, (s * nch + j) // n_col, 0)),
                     pl.BlockSpec((None, w.shape[1], tn), lambda s, j: (layer, 0, (s * nch + j) % n_col))]
        out_specs.append(pl.BlockSpec((tm, tn), lambda s, j: ((s * nch + j) // n_col, (s * nch + j) % n_col)))
        out_shape.append(jax.ShapeDtypeStruct((M, N), F32))
        operands += [a_split, w]
    body = functools.partial(_rwkv_rec_body, tc=tc, n_grp=n_grp, nch=nch, mm_pieces=mm_pieces)
    return pl.pallas_call(
        body,
        grid=(n_seq, nch),
        in_specs=in_specs,
        out_specs=out_specs,
        out_shape=out_shape,
        scratch_shapes=scratch,
        compiler_params=_params(("arbitrary", "arbitrary") if mm is not None else ("parallel", "arbitrary")),
        name="rwkv_rec",
    )(*operands)


def _rwkv_post_body(y_ref, v_ref, g_ref, c_ref, ln_ref, o_ref, *, n_grp, n_heads):
    def head_sum(x):
        return _tile_cols(_head_sum(x, n_heads), n_grp)

    y = y_ref[...]
    inv_n = 1.0 / R_HEAD_DIM
    mu = head_sum(y) * inv_n
    d = y - mu
    var = head_sum(d * d) * inv_n
    yln = d * lax.rsqrt(var + R_GN_EPS) * ln_ref[0:1, :] + ln_ref[1:2, :]
    c3 = _tile_cols(c_ref[:, 2 * LANES:3 * LANES], n_grp)
    o_ref[...] = ((yln + c3 * v_ref[...]) * g_ref[...]).astype(o_ref.dtype)


def rwkv_post(y, vmix, g, c, ln, layer, n_heads):
    M, RW = y.shape
    tm = _pick_tile(M, EW_ROWS)
    row = pl.BlockSpec((tm, RW), lambda i: (i, 0))
    body = functools.partial(_rwkv_post_body, n_grp=RW // LANES, n_heads=n_heads)
    return pl.pallas_call(
        body,
        grid=(M // tm,),
        in_specs=[row, row, row, pl.BlockSpec((tm, 3 * LANES), lambda i: (i, 0)),
                  pl.BlockSpec((None, 2, RW), lambda i: (layer, 0, 0))],
        out_specs=row,
        out_shape=jax.ShapeDtypeStruct((M, RW), BF16),
        compiler_params=_params(("parallel",)),
        name="rwkv_post",
    )(y, vmix, g, c, ln)


def _gla_body(q_ref, k_ref, v_ref, rg_ref, ag_ref, a2_ref, ab_ref, gn_ref, s0_ref,
              o_ref, sout_ref, st_ref, *, C, nch, n_heads, dk, dv):
    c = pl.program_id(1)

    @pl.when(c == 0)
    def _():
        st_ref[...] = s0_ref[...]

    x = jnp.dot(ag_ref[...], a2_ref[...], precision=HIGHEST, preferred_element_type=F32) + ab_ref[...]
    log_a = -_softplus(-x) * (1.0 / G_GATE_TAU)
    ti = lax.broadcasted_iota(jnp.int32, (C, C), 0)
    si = lax.broadcasted_iota(jnp.int32, (C, C), 1)
    tri = (si <= ti).astype(F32)
    b_all = jnp.dot(tri, log_a, precision=HIGHEST, preferred_element_type=F32)

    nt = (((1,), (1,)), ((), ()))
    tn = (((0,), (0,)), ((), ()))
    scale = float(dk) ** -0.5
    rowi = lax.broadcasted_iota(jnp.int32, (GLA_SUB, dk), 0)
    lane = lax.broadcasted_iota(jnp.int32, (GLA_SUB, LANES), 1)

    for h in range(n_heads):
        ksl = slice(h * dk, (h + 1) * dk)
        vsl = slice(h * dv, (h + 1) * dv)
        b = b_all[:, ksl]
        b_last = b[C - 1:C, :]
        q = q_ref[:, ksl] * scale
        k = k_ref[:, ksl]
        v = v_ref[:, vsl]
        vh = v.astype(BF16)
        st = st_ref[h]

        inter = lax.dot_general((q * jnp.exp(b)).astype(BF16), st.astype(BF16), nt,
                                preferred_element_type=F32)
        blocks = []
        for blk in range(C // GLA_SUB):
            t0 = blk * GLA_SUB
            rows = slice(t0, t0 + GLA_SUB)
            qb, kb, bb = q[rows], k[rows], b[rows]
            acc = inter[rows]
            if blk > 0:
                b0 = bb[0:1, :]
                qd = (qb * jnp.exp(bb - b0)).astype(BF16)
                kd = (k[0:t0] * jnp.exp(b0 - b[0:t0])).astype(BF16)
                sc = lax.dot_general(qd, kd, nt, preferred_element_type=F32)
                acc = acc + jnp.dot(sc.astype(BF16), vh[0:t0], preferred_element_type=F32)
            sc_d = jnp.zeros((GLA_SUB, LANES), F32)
            for s in range(GLA_SUB):
                e = jnp.where(rowi >= s, jnp.exp(bb - bb[s:s + 1, :]), 0.0)
                sc_s = jnp.sum(qb * e * kb[s:s + 1, :], axis=-1, keepdims=True)
                sc_d = jnp.where(lane == s, sc_s, sc_d)
            acc = acc + jnp.dot(sc_d[:, 0:GLA_SUB].astype(BF16), vh[rows], preferred_element_type=F32)
            blocks.append(acc)
        o = jnp.concatenate(blocks, axis=0) if len(blocks) > 1 else blocks[0]

        kdec = (k * jnp.exp(b_last - b)).astype(BF16)
        st_ref[h] = st * jnp.exp(b_last) + lax.dot_general(vh, kdec, tn, preferred_element_type=F32)

        ms = jnp.mean(o * o, axis=-1, keepdims=True)
        o_ref[:, vsl] = (o * lax.rsqrt(ms + NORM_EPS) * gn_ref[:, vsl]
                         * jax.nn.silu(rg_ref[:, vsl])).astype(o_ref.dtype)

    @pl.when(c == nch - 1)
    def _():
        sout_ref[...] = st_ref[...]


def gla(proj_g, proj_r, row_off, n_seq, T, s0t, s0_layer, lw, layer, o_ag):
    GK, GV = lw['gla_ab'].shape[-1], lw['gla_norm'].shape[-1]
    dk, dv = GK // G_HEADS, GV // G_HEADS
    C = min(SEQ_CHUNK, T)
    assert T % C == 0 and C % GLA_SUB == 0 and row_off % C == 0
    assert GV % GK == 0 and (2 * GK) % GV == 0 and o_ag % LANES == 0
    nch = T // C
    rows = n_seq * T
    off_b = row_off // C

    def tile(width, col0):
        assert col0 % width == 0
        return pl.BlockSpec((C, width), lambda s, c: (off_b + s * nch + c, col0 // width))

    def stacked(arr):
        return pl.BlockSpec((None,) + arr.shape[1:], lambda s, c: (layer,) + (0,) * (arr.ndim - 1))

    in_specs = [tile(GK, 0), tile(GK, GK), tile(GV, 2 * GK), tile(GV, 2 * GK + GV),
                tile(LANES, o_ag),
                stacked(lw['gla_a2']), stacked(lw['gla_ab']), stacked(lw['gla_norm']),
                pl.BlockSpec((None, None, G_HEADS, dv, dk), lambda s, c: (s0_layer, s, 0, 0, 0))]
    body = functools.partial(_gla_body, C=C, nch=nch, n_heads=G_HEADS, dk=dk, dv=dv)
    return pl.pallas_call(
        body,
        grid=(n_seq, nch),
        in_specs=in_specs,
        out_specs=[pl.BlockSpec((C, GV), lambda s, c: (s * nch + c, 0)),
                   pl.BlockSpec((None, G_HEADS, dv, dk), lambda s, c: (s, 0, 0, 0))],
        out_shape=[jax.ShapeDtypeStruct((rows, GV), BF16),
                   jax.ShapeDtypeStruct(s0t.shape[1:], F32)],
        scratch_shapes=[pltpu.VMEM((G_HEADS, dv, dk), F32)],
        compiler_params=_params(("parallel", "arbitrary")),
        name="gla",
    )(proj_g, proj_g, proj_g, proj_g, proj_r, lw['gla_a2'], lw['gla_ab'], lw['gla_norm'], s0t)


def _hx_to_xh(w, n_heads):
    sh = w.shape[:-1]
    x = w.shape[-1] // n_heads
    return jnp.swapaxes(w.reshape(*sh, n_heads, x), -1, -2).reshape(*sh, n_heads * x)


def _xh_to_hx(w, n_heads):
    sh = w.shape[:-1]
    x = w.shape[-1] // n_heads
    return jnp.swapaxes(w.reshape(*sh, x, n_heads), -1, -2).reshape(*sh, n_heads * x)


def _pad_last(x, width):
    return jnp.pad(x, [(0, 0)] * (x.ndim - 1) + [(0, width - x.shape[-1])])


def _split_r_cols(x, RW, ld, la, lg):
    offs = [0, RW, 2 * RW, 3 * RW, 3 * RW + ld, 3 * RW + ld + la, 3 * RW + ld + la + lg]
    return [x[..., offs[i]:offs[i + 1]] for i in range(6)]


def _small_block(wd, ad, gd, lgp, tail=None):
    parts = [wd, ad, _pad_last(gd, lgp)]
    if tail is not None:
        parts.append(tail)
    return _pad_last(jnp.concatenate(parts, axis=-1), SMALL_W)


def _r_cols_to_kernel(x, RW, ld, la, lg, lgp, n_heads, tail=None):
    x_r, x_k, x_v, x_wd, x_ad, x_gd = _split_r_cols(x, RW, ld, la, lg)
    return jnp.concatenate([_hx_to_xh(x_r, n_heads), _hx_to_xh(x_k, n_heads), _hx_to_xh(x_v, n_heads),
                            _small_block(x_wd, x_ad, x_gd, lgp, tail)], axis=-1)


def _r_cols_from_kernel(x, RW, ld, la, lg, n_heads):
    sm = x[..., 3 * RW:]
    return jnp.concatenate(
        [_xh_to_hx(x[..., 0:RW], n_heads), _xh_to_hx(x[..., RW:2 * RW], n_heads),
         _xh_to_hx(x[..., 2 * RW:3 * RW], n_heads),
         sm[..., 0:ld], sm[..., ld:ld + la], sm[..., ld + la:ld + la + lg]], axis=-1)


def _state_to_kernel(s, n_heads):
    lead = s.shape[:-3]
    n = len(lead)
    vv = LANES // n_heads
    g = R_HEAD_DIM // (SUBLANES * vv)
    x = s.reshape(*lead, n_heads, g, SUBLANES, vv, R_HEAD_DIM)
    x = x.transpose(*range(n), n + 1, n + 4, n + 2, n + 3, n)
    return x.reshape(*lead, g, R_HEAD_DIM, SUBLANES, LANES)


def _state_from_kernel(s, n_heads):
    lead = s.shape[:-4]
    n = len(lead)
    g = s.shape[-4]
    vv = LANES // n_heads
    x = s.reshape(*lead, g, R_HEAD_DIM, SUBLANES, vv, n_heads)
    x = x.transpose(*range(n), n + 4, n, n + 2, n + 3, n + 1)
    return x.reshape(*lead, n_heads, R_HEAD_DIM, R_HEAD_DIM)


def kernel(x_prompt, x_sample, state_shift, state_rwkv, state_gla, p_prompt, p_sample, norm_mix_pre, norm_mix_post, norm_ffn_pre, norm_ffn_post, norm_ple, w_in, rwkv_mu, rwkv_w0, rwkv_w2, rwkv_a0, rwkv_a2, rwkv_g2, rwkv_k_k, rwkv_k_a, rwkv_r_k, rwkv_ln_w, rwkv_ln_b, gla_a2, gla_ab, gla_norm, w_proj_rwkv, w_proj_gla, w_out, ffn_gate, ffn_up, ffn_down, ple_proj, ple_gate):
    Bp, Tp, D = x_prompt.shape
    Bs, Ts, _ = x_sample.shape
    L = w_in.shape[0]
    RW = rwkv_w0.shape[-1]
    H = RW // R_HEAD_DIM
    assert LANES % H == 0 and R_HEAD_DIM % (SUBLANES * (LANES // H)) == 0
    ld, la, lg = rwkv_w2.shape[1], rwkv_a2.shape[1], rwkv_g2.shape[1]
    lgp = _round_up(lg, LANES)
    GK, GV, GR = gla_a2.shape[-1], gla_norm.shape[-1], gla_a2.shape[1]
    FF = ffn_gate.shape[-1]
    Mp, Ms = Bp * Tp, Bs * Ts
    groups = [(0, Bp, Tp), (Mp, Bs, Ts)]

    NR = 3 * RW + SMALL_W
    NG = 2 * GK + 2 * GV
    o_ag = 3 * RW + ld + la + lgp
    assert o_ag + LANES <= NR and (3 * RW) % SMALL_W == 0 and GR <= LANES
    r_cols = 3 * RW + ld + la + lg
    g_cols = NG + GR
    to_kernel = functools.partial(_r_cols_to_kernel, RW=RW, ld=ld, la=la, lg=lg, lgp=lgp, n_heads=H)
    win_r = to_kernel(w_in[..., :r_cols], tail=w_in[..., r_cols + NG:r_cols + g_cols]).astype(BF16)
    win_g = w_in[..., r_cols:r_cols + NG].astype(BF16)
    win_gate = w_in[..., r_cols + g_cols:].astype(BF16)
    assert win_r.shape[-1] == NR and win_gate.shape[-1] == 2 * D

    mu_k = to_kernel(rwkv_mu)
    kq = LANES // H
    src = jnp.arange(LANES)
    dst = jnp.arange(kq * LANES)
    rep_mat = jnp.tile((src[:, None] == ((dst // LANES) * H + dst % H)[None, :]).astype(BF16), (3, 1))
    lw = dict(
        vec=jnp.stack([mu_k[:, 0:RW], mu_k[:, RW:2 * RW], mu_k[:, 2 * RW:3 * RW],
                       _hx_to_xh(rwkv_w0, H), _hx_to_xh(rwkv_a0, H), _hx_to_xh(rwkv_k_k, H),
                       _hx_to_xh(rwkv_k_a, H), _hx_to_xh(rwkv_r_k.reshape(L, RW), H)], axis=1),
        mu_sm=mu_k[:, None, 3 * RW:],
        w2=_hx_to_xh(rwkv_w2, H).astype(BF16), a2=_hx_to_xh(rwkv_a2, H).astype(BF16),
        g2=jnp.pad(_hx_to_xh(rwkv_g2, H), ((0, 0), (0, lgp - lg), (0, 0))).astype(BF16),
        rep=rep_mat, ld=ld, la=la, lgp=lgp,
        ln=jnp.stack([_hx_to_xh(rwkv_ln_w, H), _hx_to_xh(rwkv_ln_b, H)], axis=1),
        gla_a2=jnp.pad(gla_a2, ((0, 0), (0, LANES - GR), (0, 0))), gla_ab=gla_ab[:, None], gla_norm=gla_norm[:, None])
    gains = jnp.stack([norm_mix_pre, norm_mix_post, norm_ffn_pre, norm_ffn_post, norm_ple], axis=1)[:, :, None]
    w_pr = jnp.swapaxes(w_proj_rwkv.reshape(L, H, R_HEAD_DIM, D), 1, 2).reshape(L, RW, D).astype(BF16)
    w_pg, w_o, w_fg, w_fu, w_plg, w_plp = w_proj_gla, w_out, ffn_gate, ffn_up, ple_gate, ple_proj
    w_fd = cast_bf16(ffn_down)

    shift_states = [jnp.zeros((1, Bp, 1, NR), F32), to_kernel(state_shift)[:, :, None]]
    n_vgrp = R_HEAD_DIM // (SUBLANES * (LANES // H))
    rwkv_states = [jnp.zeros((1, Bp, n_vgrp, R_HEAD_DIM, SUBLANES, LANES), F32),
                   _state_to_kernel(state_rwkv, H)]
    gla_states = [jnp.zeros((1, Bp, G_HEADS, GV // G_HEADS, GK // G_HEADS), F32), jnp.swapaxes(state_gla, -1, -2)]

    h = jnp.concatenate([x_prompt.reshape(Mp, D), x_sample.reshape(Ms, D)], axis=0)
    p_all = jnp.concatenate([p_prompt.reshape(L, Mp, -1), p_sample.reshape(L, Ms, -1)], axis=1).astype(BF16)

    new_shift = [[], []]
    new_rwkv = [[], []]
    new_gla = [[], []]
    tn = _pick_tile(D, COL_TILE, LANES)
    for i in range(L):
        z, z_split = norm_cast(h, gains, i, G_MIX_PRE)
        proj_r = fused_mm([z], [(0, win_r)], i, [], _ep_identity, NR, F32, "in_proj_r")
        proj_g = fused_mm([z], [(0, win_g)], i, [], _ep_identity, NG, F32, "in_proj_g")
        gate_under_rec = (Tp % SEQ_CHUNK == 0 and SEQ_CHUNK % MM_PIECES == 0
                          and _mm_tiling(Bp * Tp // SEQ_CHUNK, Mp + Ms, 2 * D) is not None)
        proj_gate = None if gate_under_rec else fused_mm(
            [z], [(0, win_gate)], i, [], _ep_identity, 2 * D, F32, "in_proj_gate")

        o_r_parts, o_g_parts = [], []
        for gi, (row_off, n_seq, T) in enumerate(groups):
            s_layer = i if gi == 1 else 0
            nkk, wr, kka, kmod, pend, c, vmix, g = rwkv_prep(
                proj_r, row_off, n_seq, T, shift_states[gi], s_layer, lw, i)
            if gi == 0 and gate_under_rec:
                y, s_new, proj_gate = rwkv_rec(nkk, wr, kka, kmod, pend, vmix, c, rwkv_states[gi], s_layer,
                                               n_seq, T, mm=(z_split, win_gate, i))
            else:
                y, s_new = rwkv_rec(nkk, wr, kka, kmod, pend, vmix, c, rwkv_states[gi], s_layer, n_seq, T)
            o_r_parts.append(rwkv_post(y.reshape(n_seq * T, RW), vmix, g, c, lw['ln'], i, H))
            o_g, s_gla_new = gla(proj_g, proj_r, row_off, n_seq, T, gla_states[gi], s_layer, lw, i, o_ag)
            o_g_parts.append(o_g)
            new_shift[gi].append(proj_r[row_off + T - 1:row_off + n_seq * T:T])
            new_rwkv[gi].append(s_new)
            new_gla[gi].append(s_gla_new)
        o_r = jnp.concatenate(o_r_parts, axis=0)
        o_g = jnp.concatenate(o_g_parts, axis=0)

        merged = fused_mm([o_r, o_g], [(0, w_pr), (1, w_pg)], i, [(proj_gate, 0), (proj_gate, D // tn)],
                          _ep_merge, D, BF16, "merge", tm_target=ROW_TILE // 2)
        mix = fused_mm([merged], [(0, w_o)], i, [], _ep_identity, D, F32, "out_proj")
        h1, u = resid_norm(h, mix, gains, i, G_MIX_POST, G_FFN_PRE)
        t = fused_mm([u], [(0, w_fg), (0, w_fu)], i, [], _ep_swiglu, FF, BF16, "ffn_up", tn=COL_TILE // 2)
        f = fused_mm([t], [(0, w_fd)], i, [], _ep_identity, D, F32, "ffn_down",
                     tm_target=ROW_TILE // 2, tn=COL_TILE // 2)
        h2, pn = resid_norm(h1, f, gains, i, G_FFN_POST, G_PLE)
        h = fused_mm([pn, p_all[i]], [(0, w_plg), (1, w_plp)], i, [(h2, 0)], _ep_ple, D, F32, "ple_mm",
                     tm_target=ROW_TILE // 2)

    from_kernel = functools.partial(_r_cols_from_kernel, RW=RW, ld=ld, la=la, lg=lg, n_heads=H)
    y_prompt = h[:Mp].reshape(Bp, Tp, D)
    y_sample = h[Mp:].reshape(Bs, Ts, D)
    return (y_prompt, y_sample,
            from_kernel(jnp.stack(new_shift[0])), _state_from_kernel(jnp.stack(new_rwkv[0]), H),
            jnp.swapaxes(jnp.stack(new_gla[0]), -1, -2),
            from_kernel(jnp.stack(new_shift[1])), _state_from_kernel(jnp.stack(new_rwkv[1]), H),
            jnp.swapaxes(jnp.stack(new_gla[1]), -1, -2))
```

```python
import functools

import jax
import jax.numpy as jnp
from jax import lax
from jax.experimental import pallas as pl
from jax.experimental.pallas import tpu as pltpu

F32 = jnp.float32
BF16 = jnp.bfloat16
HIGHEST = lax.Precision.HIGHEST

NORM_EPS = 1e-6
R_GN_EPS = 64e-5
R_HEAD_DIM = 64
G_HEADS = 8
G_GATE_TAU = 16.0

LANES = 128
SUBLANES = 8
VMEM_LIMIT = 56 * 1024 * 1024

ROW_TILE = 1536
COL_TILE = 512
EW_ROWS = 256
CAST_TILE_BYTES = 8 * 1024 * 1024
SEQ_CHUNK = 64
GLA_SUB = 16
MM_PIECES = 8
SMALL_W = 1024

G_MIX_PRE, G_MIX_POST, G_FFN_PRE, G_FFN_POST, G_PLE = range(5)
V_MU_R, V_MU_K, V_MU_V, V_W0, V_A0, V_KK, V_KA, V_RK = range(8)


def _round_up(x, m):
    return (x + m - 1) // m * m


def _pick_tile(n, target, mult=SUBLANES):
    best = None
    for t in range(mult, min(n, target) + 1, mult):
        if n % t == 0:
            best = t
    assert best is not None, (n, target)
    return best


def _params(sem):
    return pltpu.CompilerParams(dimension_semantics=sem, vmem_limit_bytes=VMEM_LIMIT)


def _rms(x, gain):
    ms = jnp.mean(x * x, axis=-1, keepdims=True)
    return x * lax.rsqrt(ms + NORM_EPS) * gain


def _softplus(x):
    return jnp.maximum(x, 0.0) + jnp.log(1.0 + jnp.exp(-jnp.abs(x)))


def _gain_spec(D, layer, row):
    return pl.BlockSpec((None, None, 1, D), lambda *_: (layer, row, 0, 0))


def _norm_cast_body(x_ref, g_ref, o_ref, os_ref):
    z = _rms(x_ref[...], g_ref[...]).astype(o_ref.dtype)
    o_ref[...] = z
    kp = os_ref.shape[2]
    for p in range(os_ref.shape[0]):
        os_ref[p] = z[:, p * kp:(p + 1) * kp]


def norm_cast(x, gains, layer, row):
    M, D = x.shape
    tm = _pick_tile(M, EW_ROWS)
    kp = D // MM_PIECES
    return pl.pallas_call(
        _norm_cast_body,
        grid=(M // tm,),
        in_specs=[pl.BlockSpec((tm, D), lambda i: (i, 0)), _gain_spec(D, layer, row)],
        out_specs=[pl.BlockSpec((tm, D), lambda i: (i, 0)),
                   pl.BlockSpec((MM_PIECES, tm, kp), lambda i: (0, i, 0))],
        out_shape=[jax.ShapeDtypeStruct((M, D), BF16), jax.ShapeDtypeStruct((MM_PIECES, M, kp), BF16)],
        compiler_params=_params(("parallel",)),
        name="norm_cast",
    )(x, gains)


def _resid_norm_body(h_ref, x_ref, gpost_ref, gpre_ref, h1_ref, u_ref):
    h1 = h_ref[...] + _rms(x_ref[...], gpost_ref[...])
    h1_ref[...] = h1
    u_ref[...] = _rms(h1, gpre_ref[...]).astype(u_ref.dtype)


def resid_norm(h, x, gains, layer, row_post, row_pre):
    M, D = h.shape
    tm = _pick_tile(M, EW_ROWS)
    row = pl.BlockSpec((tm, D), lambda i: (i, 0))
    return pl.pallas_call(
        _resid_norm_body,
        grid=(M // tm,),
        in_specs=[row, row, _gain_spec(D, layer, row_post), _gain_spec(D, layer, row_pre)],
        out_specs=[row, row],
        out_shape=[jax.ShapeDtypeStruct((M, D), F32), jax.ShapeDtypeStruct((M, D), BF16)],
        compiler_params=_params(("parallel",)),
        name="resid_norm",
    )(h, x, gains, gains)


def _cast_body(x_ref, o_ref):
    o_ref[...] = x_ref[...].astype(o_ref.dtype)


def cast_bf16(w):
    L, K, N = w.shape
    rows = L * K
    tr = _pick_tile(rows, max(SUBLANES, CAST_TILE_BYTES // (4 * N) // SUBLANES * SUBLANES))
    out = pl.pallas_call(
        _cast_body,
        grid=(rows // tr,),
        in_specs=[pl.BlockSpec((tr, N), lambda i: (i, 0))],
        out_specs=pl.BlockSpec((tr, N), lambda i: (i, 0)),
        out_shape=jax.ShapeDtypeStruct((rows, N), BF16),
        compiler_params=_params(("parallel",)),
        name="cast_bf16",
    )(w.reshape(rows, N))
    return out.reshape(L, K, N)


def _mm_body(*refs, act_of, n_acts, n_extra, epilogue):
    n_w = len(act_of)
    a_refs = refs[:n_acts]
    w_refs = refs[n_acts:n_acts + n_w]
    e_refs = refs[n_acts + n_w:n_acts + n_w + n_extra]
    o_ref = refs[n_acts + n_w + n_extra]
    acts = [a[...] for a in a_refs]
    dots = [jnp.dot(acts[ai], w[...].astype(BF16), preferred_element_type=F32) for ai, w in zip(act_of, w_refs)]
    o_ref[...] = epilogue(dots, [e[...] for e in e_refs]).astype(o_ref.dtype)


def fused_mm(acts, weights, layer, extras, epilogue, n_out, out_dtype, name, tm_target=None, tn=None):
    M = acts[0].shape[0]
    tm = _pick_tile(M, ROW_TILE if tm_target is None else min(tm_target, ROW_TILE))
    tn = _pick_tile(n_out, COL_TILE, LANES) if tn is None else tn
    grid = (M // tm, pl.cdiv(n_out, tn))
    in_specs = ([pl.BlockSpec((tm, a.shape[1]), lambda i, j: (i, 0)) for a in acts]
                + [pl.BlockSpec((None, w.shape[1], tn), lambda i, j: (layer, 0, j)) for _, w in weights]
                + [pl.BlockSpec((tm, tn), functools.partial(lambda i, j, off: (i, j + off), off=off))
                   for _, off in extras])
    body = functools.partial(_mm_body, act_of=tuple(a for a, _ in weights), n_acts=len(acts),
                             n_extra=len(extras), epilogue=epilogue)
    return pl.pallas_call(
        body,
        grid=grid,
        in_specs=in_specs,
        out_specs=pl.BlockSpec((tm, tn), lambda i, j: (i, j)),
        out_shape=jax.ShapeDtypeStruct((M, n_out), out_dtype),
        compiler_params=_params(("parallel", "parallel")),
        name=name,
    )(*acts, *[w for _, w in weights], *[e for e, _ in extras])


def _ep_identity(dots, extras):
    return dots[0]


def _ep_merge(dots, extras):
    return jax.nn.sigmoid(extras[0]) * dots[0] + jax.nn.sigmoid(extras[1]) * dots[1]


def _ep_swiglu(dots, extras):
    return jax.nn.silu(dots[0]) * dots[1]


def _ep_ple(dots, extras):
    return extras[0] + jax.nn.sigmoid(dots[0]) * dots[1]


def _shift_mix(x, prev_first, mu):
    rolled = pltpu.roll(x, 1, axis=0)
    row = lax.broadcasted_iota(jnp.int32, x.shape, 0)
    prev = jnp.where(row == 0, prev_first, rolled)
    return x + (prev - x) * mu


def _head_sum(x, n_heads):
    s = x[:, 0:LANES]
    for b in range(1, x.shape[1] // LANES):
        s = s + x[:, b * LANES:(b + 1) * LANES]
    shift = LANES // 2
    while shift >= n_heads:
        s = s + pltpu.roll(s, shift, axis=1)
        shift //= 2
    return s


def _tile_cols(x, n):
    return jnp.concatenate([x] * n, axis=1)


def _replicate(x, rep, out_ref):
    rows = x.shape[0]
    n_blk = x.shape[1] // LANES
    grp = rep.shape[1]
    xs = jnp.concatenate([x[:, b * LANES:(b + 1) * LANES] for b in range(n_blk)], axis=0)
    hi = xs.astype(BF16)
    rest = xs - hi.astype(F32)
    mid = rest.astype(BF16)
    lo = (rest - mid.astype(F32)).astype(BF16)
    y = jnp.dot(jnp.concatenate([hi, mid, lo], axis=1), rep, preferred_element_type=F32)
    for b in range(n_blk):
        out_ref[:, b * grp:(b + 1) * grp] = y[b * rows:(b + 1) * rows, :]


def _rwkv_prep_body(r_ref, k_ref, v_ref, sm_ref, rp_ref, kp_ref, vp_ref, smp_ref,
                    rs_ref, ks_ref, vs_ref, sms_ref, vec_ref, musm_ref,
                    w2_ref, a2_ref, g2_ref, rep_ref,
                    nkk_ref, wr_ref, kka_ref, kmod_ref, pend_ref, c_ref, vmix_ref, g_ref,
                    *, ld, la, lgp, n_heads):
    j = pl.program_id(1)
    first = j == 0

    def prev_row(p_ref, s_ref):
        return jnp.where(first, s_ref[...], p_ref[SUBLANES - 1:SUBLANES, :])

    def vec(row):
        return vec_ref[row:row + 1, :]

    r = _shift_mix(r_ref[...], prev_row(rp_ref, rs_ref), vec(V_MU_R))
    k = _shift_mix(k_ref[...], prev_row(kp_ref, ks_ref), vec(V_MU_K))
    vmix_ref[...] = _shift_mix(v_ref[...], prev_row(vp_ref, vs_ref), vec(V_MU_V))
    sm = _shift_mix(sm_ref[...], prev_row(smp_ref, sms_ref), musm_ref[...])

    wd = jnp.tanh(sm[:, 0:ld]).astype(BF16)
    ad = sm[:, ld:ld + la].astype(BF16)
    gd = jax.nn.sigmoid(sm[:, ld + la:ld + la + lgp]).astype(BF16)
    g_ref[...] = jnp.dot(gd, g2_ref[...], preferred_element_type=F32)

    tc = r.shape[0]
    n_blk = r.shape[1] // LANES
    w = -_softplus(-(vec(V_W0) + jnp.dot(wd, w2_ref[...], preferred_element_type=F32))) - 0.5
    neg_log_decay = jnp.exp(w)
    a = jax.nn.sigmoid(vec(V_A0) + jnp.dot(ad, a2_ref[...], preferred_element_type=F32))
    kk = k * vec(V_KK)
    inv_norm = 1.0 / jnp.maximum(jnp.sqrt(_head_sum(kk * kk, n_heads)), 1e-12)
    kkn = kk * _tile_cols(inv_norm, n_blk)
    kmod = k * (1.0 + (a - 1.0) * vec(V_KA))
    kka = kkn * a
    rk = r * kmod
    c_ref[:, 0:LANES] = _head_sum(kka * r, n_heads)
    c_ref[:, LANES:2 * LANES] = _head_sum(rk, n_heads)
    c_ref[:, 2 * LANES:3 * LANES] = _head_sum(rk * vec(V_RK), n_heads)

    ti = lax.broadcasted_iota(jnp.int32, (tc, tc), 0)
    si = lax.broadcasted_iota(jnp.int32, (tc, tc), 1)
    tri = (si <= ti).astype(F32)
    log_p = -jnp.dot(tri, neg_log_decay, precision=HIGHEST, preferred_element_type=F32)
    p_incl = jnp.exp(log_p)
    p_prev = jnp.exp(log_p + neg_log_decay)
    inv_p = jnp.exp(-log_p)

    rep = rep_ref[...]
    _replicate(-kkn * p_prev, rep, nkk_ref)
    _replicate(r * p_incl, rep, wr_ref)
    _replicate(kka * inv_p, rep, kka_ref)
    _replicate(kmod * inv_p, rep, kmod_ref)
    _replicate(p_incl[tc - SUBLANES:tc, :], rep, pend_ref)


def rwkv_prep(proj_r, row_off, n_seq, T, shift, shift_layer, lw, layer):
    RW = lw['vec'].shape[-1]
    rep_w = R_HEAD_DIM * LANES
    tc = min(SEQ_CHUNK, T)
    assert T % tc == 0 and row_off % tc == 0 and tc % SUBLANES == 0
    nch = T // tc
    rows = n_seq * T
    off_b = row_off // tc
    sm_col = 3 * RW // SMALL_W
    ld, la, lgp = lw['ld'], lw['la'], lw['lgp']

    def tile(width, colblk):
        return pl.BlockSpec((tc, width), lambda s, j: (off_b + s * nch + j, colblk))

    def prev(width, colblk):
        per = tc // SUBLANES
        return pl.BlockSpec((SUBLANES, width),
                            lambda s, j: (jnp.maximum((off_b + s * nch + j) * per - 1, 0), colblk))

    def state(width, colblk):
        return pl.BlockSpec((None, None, 1, width), lambda s, j: (shift_layer, s, 0, colblk))

    def stacked(arr):
        return pl.BlockSpec((None,) + arr.shape[1:], lambda s, j: (layer,) + (0,) * (arr.ndim - 1))

    def out(width):
        return pl.BlockSpec((tc, width), lambda s, j: (s * nch + j, 0))

    in_specs = [tile(RW, 0), tile(RW, 1), tile(RW, 2), tile(SMALL_W, sm_col),
                prev(RW, 0), prev(RW, 1), prev(RW, 2), prev(SMALL_W, sm_col),
                state(RW, 0), state(RW, 1), state(RW, 2), state(SMALL_W, sm_col),
                stacked(lw['vec']), stacked(lw['mu_sm']),
                stacked(lw['w2']), stacked(lw['a2']), stacked(lw['g2']),
                pl.BlockSpec(lw['rep'].shape, lambda s, j: (0, 0))]
    out_specs = ([out(rep_w)] * 4
                 + [pl.BlockSpec((SUBLANES, rep_w), lambda s, j: (s * nch + j, 0)),
                    out(3 * LANES), out(RW), out(RW)])
    out_shape = ([jax.ShapeDtypeStruct((rows, rep_w), F32)] * 4
                 + [jax.ShapeDtypeStruct((n_seq * nch * SUBLANES, rep_w), F32),
                    jax.ShapeDtypeStruct((rows, 3 * LANES), F32),
                    jax.ShapeDtypeStruct((rows, RW), F32), jax.ShapeDtypeStruct((rows, RW), F32)])
    body = functools.partial(_rwkv_prep_body, ld=ld, la=la, lgp=lgp, n_heads=RW // R_HEAD_DIM)
    return pl.pallas_call(
        body,
        grid=(n_seq, nch),
        in_specs=in_specs,
        out_specs=out_specs,
        out_shape=out_shape,
        compiler_params=_params(("parallel", "arbitrary")),
        name="rwkv_prep",
    )(proj_r, proj_r, proj_r, proj_r, proj_r, proj_r, proj_r, proj_r,
      shift, shift, shift, shift, lw['vec'], lw['mu_sm'], lw['w2'], lw['a2'], lw['g2'], lw['rep'])


def _rwkv_rec_body(*refs, tc, n_grp, nch, mm_pieces):
    if mm_pieces:
        (nkk_ref, wr_ref, kka_ref, kmod_ref, pend_ref, v_ref, c_ref, s0_ref, a_ref, w_ref,
         y_ref, sout_ref, mm_ref, s_ref) = refs
    else:
        (nkk_ref, wr_ref, kka_ref, kmod_ref, pend_ref, v_ref, c_ref, s0_ref,
         y_ref, sout_ref, s_ref) = refs
    j = pl.program_id(1)

    @pl.when(j == 0)
    def _():
        s_ref[...] = s0_ref[...]

    def row(ref, t, k):
        return jnp.broadcast_to(ref[t, pl.ds(k, 1), :], (SUBLANES, LANES))

    def reduce(t, tiles):
        acc = [[None, None] for _ in range(2 * n_grp)]
        for k in range(R_HEAD_DIM):
            nkk = row(nkk_ref, t, k)
            wr = row(wr_ref, t, k)
            for g, sg in enumerate(tiles(k)):
                for slot, term in ((g, sg * nkk), (n_grp + g, sg * wr)):
                    prev = acc[slot][k % 2]
                    acc[slot][k % 2] = term if prev is None else prev + term
        return tuple(a[0] + a[1] for a in acc)

    def step(t, carry):
        sa, yp = carry[:n_grp], carry[n_grp:]
        crow = c_ref[pl.ds(t, 1), :]
        c1 = crow[:, 0:LANES]
        c2 = crow[:, LANES:2 * LANES]
        vg = [v_ref[t, g * SUBLANES:(g + 1) * SUBLANES, :] for g in range(n_grp)]
        for g in range(n_grp):
            y_ref[t, g * SUBLANES:(g + 1) * SUBLANES, :] = yp[g] + sa[g] * c1 + vg[g] * c2

        def updated(k):
            kka = row(kka_ref, t, k)
            kmod = row(kmod_ref, t, k)
            new = [s_ref[g, k] + sa[g] * kka + vg[g] * kmod for g in range(n_grp)]
            for g in range(n_grp):
                s_ref[g, k] = new[g]
            return new

        return reduce(jnp.minimum(t + 1, tc - 1), updated)

    carry0 = reduce(0, lambda k: [s_ref[g, k] for g in range(n_grp)])
    if mm_pieces:
        per = tc // mm_pieces
        kslice = w_ref.shape[0] // mm_pieces

        def steps_and_piece(it, carry, first=False):
            rows = pl.ds(0 if first else pl.multiple_of(it * kslice, kslice), kslice)
            part = jnp.dot(a_ref[it], w_ref[rows, :], preferred_element_type=F32)
            mm_ref[...] = part if first else mm_ref[...] + part
            for u in range(per):
                carry = step(it * per + u, carry)
            return carry

        lax.fori_loop(1, mm_pieces, steps_and_piece, steps_and_piece(0, carry0, first=True))
    else:
        lax.fori_loop(0, tc, step, carry0)

    for k in range(R_HEAD_DIM):
        p_end = row(pend_ref, SUBLANES - 1, k)
        for g in range(n_grp):
            s_ref[g, k] = s_ref[g, k] * p_end

    @pl.when(j == nch - 1)
    def _():
        sout_ref[...] = s_ref[...]


def _mm_tiling(n_steps, M, N):
    for n_col in range(min(n_steps, N // COL_TILE), 0, -1):
        if n_steps % n_col or N % n_col or (N // n_col) % LANES:
            continue
        n_row = n_steps // n_col
        if M % n_row == 0 and (M // n_row) % SUBLANES == 0:
            return n_row, n_col
    return None


def rwkv_rec(nkk, wr, kka, kmod, pend, vmix, c, s0, s0_layer, n_seq, T, mm=None):
    rows = n_seq * T
    n_grp = s0.shape[2]
    v_rows = n_grp * SUBLANES
    tc = min(SEQ_CHUNK, T)
    nch = T // tc
    k3 = lambda x: x.reshape(x.shape[0], R_HEAD_DIM, LANES)
    kspec = pl.BlockSpec((tc, R_HEAD_DIM, LANES), lambda s, j: (s * nch + j, 0, 0))
    pspec = pl.BlockSpec((SUBLANES, R_HEAD_DIM, LANES), lambda s, j: (s * nch + j, 0, 0))
    vspec = pl.BlockSpec((tc, v_rows, LANES), lambda s, j: (s * nch + j, 0, 0))
    tile = (n_grp, R_HEAD_DIM, SUBLANES, LANES)
    s_in = pl.BlockSpec((None, None) + tile, lambda s, j: (s0_layer, s, 0, 0, 0, 0))
    s_out = pl.BlockSpec((None,) + tile, lambda s, j: (s, 0, 0, 0, 0))
    in_specs = [kspec] * 4 + [pspec, vspec, pl.BlockSpec((tc, 3 * LANES), lambda s, j: (s * nch + j, 0)), s_in]
    out_specs = [vspec, s_out]
    out_shape = [jax.ShapeDtypeStruct((rows, v_rows, LANES), F32), jax.ShapeDtypeStruct(s0.shape[1:], F32)]
    scratch = [pltpu.VMEM(tile, F32)]
    operands = [k3(nkk), k3(wr), k3(kka), k3(kmod), k3(pend), vmix.reshape(rows, v_rows, LANES), c, s0]
    mm_pieces = 0
    if mm is not None:
        a_split, w, layer = mm
        mm_pieces, M, k_piece = a_split.shape
        n_col, tn = w.shape[1], w.shape[3]
        N = n_col * tn
        n_row = n_seq * nch // n_col
        tm = M // n_row
        assert n_row * n_col == n_seq * nch and M % n_row == 0
        assert tc % mm_pieces == 0 and w.shape[2] == mm_pieces * k_piece
        in_specs += [pl.BlockSpec((mm_pieces, tm, k_piece), lambda s, j: (0, (s * nch + j) // n_col, 0)),
                     pl.BlockSpec((None, None, w.shape[2], tn), lambda s, j: (layer, (s * nch + j) % n_col, 0, 0))]
        out_specs.append(pl.BlockSpec((tm, tn), lambda s, j: ((s * nch + j) // n_col, (s * nch + j) % n_col)))
        out_shape.append(jax.ShapeDtypeStruct((M, N), F32))
        operands += [a_split, w]
    body = functools.partial(_rwkv_rec_body, tc=tc, n_grp=n_grp, nch=nch, mm_pieces=mm_pieces)
    return pl.pallas_call(
        body,
        grid=(n_seq, nch),
        in_specs=in_specs,
        out_specs=out_specs,
        out_shape=out_shape,
        scratch_shapes=scratch,
        compiler_params=_params(("arbitrary", "arbitrary") if mm is not None else ("parallel", "arbitrary")),
        name="rwkv_rec",
    )(*operands)


def _rwkv_post_body(y_ref, v_ref, g_ref, c_ref, ln_ref, o_ref, *, n_grp, n_heads):
    def head_sum(x):
        return _tile_cols(_head_sum(x, n_heads), n_grp)

    y = y_ref[...]
    inv_n = 1.0 / R_HEAD_DIM
    mu = head_sum(y) * inv_n
    d = y - mu
    var = head_sum(d * d) * inv_n
    yln = d * lax.rsqrt(var + R_GN_EPS) * ln_ref[0:1, :] + ln_ref[1:2, :]
    c3 = _tile_cols(c_ref[:, 2 * LANES:3 * LANES], n_grp)
    o_ref[...] = ((yln + c3 * v_ref[...]) * g_ref[...]).astype(o_ref.dtype)


def rwkv_post(y, vmix, g, c, ln, layer, n_heads):
    M, RW = y.shape
    tm = _pick_tile(M, EW_ROWS)
    row = pl.BlockSpec((tm, RW), lambda i: (i, 0))
    body = functools.partial(_rwkv_post_body, n_grp=RW // LANES, n_heads=n_heads)
    return pl.pallas_call(
        body,
        grid=(M // tm,),
        in_specs=[row, row, row, pl.BlockSpec((tm, 3 * LANES), lambda i: (i, 0)),
                  pl.BlockSpec((None, 2, RW), lambda i: (layer, 0, 0))],
        out_specs=row,
        out_shape=jax.ShapeDtypeStruct((M, RW), BF16),
        compiler_params=_params(("parallel",)),
        name="rwkv_post",
    )(y, vmix, g, c, ln)


def _gla_body(q_ref, k_ref, v_ref, rg_ref, ag_ref, a2_ref, ab_ref, gn_ref, s0_ref,
              o_ref, sout_ref, st_ref, *, C, nch, n_heads, dk, dv):
    c = pl.program_id(1)

    @pl.when(c == 0)
    def _():
        st_ref[...] = s0_ref[...]

    x = jnp.dot(ag_ref[...], a2_ref[...], precision=HIGHEST, preferred_element_type=F32) + ab_ref[...]
    log_a = -_softplus(-x) * (1.0 / G_GATE_TAU)
    ti = lax.broadcasted_iota(jnp.int32, (C, C), 0)
    si = lax.broadcasted_iota(jnp.int32, (C, C), 1)
    tri = (si <= ti).astype(F32)
    b_all = jnp.dot(tri, log_a, precision=HIGHEST, preferred_element_type=F32)

    nt = (((1,), (1,)), ((), ()))
    tn = (((0,), (0,)), ((), ()))
    scale = float(dk) ** -0.5
    rowi = lax.broadcasted_iota(jnp.int32, (GLA_SUB, dk), 0)
    lane = lax.broadcasted_iota(jnp.int32, (GLA_SUB, LANES), 1)

    for h in range(n_heads):
        ksl = slice(h * dk, (h + 1) * dk)
        vsl = slice(h * dv, (h + 1) * dv)
        b = b_all[:, ksl]
        b_last = b[C - 1:C, :]
        q = q_ref[:, ksl] * scale
        k = k_ref[:, ksl]
        v = v_ref[:, vsl]
        vh = v.astype(BF16)
        st = st_ref[h]

        inter = lax.dot_general((q * jnp.exp(b)).astype(BF16), st.astype(BF16), nt,
                                preferred_element_type=F32)
        blocks = []
        for blk in range(C // GLA_SUB):
            t0 = blk * GLA_SUB
            rows = slice(t0, t0 + GLA_SUB)
            qb, kb, bb = q[rows], k[rows], b[rows]
            acc = inter[rows]
            if blk > 0:
                b0 = bb[0:1, :]
                qd = (qb * jnp.exp(bb - b0)).astype(BF16)
                kd = (k[0:t0] * jnp.exp(b0 - b[0:t0])).astype(BF16)
                sc = lax.dot_general(qd, kd, nt, preferred_element_type=F32)
                acc = acc + jnp.dot(sc.astype(BF16), vh[0:t0], preferred_element_type=F32)
            sc_d = jnp.zeros((GLA_SUB, LANES), F32)
            for s in range(GLA_SUB):
                e = jnp.where(rowi >= s, jnp.exp(bb - bb[s:s + 1, :]), 0.0)
                sc_s = jnp.sum(qb * e * kb[s:s + 1, :], axis=-1, keepdims=True)
                sc_d = jnp.where(lane == s, sc_s, sc_d)
            acc = acc + jnp.dot(sc_d[:, 0:GLA_SUB].astype(BF16), vh[rows], preferred_element_type=F32)
            blocks.append(acc)
        o = jnp.concatenate(blocks, axis=0) if len(blocks) > 1 else blocks[0]

        kdec = (k * jnp.exp(b_last - b)).astype(BF16)
        st_ref[h] = st * jnp.exp(b_last) + lax.dot_general(vh, kdec, tn, preferred_element_type=F32)

        ms = jnp.mean(o * o, axis=-1, keepdims=True)
        o_ref[:, vsl] = (o * lax.rsqrt(ms + NORM_EPS) * gn_ref[:, vsl]
                         * jax.nn.silu(rg_ref[:, vsl])).astype(o_ref.dtype)

    @pl.when(c == nch - 1)
    def _():
        sout_ref[...] = st_ref[...]


def gla(proj_g, proj_r, row_off, n_seq, T, s0t, s0_layer, lw, layer, o_ag):
    GK, GV = lw['gla_ab'].shape[-1], lw['gla_norm'].shape[-1]
    dk, dv = GK // G_HEADS, GV // G_HEADS
    C = min(SEQ_CHUNK, T)
    assert T % C == 0 and C % GLA_SUB == 0 and row_off % C == 0
    assert GV % GK == 0 and (2 * GK) % GV == 0 and o_ag % LANES == 0
    nch = T // C
    rows = n_seq * T
    off_b = row_off // C

    def tile(width, col0):
        assert col0 % width == 0
        return pl.BlockSpec((C, width), lambda s, c: (off_b + s * nch + c, col0 // width))

    def stacked(arr):
        return pl.BlockSpec((None,) + arr.shape[1:], lambda s, c: (layer,) + (0,) * (arr.ndim - 1))

    in_specs = [tile(GK, 0), tile(GK, GK), tile(GV, 2 * GK), tile(GV, 2 * GK + GV),
                tile(LANES, o_ag),
                stacked(lw['gla_a2']), stacked(lw['gla_ab']), stacked(lw['gla_norm']),
                pl.BlockSpec((None, None, G_HEADS, dv, dk), lambda s, c: (s0_layer, s, 0, 0, 0))]
    body = functools.partial(_gla_body, C=C, nch=nch, n_heads=G_HEADS, dk=dk, dv=dv)
    return pl.pallas_call(
        body,
        grid=(n_seq, nch),
        in_specs=in_specs,
        out_specs=[pl.BlockSpec((C, GV), lambda s, c: (s * nch + c, 0)),
                   pl.BlockSpec((None, G_HEADS, dv, dk), lambda s, c: (s, 0, 0, 0))],
        out_shape=[jax.ShapeDtypeStruct((rows, GV), BF16),
                   jax.ShapeDtypeStruct(s0t.shape[1:], F32)],
        scratch_shapes=[pltpu.VMEM((G_HEADS, dv, dk), F32)],
        compiler_params=_params(("parallel", "arbitrary")),
        name="gla",
    )(proj_g, proj_g, proj_g, proj_g, proj_r, lw['gla_a2'], lw['gla_ab'], lw['gla_norm'], s0t)


def _hx_to_xh(w, n_heads):
    sh = w.shape[:-1]
    x = w.shape[-1] // n_heads
    return jnp.swapaxes(w.reshape(*sh, n_heads, x), -1, -2).reshape(*sh, n_heads * x)


def _xh_to_hx(w, n_heads):
    sh = w.shape[:-1]
    x = w.shape[-1] // n_heads
    return jnp.swapaxes(w.reshape(*sh, x, n_heads), -1, -2).reshape(*sh, n_heads * x)


def _pad_last(x, width):
    return jnp.pad(x, [(0, 0)] * (x.ndim - 1) + [(0, width - x.shape[-1])])


def _split_r_cols(x, RW, ld, la, lg):
    offs = [0, RW, 2 * RW, 3 * RW, 3 * RW + ld, 3 * RW + ld + la, 3 * RW + ld + la + lg]
    return [x[..., offs[i]:offs[i + 1]] for i in range(6)]


def _small_block(wd, ad, gd, lgp, tail=None):
    parts = [wd, ad, _pad_last(gd, lgp)]
    if tail is not None:
        parts.append(tail)
    return _pad_last(jnp.concatenate(parts, axis=-1), SMALL_W)


def _r_cols_to_kernel(x, RW, ld, la, lg, lgp, n_heads, tail=None):
    x_r, x_k, x_v, x_wd, x_ad, x_gd = _split_r_cols(x, RW, ld, la, lg)
    return jnp.concatenate([_hx_to_xh(x_r, n_heads), _hx_to_xh(x_k, n_heads), _hx_to_xh(x_v, n_heads),
                            _small_block(x_wd, x_ad, x_gd, lgp, tail)], axis=-1)


def _r_cols_from_kernel(x, RW, ld, la, lg, n_heads):
    sm = x[..., 3 * RW:]
    return jnp.concatenate(
        [_xh_to_hx(x[..., 0:RW], n_heads), _xh_to_hx(x[..., RW:2 * RW], n_heads),
         _xh_to_hx(x[..., 2 * RW:3 * RW], n_heads),
         sm[..., 0:ld], sm[..., ld:ld + la], sm[..., ld + la:ld + la + lg]], axis=-1)


def _state_to_kernel(s, n_heads):
    lead = s.shape[:-3]
    n = len(lead)
    vv = LANES // n_heads
    g = R_HEAD_DIM // (SUBLANES * vv)
    x = s.reshape(*lead, n_heads, g, SUBLANES, vv, R_HEAD_DIM)
    x = x.transpose(*range(n), n + 1, n + 4, n + 2, n + 3, n)
    return x.reshape(*lead, g, R_HEAD_DIM, SUBLANES, LANES)


def _state_from_kernel(s, n_heads):
    lead = s.shape[:-4]
    n = len(lead)
    g = s.shape[-4]
    vv = LANES // n_heads
    x = s.reshape(*lead, g, R_HEAD_DIM, SUBLANES, vv, n_heads)
    x = x.transpose(*range(n), n + 4, n, n + 2, n + 3, n + 1)
    return x.reshape(*lead, n_heads, R_HEAD_DIM, R_HEAD_DIM)


def kernel(x_prompt, x_sample, state_shift, state_rwkv, state_gla, p_prompt, p_sample, norm_mix_pre, norm_mix_post, norm_ffn_pre, norm_ffn_post, norm_ple, w_in, rwkv_mu, rwkv_w0, rwkv_w2, rwkv_a0, rwkv_a2, rwkv_g2, rwkv_k_k, rwkv_k_a, rwkv_r_k, rwkv_ln_w, rwkv_ln_b, gla_a2, gla_ab, gla_norm, w_proj_rwkv, w_proj_gla, w_out, ffn_gate, ffn_up, ffn_down, ple_proj, ple_gate):
    Bp, Tp, D = x_prompt.shape
    Bs, Ts, _ = x_sample.shape
    L = w_in.shape[0]
    RW = rwkv_w0.shape[-1]
    H = RW // R_HEAD_DIM
    assert LANES % H == 0 and R_HEAD_DIM % (SUBLANES * (LANES // H)) == 0
    ld, la, lg = rwkv_w2.shape[1], rwkv_a2.shape[1], rwkv_g2.shape[1]
    lgp = _round_up(lg, LANES)
    GK, GV, GR = gla_a2.shape[-1], gla_norm.shape[-1], gla_a2.shape[1]
    FF = ffn_gate.shape[-1]
    Mp, Ms = Bp * Tp, Bs * Ts
    groups = [(0, Bp, Tp), (Mp, Bs, Ts)]

    NR = 3 * RW + SMALL_W
    NG = 2 * GK + 2 * GV
    o_ag = 3 * RW + ld + la + lgp
    assert o_ag + LANES <= NR and (3 * RW) % SMALL_W == 0 and GR <= LANES
    r_cols = 3 * RW + ld + la + lg
    g_cols = NG + GR
    to_kernel = functools.partial(_r_cols_to_kernel, RW=RW, ld=ld, la=la, lg=lg, lgp=lgp, n_heads=H)
    win_r = to_kernel(w_in[..., :r_cols], tail=w_in[..., r_cols + NG:r_cols + g_cols]).astype(BF16)
    win_g = w_in[..., r_cols:r_cols + NG].astype(BF16)
    win_gate = w_in[..., r_cols + g_cols:].astype(BF16)
    assert win_r.shape[-1] == NR and win_gate.shape[-1] == 2 * D

    mu_k = to_kernel(rwkv_mu)
    kq = LANES // H
    src = jnp.arange(LANES)
    dst = jnp.arange(kq * LANES)
    rep_mat = jnp.tile((src[:, None] == ((dst // LANES) * H + dst % H)[None, :]).astype(BF16), (3, 1))
    lw = dict(
        vec=jnp.stack([mu_k[:, 0:RW], mu_k[:, RW:2 * RW], mu_k[:, 2 * RW:3 * RW],
                       _hx_to_xh(rwkv_w0, H), _hx_to_xh(rwkv_a0, H), _hx_to_xh(rwkv_k_k, H),
                       _hx_to_xh(rwkv_k_a, H), _hx_to_xh(rwkv_r_k.reshape(L, RW), H)], axis=1),
        mu_sm=mu_k[:, None, 3 * RW:],
        w2=_hx_to_xh(rwkv_w2, H).astype(BF16), a2=_hx_to_xh(rwkv_a2, H).astype(BF16),
        g2=jnp.pad(_hx_to_xh(rwkv_g2, H), ((0, 0), (0, lgp - lg), (0, 0))).astype(BF16),
        rep=rep_mat, ld=ld, la=la, lgp=lgp,
        ln=jnp.stack([_hx_to_xh(rwkv_ln_w, H), _hx_to_xh(rwkv_ln_b, H)], axis=1),
        gla_a2=jnp.pad(gla_a2, ((0, 0), (0, LANES - GR), (0, 0))), gla_ab=gla_ab[:, None], gla_norm=gla_norm[:, None])
    gains = jnp.stack([norm_mix_pre, norm_mix_post, norm_ffn_pre, norm_ffn_post, norm_ple], axis=1)[:, :, None]
    w_pr = jnp.swapaxes(w_proj_rwkv.reshape(L, H, R_HEAD_DIM, D), 1, 2).reshape(L, RW, D).astype(BF16)
    w_pg, w_o, w_fg, w_fu, w_plg, w_plp = w_proj_gla, w_out, ffn_gate, ffn_up, ple_gate, ple_proj
    w_fd = cast_bf16(ffn_down)

    shift_states = [jnp.zeros((1, Bp, 1, NR), F32), to_kernel(state_shift)[:, :, None]]
    n_vgrp = R_HEAD_DIM // (SUBLANES * (LANES // H))
    rwkv_states = [jnp.zeros((1, Bp, n_vgrp, R_HEAD_DIM, SUBLANES, LANES), F32),
                   _state_to_kernel(state_rwkv, H)]
    gla_states = [jnp.zeros((1, Bp, G_HEADS, GV // G_HEADS, GK // G_HEADS), F32), jnp.swapaxes(state_gla, -1, -2)]

    h = jnp.concatenate([x_prompt.reshape(Mp, D), x_sample.reshape(Ms, D)], axis=0)
    p_all = jnp.concatenate([p_prompt.reshape(L, Mp, -1), p_sample.reshape(L, Ms, -1)], axis=1).astype(BF16)

    new_shift = [[], []]
    new_rwkv = [[], []]
    new_gla = [[], []]
    tn = _pick_tile(D, COL_TILE // 2, LANES)
    gate_tiling = None
    if Tp % SEQ_CHUNK == 0 and SEQ_CHUNK % MM_PIECES == 0:
        gate_tiling = _mm_tiling(Bp * Tp // SEQ_CHUNK, Mp + Ms, 2 * D)
    gate_under_rec = gate_tiling is not None
    if gate_under_rec:
        n_gcol = gate_tiling[1]
        win_gate_t = jnp.swapaxes(win_gate.reshape(L, D, n_gcol, 2 * D // n_gcol), 1, 2)
    for i in range(L):
        z, z_split = norm_cast(h, gains, i, G_MIX_PRE)
        proj_r = fused_mm([z], [(0, win_r)], i, [], _ep_identity, NR, F32, "in_proj_r")
        proj_g = fused_mm([z], [(0, win_g)], i, [], _ep_identity, NG, F32, "in_proj_g")
        proj_gate = None if gate_under_rec else fused_mm(
            [z], [(0, win_gate)], i, [], _ep_identity, 2 * D, F32, "in_proj_gate")

        o_r_parts, o_g_parts = [], []
        for gi, (row_off, n_seq, T) in enumerate(groups):
            s_layer = i if gi == 1 else 0
            nkk, wr, kka, kmod, pend, c, vmix, g = rwkv_prep(
                proj_r, row_off, n_seq, T, shift_states[gi], s_layer, lw, i)
            if gi == 0 and gate_under_rec:
                y, s_new, proj_gate = rwkv_rec(nkk, wr, kka, kmod, pend, vmix, c, rwkv_states[gi], s_layer,
                                               n_seq, T, mm=(z_split, win_gate_t, i))
            else:
                y, s_new = rwkv_rec(nkk, wr, kka, kmod, pend, vmix, c, rwkv_states[gi], s_layer, n_seq, T)
            o_r_parts.append(rwkv_post(y.reshape(n_seq * T, RW), vmix, g, c, lw['ln'], i, H))
            o_g, s_gla_new = gla(proj_g, proj_r, row_off, n_seq, T, gla_states[gi], s_layer, lw, i, o_ag)
            o_g_parts.append(o_g)
            new_shift[gi].append(proj_r[row_off + T - 1:row_off + n_seq * T:T])
            new_rwkv[gi].append(s_new)
            new_gla[gi].append(s_gla_new)
        o_r = jnp.concatenate(o_r_parts, axis=0)
        o_g = jnp.concatenate(o_g_parts, axis=0)

        merged = fused_mm([o_r, o_g], [(0, w_pr), (1, w_pg)], i, [(proj_gate, 0), (proj_gate, D // tn)],
                          _ep_merge, D, BF16, "merge", tn=tn)
        mix = fused_mm([merged], [(0, w_o)], i, [], _ep_identity, D, F32, "out_proj")
        h1, u = resid_norm(h, mix, gains, i, G_MIX_POST, G_FFN_PRE)
        t = fused_mm([u], [(0, w_fg), (0, w_fu)], i, [], _ep_swiglu, FF, BF16, "ffn_up", tn=COL_TILE // 2)
        f = fused_mm([t], [(0, w_fd)], i, [], _ep_identity, D, F32, "ffn_down",
                     tm_target=ROW_TILE // 2, tn=COL_TILE // 2)
        h2, pn = resid_norm(h1, f, gains, i, G_FFN_POST, G_PLE)
        h = fused_mm([pn, p_all[i]], [(0, w_plg), (1, w_plp)], i, [(h2, 0)], _ep_ple, D, F32, "ple_mm", tn=tn)

    from_kernel = functools.partial(_r_cols_from_kernel, RW=RW, ld=ld, la=la, lg=lg, n_heads=H)
    y_prompt = h[:Mp].reshape(Bp, Tp, D)
    y_sample = h[Mp:].reshape(Bs, Ts, D)
    return (y_prompt, y_sample,
            from_kernel(jnp.stack(new_shift[0])), _state_from_kernel(jnp.stack(new_rwkv[0]), H),
            jnp.swapaxes(jnp.stack(new_gla[0]), -1, -2),
            from_kernel(jnp.stack(new_shift[1])), _state_from_kernel(jnp.stack(new_rwkv[1]), H),
            jnp.swapaxes(jnp.stack(new_gla[1]), -1, -2))
```

```python
import functools

import jax
import jax.numpy as jnp
from jax import lax
from jax.experimental import pallas as pl
from jax.experimental.pallas import tpu as pltpu

F32 = jnp.float32
BF16 = jnp.bfloat16
HIGHEST = lax.Precision.HIGHEST

NORM_EPS = 1e-6
R_GN_EPS = 64e-5
R_HEAD_DIM = 64
G_HEADS = 8
G_GATE_TAU = 16.0

LANES = 128
SUBLANES = 8
VMEM_LIMIT = 56 * 1024 * 1024

ROW_TILE = 1536
COL_TILE = 512
EW_ROWS = 256
CAST_TILE_BYTES = 8 * 1024 * 1024
SEQ_CHUNK = 64
GLA_SUB = 16
SMALL_W = 1024

G_MIX_PRE, G_MIX_POST, G_FFN_PRE, G_FFN_POST, G_PLE = range(5)
V_MU_R, V_MU_K, V_MU_V, V_W0, V_A0, V_KK, V_KA, V_RK = range(8)


def _round_up(x, m):
    return (x + m - 1) // m * m


def _pick_tile(n, target, mult=SUBLANES):
    best = None
    for t in range(mult, min(n, target) + 1, mult):
        if n % t == 0:
            best = t
    assert best is not None, (n, target)
    return best


def _params(sem):
    return pltpu.CompilerParams(dimension_semantics=sem, vmem_limit_bytes=VMEM_LIMIT)


def _rms(x, gain):
    ms = jnp.mean(x * x, axis=-1, keepdims=True)
    return x * lax.rsqrt(ms + NORM_EPS) * gain


def _softplus(x):
    return jnp.maximum(x, 0.0) + jnp.log(1.0 + jnp.exp(-jnp.abs(x)))


def _gain_spec(D, layer, row):
    return pl.BlockSpec((None, None, 1, D), lambda *_: (layer, row, 0, 0))


def _norm_cast_body(x_ref, g_ref, o_ref):
    o_ref[...] = _rms(x_ref[...], g_ref[...]).astype(o_ref.dtype)


def norm_cast(x, gains, layer, row):
    M, D = x.shape
    tm = _pick_tile(M, EW_ROWS)
    return pl.pallas_call(
        _norm_cast_body,
        grid=(M // tm,),
        in_specs=[pl.BlockSpec((tm, D), lambda i: (i, 0)), _gain_spec(D, layer, row)],
        out_specs=pl.BlockSpec((tm, D), lambda i: (i, 0)),
        out_shape=jax.ShapeDtypeStruct((M, D), BF16),
        compiler_params=_params(("parallel",)),
        name="norm_cast",
    )(x, gains)


def _resid_norm_body(h_ref, x_ref, gpost_ref, gpre_ref, h1_ref, u_ref):
    h1 = h_ref[...] + _rms(x_ref[...], gpost_ref[...])
    h1_ref[...] = h1
    u_ref[...] = _rms(h1, gpre_ref[...]).astype(u_ref.dtype)


def resid_norm(h, x, gains, layer, row_post, row_pre):
    M, D = h.shape
    tm = _pick_tile(M, EW_ROWS)
    row = pl.BlockSpec((tm, D), lambda i: (i, 0))
    return pl.pallas_call(
        _resid_norm_body,
        grid=(M // tm,),
        in_specs=[row, row, _gain_spec(D, layer, row_post), _gain_spec(D, layer, row_pre)],
        out_specs=[row, row],
        out_shape=[jax.ShapeDtypeStruct((M, D), F32), jax.ShapeDtypeStruct((M, D), BF16)],
        compiler_params=_params(("parallel",)),
        name="resid_norm",
    )(h, x, gains, gains)


def _cast_body(x_ref, o_ref):
    o_ref[...] = x_ref[...].astype(o_ref.dtype)


def cast_bf16(w):
    L, K, N = w.shape
    rows = L * K
    tr = _pick_tile(rows, max(SUBLANES, CAST_TILE_BYTES // (4 * N) // SUBLANES * SUBLANES))
    out = pl.pallas_call(
        _cast_body,
        grid=(rows // tr,),
        in_specs=[pl.BlockSpec((tr, N), lambda i: (i, 0))],
        out_specs=pl.BlockSpec((tr, N), lambda i: (i, 0)),
        out_shape=jax.ShapeDtypeStruct((rows, N), BF16),
        compiler_params=_params(("parallel",)),
        name="cast_bf16",
    )(w.reshape(rows, N))
    return out.reshape(L, K, N)


def _mm_body(*refs, act_of, n_acts, n_extra, epilogue):
    n_w = len(act_of)
    a_refs = refs[:n_acts]
    w_refs = refs[n_acts:n_acts + n_w]
    e_refs = refs[n_acts + n_w:n_acts + n_w + n_extra]
    o_ref = refs[n_acts + n_w + n_extra]
    acts = [a[...] for a in a_refs]
    dots = [jnp.dot(acts[ai], w[...].astype(BF16), preferred_element_type=F32) for ai, w in zip(act_of, w_refs)]
    o_ref[...] = epilogue(dots, [e[...] for e in e_refs]).astype(o_ref.dtype)


def fused_mm(acts, weights, layer, extras, epilogue, n_out, out_dtype, name, tm_target=None, tn=None):
    M = acts[0].shape[0]
    tm = _pick_tile(M, ROW_TILE if tm_target is None else min(tm_target, ROW_TILE))
    tn = _pick_tile(n_out, COL_TILE, LANES) if tn is None else tn
    grid = (M // tm, pl.cdiv(n_out, tn))
    in_specs = ([pl.BlockSpec((tm, a.shape[1]), lambda i, j: (i, 0)) for a in acts]
                + [pl.BlockSpec((None, w.shape[1], tn), lambda i, j: (layer, 0, j)) for _, w in weights]
                + [pl.BlockSpec((tm, tn), functools.partial(lambda i, j, off: (i, j + off), off=off))
                   for _, off in extras])
    body = functools.partial(_mm_body, act_of=tuple(a for a, _ in weights), n_acts=len(acts),
                             n_extra=len(extras), epilogue=epilogue)
    return pl.pallas_call(
        body,
        grid=grid,
        in_specs=in_specs,
        out_specs=pl.BlockSpec((tm, tn), lambda i, j: (i, j)),
        out_shape=jax.ShapeDtypeStruct((M, n_out), out_dtype),
        compiler_params=_params(("parallel", "parallel")),
        name=name,
    )(*acts, *[w for _, w in weights], *[e for e, _ in extras])


def _ep_identity(dots, extras):
    return dots[0]


def _ep_merge(dots, extras):
    return jax.nn.sigmoid(extras[0]) * dots[0] + jax.nn.sigmoid(extras[1]) * dots[1]


def _ep_swiglu(dots, extras):
    return jax.nn.silu(dots[0]) * dots[1]


def _ep_ple(dots, extras):
    return extras[0] + jax.nn.sigmoid(dots[0]) * dots[1]


def _shift_mix(x, prev_first, mu):
    rolled = pltpu.roll(x, 1, axis=0)
    row = lax.broadcasted_iota(jnp.int32, x.shape, 0)
    prev = jnp.where(row == 0, prev_first, rolled)
    return x + (prev - x) * mu


def _head_sum(x, n_heads):
    s = x[:, 0:LANES]
    for b in range(1, x.shape[1] // LANES):
        s = s + x[:, b * LANES:(b + 1) * LANES]
    shift = LANES // 2
    while shift >= n_heads:
        s = s + pltpu.roll(s, shift, axis=1)
        shift //= 2
    return s


def _tile_cols(x, n):
    return jnp.concatenate([x] * n, axis=1)


def _replicate(x, rep, out_ref):
    rows = x.shape[0]
    n_blk = x.shape[1] // LANES
    grp = rep.shape[1]
    xs = jnp.concatenate([x[:, b * LANES:(b + 1) * LANES] for b in range(n_blk)], axis=0)
    hi = xs.astype(BF16)
    rest = xs - hi.astype(F32)
    mid = rest.astype(BF16)
    lo = (rest - mid.astype(F32)).astype(BF16)
    y = jnp.dot(jnp.concatenate([hi, mid, lo], axis=1), rep, preferred_element_type=F32)
    for b in range(n_blk):
        out_ref[:, b * grp:(b + 1) * grp] = y[b * rows:(b + 1) * rows, :]


def _rwkv_prep_body(r_ref, k_ref, v_ref, sm_ref, rp_ref, kp_ref, vp_ref, smp_ref,
                    rs_ref, ks_ref, vs_ref, sms_ref, vec_ref, musm_ref,
                    w2_ref, a2_ref, g2_ref, rep_ref,
                    nkk_ref, wr_ref, kka_ref, kmod_ref, pend_ref, c_ref, vmix_ref, g_ref,
                    *, ld, la, lgp, n_heads):
    j = pl.program_id(1)
    first = j == 0

    def prev_row(p_ref, s_ref):
        return jnp.where(first, s_ref[...], p_ref[SUBLANES - 1:SUBLANES, :])

    def vec(row):
        return vec_ref[row:row + 1, :]

    r = _shift_mix(r_ref[...], prev_row(rp_ref, rs_ref), vec(V_MU_R))
    k = _shift_mix(k_ref[...], prev_row(kp_ref, ks_ref), vec(V_MU_K))
    vmix_ref[...] = _shift_mix(v_ref[...], prev_row(vp_ref, vs_ref), vec(V_MU_V))
    sm = _shift_mix(sm_ref[...], prev_row(smp_ref, sms_ref), musm_ref[...])

    wd = jnp.tanh(sm[:, 0:ld]).astype(BF16)
    ad = sm[:, ld:ld + la].astype(BF16)
    gd = jax.nn.sigmoid(sm[:, ld + la:ld + la + lgp]).astype(BF16)
    g_ref[...] = jnp.dot(gd, g2_ref[...], preferred_element_type=F32)

    tc = r.shape[0]
    n_blk = r.shape[1] // LANES
    w = -_softplus(-(vec(V_W0) + jnp.dot(wd, w2_ref[...], preferred_element_type=F32))) - 0.5
    neg_log_decay = jnp.exp(w)
    a = jax.nn.sigmoid(vec(V_A0) + jnp.dot(ad, a2_ref[...], preferred_element_type=F32))
    kk = k * vec(V_KK)
    inv_norm = 1.0 / jnp.maximum(jnp.sqrt(_head_sum(kk * kk, n_heads)), 1e-12)
    kkn = kk * _tile_cols(inv_norm, n_blk)
    kmod = k * (1.0 + (a - 1.0) * vec(V_KA))
    kka = kkn * a
    rk = r * kmod
    c_ref[:, 0:LANES] = _head_sum(kka * r, n_heads)
    c_ref[:, LANES:2 * LANES] = _head_sum(rk, n_heads)
    c_ref[:, 2 * LANES:3 * LANES] = _head_sum(rk * vec(V_RK), n_heads)

    ti = lax.broadcasted_iota(jnp.int32, (tc, tc), 0)
    si = lax.broadcasted_iota(jnp.int32, (tc, tc), 1)
    tri = (si <= ti).astype(F32)
    log_p = -jnp.dot(tri, neg_log_decay, precision=HIGHEST, preferred_element_type=F32)
    p_incl = jnp.exp(log_p)
    p_prev = jnp.exp(log_p + neg_log_decay)
    inv_p = jnp.exp(-log_p)

    rep = rep_ref[...]
    _replicate(-kkn * p_prev, rep, nkk_ref)
    _replicate(r * p_incl, rep, wr_ref)
    _replicate(kka * inv_p, rep, kka_ref)
    _replicate(kmod * inv_p, rep, kmod_ref)
    _replicate(p_incl[tc - SUBLANES:tc, :], rep, pend_ref)


def rwkv_prep(proj_r, row_off, n_seq, T, shift, shift_layer, lw, layer):
    RW = lw['vec'].shape[-1]
    rep_w = R_HEAD_DIM * LANES
    tc = min(SEQ_CHUNK, T)
    assert T % tc == 0 and row_off % tc == 0 and tc % SUBLANES == 0
    nch = T // tc
    rows = n_seq * T
    off_b = row_off // tc
    sm_col = 3 * RW // SMALL_W
    ld, la, lgp = lw['ld'], lw['la'], lw['lgp']

    def tile(width, colblk):
        return pl.BlockSpec((tc, width), lambda s, j: (off_b + s * nch + j, colblk))

    def prev(width, colblk):
        per = tc // SUBLANES
        return pl.BlockSpec((SUBLANES, width),
                            lambda s, j: (jnp.maximum((off_b + s * nch + j) * per - 1, 0), colblk))

    def state(width, colblk):
        return pl.BlockSpec((None, None, 1, width), lambda s, j: (shift_layer, s, 0, colblk))

    def stacked(arr):
        return pl.BlockSpec((None,) + arr.shape[1:], lambda s, j: (layer,) + (0,) * (arr.ndim - 1))

    def out(width):
        return pl.BlockSpec((tc, width), lambda s, j: (s * nch + j, 0))

    in_specs = [tile(RW, 0), tile(RW, 1), tile(RW, 2), tile(SMALL_W, sm_col),
                prev(RW, 0), prev(RW, 1), prev(RW, 2), prev(SMALL_W, sm_col),
                state(RW, 0), state(RW, 1), state(RW, 2), state(SMALL_W, sm_col),
                stacked(lw['vec']), stacked(lw['mu_sm']),
                stacked(lw['w2']), stacked(lw['a2']), stacked(lw['g2']),
                pl.BlockSpec(lw['rep'].shape, lambda s, j: (0, 0))]
    out_specs = ([out(rep_w)] * 4
                 + [pl.BlockSpec((SUBLANES, rep_w), lambda s, j: (s * nch + j, 0)),
                    out(3 * LANES), out(RW), out(RW)])
    out_shape = ([jax.ShapeDtypeStruct((rows, rep_w), F32)] * 4
                 + [jax.ShapeDtypeStruct((n_seq * nch * SUBLANES, rep_w), F32),
                    jax.ShapeDtypeStruct((rows, 3 * LANES), F32),
                    jax.ShapeDtypeStruct((rows, RW), F32), jax.ShapeDtypeStruct((rows, RW), F32)])
    body = functools.partial(_rwkv_prep_body, ld=ld, la=la, lgp=lgp, n_heads=RW // R_HEAD_DIM)
    return pl.pallas_call(
        body,
        grid=(n_seq, nch),
        in_specs=in_specs,
        out_specs=out_specs,
        out_shape=out_shape,
        compiler_params=_params(("parallel", "arbitrary")),
        name="rwkv_prep",
    )(proj_r, proj_r, proj_r, proj_r, proj_r, proj_r, proj_r, proj_r,
      shift, shift, shift, shift, lw['vec'], lw['mu_sm'], lw['w2'], lw['a2'], lw['g2'], lw['rep'])


def _rwkv_rec_body(nkk_ref, wr_ref, kka_ref, kmod_ref, pend_ref, v_ref, c_ref, s0_ref,
                   y_ref, sout_ref, s_ref, *, tc, n_grp, nch):
    j = pl.program_id(1)

    @pl.when(j == 0)
    def _():
        s_ref[...] = s0_ref[...]

    def row(ref, t, k):
        return jnp.broadcast_to(ref[t, pl.ds(k, 1), :], (SUBLANES, LANES))

    def reduce(t, tiles):
        acc = [[None, None] for _ in range(2 * n_grp)]
        for k in range(R_HEAD_DIM):
            nkk = row(nkk_ref, t, k)
            wr = row(wr_ref, t, k)
            for g, sg in enumerate(tiles(k)):
                for slot, term in ((g, sg * nkk), (n_grp + g, sg * wr)):
                    prev = acc[slot][k % 2]
                    acc[slot][k % 2] = term if prev is None else prev + term
        return tuple(a[0] + a[1] for a in acc)

    def step(t, carry):
        sa, yp = carry[:n_grp], carry[n_grp:]
        crow = c_ref[pl.ds(t, 1), :]
        c1 = crow[:, 0:LANES]
        c2 = crow[:, LANES:2 * LANES]
        vg = [v_ref[t, g * SUBLANES:(g + 1) * SUBLANES, :] for g in range(n_grp)]
        for g in range(n_grp):
            y_ref[t, g * SUBLANES:(g + 1) * SUBLANES, :] = yp[g] + sa[g] * c1 + vg[g] * c2

        def updated(k):
            kka = row(kka_ref, t, k)
            kmod = row(kmod_ref, t, k)
            new = [s_ref[g, k] + sa[g] * kka + vg[g] * kmod for g in range(n_grp)]
            for g in range(n_grp):
                s_ref[g, k] = new[g]
            return new

        return reduce(jnp.minimum(t + 1, tc - 1), updated)

    carry0 = reduce(0, lambda k: [s_ref[g, k] for g in range(n_grp)])
    lax.fori_loop(0, tc, step, carry0)

    for k in range(R_HEAD_DIM):
        p_end = row(pend_ref, SUBLANES - 1, k)
        for g in range(n_grp):
            s_ref[g, k] = s_ref[g, k] * p_end

    @pl.when(j == nch - 1)
    def _():
        sout_ref[...] = s_ref[...]


def rwkv_rec(nkk, wr, kka, kmod, pend, vmix, c, s0, s0_layer, n_seq, T):
    rows = n_seq * T
    n_grp = s0.shape[2]
    v_rows = n_grp * SUBLANES
    tc = min(SEQ_CHUNK, T)
    nch = T // tc
    k3 = lambda x: x.reshape(x.shape[0], R_HEAD_DIM, LANES)
    kspec = pl.BlockSpec((tc, R_HEAD_DIM, LANES), lambda s, j: (s * nch + j, 0, 0))
    pspec = pl.BlockSpec((SUBLANES, R_HEAD_DIM, LANES), lambda s, j: (s * nch + j, 0, 0))
    vspec = pl.BlockSpec((tc, v_rows, LANES), lambda s, j: (s * nch + j, 0, 0))
    tile = (n_grp, R_HEAD_DIM, SUBLANES, LANES)
    s_in = pl.BlockSpec((None, None) + tile, lambda s, j: (s0_layer, s, 0, 0, 0, 0))
    s_out = pl.BlockSpec((None,) + tile, lambda s, j: (s, 0, 0, 0, 0))
    body = functools.partial(_rwkv_rec_body, tc=tc, n_grp=n_grp, nch=nch)
    return pl.pallas_call(
        body,
        grid=(n_seq, nch),
        in_specs=[kspec] * 4 + [pspec, vspec, pl.BlockSpec((tc, 3 * LANES), lambda s, j: (s * nch + j, 0)), s_in],
        out_specs=[vspec, s_out],
        out_shape=[jax.ShapeDtypeStruct((rows, v_rows, LANES), F32),
                   jax.ShapeDtypeStruct(s0.shape[1:], F32)],
        scratch_shapes=[pltpu.VMEM(tile, F32)],
        compiler_params=_params(("parallel", "arbitrary")),
        name="rwkv_rec",
    )(k3(nkk), k3(wr), k3(kka), k3(kmod), k3(pend), vmix.reshape(rows, v_rows, LANES), c, s0)


def _rwkv_post_body(y_ref, v_ref, g_ref, c_ref, ln_ref, o_ref, *, n_grp, n_heads):
    def head_sum(x):
        return _tile_cols(_head_sum(x, n_heads), n_grp)

    y = y_ref[...]
    inv_n = 1.0 / R_HEAD_DIM
    mu = head_sum(y) * inv_n
    d = y - mu
    var = head_sum(d * d) * inv_n
    yln = d * lax.rsqrt(var + R_GN_EPS) * ln_ref[0:1, :] + ln_ref[1:2, :]
    c3 = _tile_cols(c_ref[:, 2 * LANES:3 * LANES], n_grp)
    o_ref[...] = ((yln + c3 * v_ref[...]) * g_ref[...]).astype(o_ref.dtype)


def rwkv_post(y, vmix, g, c, ln, layer, n_heads):
    M, RW = y.shape
    tm = _pick_tile(M, EW_ROWS)
    row = pl.BlockSpec((tm, RW), lambda i: (i, 0))
    body = functools.partial(_rwkv_post_body, n_grp=RW // LANES, n_heads=n_heads)
    return pl.pallas_call(
        body,
        grid=(M // tm,),
        in_specs=[row, row, row, pl.BlockSpec((tm, 3 * LANES), lambda i: (i, 0)),
                  pl.BlockSpec((None, 2, RW), lambda i: (layer, 0, 0))],
        out_specs=row,
        out_shape=jax.ShapeDtypeStruct((M, RW), BF16),
        compiler_params=_params(("parallel",)),
        name="rwkv_post",
    )(y, vmix, g, c, ln)


def _gla_body(q_ref, k_ref, v_ref, rg_ref, ag_ref, a2_ref, ab_ref, gn_ref, s0_ref,
              o_ref, sout_ref, st_ref, *, C, nch, n_heads, dk, dv):
    c = pl.program_id(1)

    @pl.when(c == 0)
    def _():
        st_ref[...] = s0_ref[...]

    x = jnp.dot(ag_ref[...], a2_ref[...], precision=HIGHEST, preferred_element_type=F32) + ab_ref[...]
    log_a = -_softplus(-x) * (1.0 / G_GATE_TAU)
    ti = lax.broadcasted_iota(jnp.int32, (C, C), 0)
    si = lax.broadcasted_iota(jnp.int32, (C, C), 1)
    tri = (si <= ti).astype(F32)
    b_all = jnp.dot(tri, log_a, precision=HIGHEST, preferred_element_type=F32)

    nt = (((1,), (1,)), ((), ()))
    tn = (((0,), (0,)), ((), ()))
    scale = float(dk) ** -0.5
    rowi = lax.broadcasted_iota(jnp.int32, (GLA_SUB, dk), 0)
    lane = lax.broadcasted_iota(jnp.int32, (GLA_SUB, LANES), 1)

    for h in range(n_heads):
        ksl = slice(h * dk, (h + 1) * dk)
        vsl = slice(h * dv, (h + 1) * dv)
        b = b_all[:, ksl]
        b_last = b[C - 1:C, :]
        q = q_ref[:, ksl] * scale
        k = k_ref[:, ksl]
        v = v_ref[:, vsl]
        vh = v.astype(BF16)
        st = st_ref[h]

        inter = lax.dot_general((q * jnp.exp(b)).astype(BF16), st.astype(BF16), nt,
                                preferred_element_type=F32)
        blocks = []
        for blk in range(C // GLA_SUB):
            t0 = blk * GLA_SUB
            rows = slice(t0, t0 + GLA_SUB)
            qb, kb, bb = q[rows], k[rows], b[rows]
            acc = inter[rows]
            if blk > 0:
                b0 = bb[0:1, :]
                qd = (qb * jnp.exp(bb - b0)).astype(BF16)
                kd = (k[0:t0] * jnp.exp(b0 - b[0:t0])).astype(BF16)
                sc = lax.dot_general(qd, kd, nt, preferred_element_type=F32)
                acc = acc + jnp.dot(sc.astype(BF16), vh[0:t0], preferred_element_type=F32)
            sc_d = jnp.zeros((GLA_SUB, LANES), F32)
            for s in range(GLA_SUB):
                e = jnp.where(rowi >= s, jnp.exp(bb - bb[s:s + 1, :]), 0.0)
                sc_s = jnp.sum(qb * e * kb[s:s + 1, :], axis=-1, keepdims=True)
                sc_d = jnp.where(lane == s, sc_s, sc_d)
            acc = acc + jnp.dot(sc_d[:, 0:GLA_SUB].astype(BF16), vh[rows], preferred_element_type=F32)
            blocks.append(acc)
        o = jnp.concatenate(blocks, axis=0) if len(blocks) > 1 else blocks[0]

        kdec = (k * jnp.exp(b_last - b)).astype(BF16)
        st_ref[h] = st * jnp.exp(b_last) + lax.dot_general(vh, kdec, tn, preferred_element_type=F32)

        ms = jnp.mean(o * o, axis=-1, keepdims=True)
        o_ref[:, vsl] = (o * lax.rsqrt(ms + NORM_EPS) * gn_ref[:, vsl]
                         * jax.nn.silu(rg_ref[:, vsl])).astype(o_ref.dtype)

    @pl.when(c == nch - 1)
    def _():
        sout_ref[...] = st_ref[...]


def gla(proj_g, proj_r, row_off, n_seq, T, s0t, s0_layer, lw, layer, o_ag):
    GK, GV = lw['gla_ab'].shape[-1], lw['gla_norm'].shape[-1]
    dk, dv = GK // G_HEADS, GV // G_HEADS
    C = min(SEQ_CHUNK, T)
    assert T % C == 0 and C % GLA_SUB == 0 and row_off % C == 0
    assert GV % GK == 0 and (2 * GK) % GV == 0 and o_ag % LANES == 0
    nch = T // C
    rows = n_seq * T
    off_b = row_off // C

    def tile(width, col0):
        assert col0 % width == 0
        return pl.BlockSpec((C, width), lambda s, c: (off_b + s * nch + c, col0 // width))

    def stacked(arr):
        return pl.BlockSpec((None,) + arr.shape[1:], lambda s, c: (layer,) + (0,) * (arr.ndim - 1))

    in_specs = [tile(GK, 0), tile(GK, GK), tile(GV, 2 * GK), tile(GV, 2 * GK + GV),
                tile(LANES, o_ag),
                stacked(lw['gla_a2']), stacked(lw['gla_ab']), stacked(lw['gla_norm']),
                pl.BlockSpec((None, None, G_HEADS, dv, dk), lambda s, c: (s0_layer, s, 0, 0, 0))]
    body = functools.partial(_gla_body, C=C, nch=nch, n_heads=G_HEADS, dk=dk, dv=dv)
    return pl.pallas_call(
        body,
        grid=(n_seq, nch),
        in_specs=in_specs,
        out_specs=[pl.BlockSpec((C, GV), lambda s, c: (s * nch + c, 0)),
                   pl.BlockSpec((None, G_HEADS, dv, dk), lambda s, c: (s, 0, 0, 0))],
        out_shape=[jax.ShapeDtypeStruct((rows, GV), BF16),
                   jax.ShapeDtypeStruct(s0t.shape[1:], F32)],
        scratch_shapes=[pltpu.VMEM((G_HEADS, dv, dk), F32)],
        compiler_params=_params(("parallel", "arbitrary")),
        name="gla",
    )(proj_g, proj_g, proj_g, proj_g, proj_r, lw['gla_a2'], lw['gla_ab'], lw['gla_norm'], s0t)


def _hx_to_xh(w, n_heads):
    sh = w.shape[:-1]
    x = w.shape[-1] // n_heads
    return jnp.swapaxes(w.reshape(*sh, n_heads, x), -1, -2).reshape(*sh, n_heads * x)


def _xh_to_hx(w, n_heads):
    sh = w.shape[:-1]
    x = w.shape[-1] // n_heads
    return jnp.swapaxes(w.reshape(*sh, x, n_heads), -1, -2).reshape(*sh, n_heads * x)


def _pad_last(x, width):
    return jnp.pad(x, [(0, 0)] * (x.ndim - 1) + [(0, width - x.shape[-1])])


def _split_r_cols(x, RW, ld, la, lg):
    offs = [0, RW, 2 * RW, 3 * RW, 3 * RW + ld, 3 * RW + ld + la, 3 * RW + ld + la + lg]
    return [x[..., offs[i]:offs[i + 1]] for i in range(6)]


def _small_block(wd, ad, gd, lgp, tail=None):
    parts = [wd, ad, _pad_last(gd, lgp)]
    if tail is not None:
        parts.append(tail)
    return _pad_last(jnp.concatenate(parts, axis=-1), SMALL_W)


def _r_cols_to_kernel(x, RW, ld, la, lg, lgp, n_heads, tail=None):
    x_r, x_k, x_v, x_wd, x_ad, x_gd = _split_r_cols(x, RW, ld, la, lg)
    return jnp.concatenate([_hx_to_xh(x_r, n_heads), _hx_to_xh(x_k, n_heads), _hx_to_xh(x_v, n_heads),
                            _small_block(x_wd, x_ad, x_gd, lgp, tail)], axis=-1)


def _r_cols_from_kernel(x, RW, ld, la, lg, n_heads):
    sm = x[..., 3 * RW:]
    return jnp.concatenate(
        [_xh_to_hx(x[..., 0:RW], n_heads), _xh_to_hx(x[..., RW:2 * RW], n_heads),
         _xh_to_hx(x[..., 2 * RW:3 * RW], n_heads),
         sm[..., 0:ld], sm[..., ld:ld + la], sm[..., ld + la:ld + la + lg]], axis=-1)


def _state_to_kernel(s, n_heads):
    lead = s.shape[:-3]
    n = len(lead)
    vv = LANES // n_heads
    g = R_HEAD_DIM // (SUBLANES * vv)
    x = s.reshape(*lead, n_heads, g, SUBLANES, vv, R_HEAD_DIM)
    x = x.transpose(*range(n), n + 1, n + 4, n + 2, n + 3, n)
    return x.reshape(*lead, g, R_HEAD_DIM, SUBLANES, LANES)


def _state_from_kernel(s, n_heads):
    lead = s.shape[:-4]
    n = len(lead)
    g = s.shape[-4]
    vv = LANES // n_heads
    x = s.reshape(*lead, g, R_HEAD_DIM, SUBLANES, vv, n_heads)
    x = x.transpose(*range(n), n + 4, n, n + 2, n + 3, n + 1)
    return x.reshape(*lead, n_heads, R_HEAD_DIM, R_HEAD_DIM)


def kernel(x_prompt, x_sample, state_shift, state_rwkv, state_gla, p_prompt, p_sample, norm_mix_pre, norm_mix_post, norm_ffn_pre, norm_ffn_post, norm_ple, w_in, rwkv_mu, rwkv_w0, rwkv_w2, rwkv_a0, rwkv_a2, rwkv_g2, rwkv_k_k, rwkv_k_a, rwkv_r_k, rwkv_ln_w, rwkv_ln_b, gla_a2, gla_ab, gla_norm, w_proj_rwkv, w_proj_gla, w_out, ffn_gate, ffn_up, ffn_down, ple_proj, ple_gate):
    Bp, Tp, D = x_prompt.shape
    Bs, Ts, _ = x_sample.shape
    L = w_in.shape[0]
    RW = rwkv_w0.shape[-1]
    H = RW // R_HEAD_DIM
    assert LANES % H == 0 and R_HEAD_DIM % (SUBLANES * (LANES // H)) == 0
    ld, la, lg = rwkv_w2.shape[1], rwkv_a2.shape[1], rwkv_g2.shape[1]
    lgp = _round_up(lg, LANES)
    GK, GV, GR = gla_a2.shape[-1], gla_norm.shape[-1], gla_a2.shape[1]
    FF = ffn_gate.shape[-1]
    Mp, Ms = Bp * Tp, Bs * Ts
    groups = [(0, Bp, Tp), (Mp, Bs, Ts)]

    NR = 3 * RW + SMALL_W
    NG = 2 * GK + 2 * GV
    o_ag = 3 * RW + ld + la + lgp
    assert o_ag + LANES <= NR and (3 * RW) % SMALL_W == 0 and GR <= LANES
    r_cols = 3 * RW + ld + la + lg
    g_cols = NG + GR
    to_kernel = functools.partial(_r_cols_to_kernel, RW=RW, ld=ld, la=la, lg=lg, lgp=lgp, n_heads=H)
    win_r = to_kernel(w_in[..., :r_cols], tail=w_in[..., r_cols + NG:r_cols + g_cols]).astype(BF16)
    win_g = w_in[..., r_cols:r_cols + NG].astype(BF16)
    win_gate = w_in[..., r_cols + g_cols:].astype(BF16)
    assert win_r.shape[-1] == NR and win_gate.shape[-1] == 2 * D

    mu_k = to_kernel(rwkv_mu)
    kq = LANES // H
    src = jnp.arange(LANES)
    dst = jnp.arange(kq * LANES)
    rep_mat = jnp.tile((src[:, None] == ((dst // LANES) * H + dst % H)[None, :]).astype(BF16), (3, 1))
    lw = dict(
        vec=jnp.stack([mu_k[:, 0:RW], mu_k[:, RW:2 * RW], mu_k[:, 2 * RW:3 * RW],
                       _hx_to_xh(rwkv_w0, H), _hx_to_xh(rwkv_a0, H), _hx_to_xh(rwkv_k_k, H),
                       _hx_to_xh(rwkv_k_a, H), _hx_to_xh(rwkv_r_k.reshape(L, RW), H)], axis=1),
        mu_sm=mu_k[:, None, 3 * RW:],
        w2=_hx_to_xh(rwkv_w2, H).astype(BF16), a2=_hx_to_xh(rwkv_a2, H).astype(BF16),
        g2=jnp.pad(_hx_to_xh(rwkv_g2, H), ((0, 0), (0, lgp - lg), (0, 0))).astype(BF16),
        rep=rep_mat, ld=ld, la=la, lgp=lgp,
        ln=jnp.stack([_hx_to_xh(rwkv_ln_w, H), _hx_to_xh(rwkv_ln_b, H)], axis=1),
        gla_a2=jnp.pad(gla_a2, ((0, 0), (0, LANES - GR), (0, 0))), gla_ab=gla_ab[:, None], gla_norm=gla_norm[:, None])
    gains = jnp.stack([norm_mix_pre, norm_mix_post, norm_ffn_pre, norm_ffn_post, norm_ple], axis=1)[:, :, None]
    w_pr = jnp.swapaxes(w_proj_rwkv.reshape(L, H, R_HEAD_DIM, D), 1, 2).reshape(L, RW, D).astype(BF16)
    w_pg, w_o, w_fg, w_fu, w_plg, w_plp = w_proj_gla, w_out, ffn_gate, ffn_up, ple_gate, ple_proj
    w_fd = cast_bf16(ffn_down)

    shift_states = [jnp.zeros((1, Bp, 1, NR), F32), to_kernel(state_shift)[:, :, None]]
    n_vgrp = R_HEAD_DIM // (SUBLANES * (LANES // H))
    rwkv_states = [jnp.zeros((1, Bp, n_vgrp, R_HEAD_DIM, SUBLANES, LANES), F32),
                   _state_to_kernel(state_rwkv, H)]
    gla_states = [jnp.zeros((1, Bp, G_HEADS, GV // G_HEADS, GK // G_HEADS), F32), jnp.swapaxes(state_gla, -1, -2)]

    h = jnp.concatenate([x_prompt.reshape(Mp, D), x_sample.reshape(Ms, D)], axis=0)
    p_all = jnp.concatenate([p_prompt.reshape(L, Mp, -1), p_sample.reshape(L, Ms, -1)], axis=1).astype(BF16)

    new_shift = [[], []]
    new_rwkv = [[], []]
    new_gla = [[], []]
    tn = _pick_tile(D, COL_TILE, LANES)
    for i in range(L):
        z = norm_cast(h, gains, i, G_MIX_PRE)
        proj_r = fused_mm([z], [(0, win_r)], i, [], _ep_identity, NR, F32, "in_proj_r")
        proj_g = fused_mm([z], [(0, win_g)], i, [], _ep_identity, NG, F32, "in_proj_g")
        proj_gate = fused_mm([z], [(0, win_gate)], i, [], _ep_identity, 2 * D, F32, "in_proj_gate")

        o_r_parts, o_g_parts = [], []
        for gi, (row_off, n_seq, T) in enumerate(groups):
            s_layer = i if gi == 1 else 0
            nkk, wr, kka, kmod, pend, c, vmix, g = rwkv_prep(
                proj_r, row_off, n_seq, T, shift_states[gi], s_layer, lw, i)
            y, s_new = rwkv_rec(nkk, wr, kka, kmod, pend, vmix, c, rwkv_states[gi], s_layer, n_seq, T)
            o_r_parts.append(rwkv_post(y.reshape(n_seq * T, RW), vmix, g, c, lw['ln'], i, H))
            o_g, s_gla_new = gla(proj_g, proj_r, row_off, n_seq, T, gla_states[gi], s_layer, lw, i, o_ag)
            o_g_parts.append(o_g)
            new_shift[gi].append(proj_r[row_off + T - 1:row_off + n_seq * T:T])
            new_rwkv[gi].append(s_new)
            new_gla[gi].append(s_gla_new)
        o_r = jnp.concatenate(o_r_parts, axis=0)
        o_g = jnp.concatenate(o_g_parts, axis=0)

        merged = fused_mm([o_r, o_g], [(0, w_pr), (1, w_pg)], i, [(proj_gate, 0), (proj_gate, D // tn)],
                          _ep_merge, D, BF16, "merge", tm_target=ROW_TILE // 2)
        mix = fused_mm([merged], [(0, w_o)], i, [], _ep_identity, D, F32, "out_proj")
        h1, u = resid_norm(h, mix, gains, i, G_MIX_POST, G_FFN_PRE)
        t = fused_mm([u], [(0, w_fg), (0, w_fu)], i, [], _ep_swiglu, FF, BF16, "ffn_up", tn=COL_TILE // 2)
        f = fused_mm([t], [(0, w_fd)], i, [], _ep_identity, D, F32, "ffn_down",
                     tm_target=ROW_TILE // 2, tn=COL_TILE // 2)
        h2, pn = resid_norm(h1, f, gains, i, G_FFN_POST, G_PLE)
        h = fused_mm([pn, p_all[i]], [(0, w_plg), (1, w_plp)], i, [(h2, 0)], _ep_ple, D, F32, "ple_mm",
                     tn=COL_TILE // 2)

    from_kernel = functools.partial(_r_cols_from_kernel, RW=RW, ld=ld, la=la, lg=lg, n_heads=H)
    y_prompt = h[:Mp].reshape(Bp, Tp, D)
    y_sample = h[Mp:].reshape(Bs, Ts, D)
    return (y_prompt, y_sample,
            from_kernel(jnp.stack(new_shift[0])), _state_from_kernel(jnp.stack(new_rwkv[0]), H),
            jnp.swapaxes(jnp.stack(new_gla[0]), -1, -2),
            from_kernel(jnp.stack(new_shift[1])), _state_from_kernel(jnp.stack(new_rwkv[1]), H),
            jnp.swapaxes(jnp.stack(new_gla[1]), -1, -2))
```

```python
import functools

import jax
import jax.numpy as jnp
from jax import lax
from jax.experimental import pallas as pl
from jax.experimental.pallas import tpu as pltpu

F32 = jnp.float32
BF16 = jnp.bfloat16
HIGHEST = lax.Precision.HIGHEST

NORM_EPS = 1e-6
R_GN_EPS = 64e-5
R_HEAD_DIM = 64
G_HEADS = 8
G_GATE_TAU = 16.0

LANES = 128
SUBLANES = 8
VMEM_LIMIT = 56 * 1024 * 1024

ROW_TILE = 1536
COL_TILE = 512
EW_ROWS = 256
CAST_TILE_BYTES = 8 * 1024 * 1024
SEQ_CHUNK = 64
GLA_SUB = 16
SMALL_W = 1024

G_MIX_PRE, G_MIX_POST, G_FFN_PRE, G_FFN_POST, G_PLE = range(5)
V_MU_R, V_MU_K, V_MU_V, V_W0, V_A0, V_KK, V_KA, V_RK = range(8)


def _round_up(x, m):
    return (x + m - 1) // m * m


def _pick_tile(n, target, mult=SUBLANES):
    best = None
    for t in range(mult, min(n, target) + 1, mult):
        if n % t == 0:
            best = t
    assert best is not None, (n, target)
    return best


def _params(sem):
    return pltpu.CompilerParams(dimension_semantics=sem, vmem_limit_bytes=VMEM_LIMIT)


def _rms(x, gain):
    ms = jnp.mean(x * x, axis=-1, keepdims=True)
    return x * lax.rsqrt(ms + NORM_EPS) * gain


def _softplus(x):
    return jnp.maximum(x, 0.0) + jnp.log(1.0 + jnp.exp(-jnp.abs(x)))


def _gain_spec(D, layer, row):
    return pl.BlockSpec((None, None, 1, D), lambda *_: (layer, row, 0, 0))


def _norm_cast_body(x_ref, g_ref, o_ref):
    o_ref[...] = _rms(x_ref[...], g_ref[...]).astype(o_ref.dtype)


def norm_cast(x, gains, layer, row):
    M, D = x.shape
    tm = _pick_tile(M, EW_ROWS)
    return pl.pallas_call(
        _norm_cast_body,
        grid=(M // tm,),
        in_specs=[pl.BlockSpec((tm, D), lambda i: (i, 0)), _gain_spec(D, layer, row)],
        out_specs=pl.BlockSpec((tm, D), lambda i: (i, 0)),
        out_shape=jax.ShapeDtypeStruct((M, D), BF16),
        compiler_params=_params(("parallel",)),
        name="norm_cast",
    )(x, gains)


def _resid_norm_body(h_ref, x_ref, gpost_ref, gpre_ref, h1_ref, u_ref):
    h1 = h_ref[...] + _rms(x_ref[...], gpost_ref[...])
    h1_ref[...] = h1
    u_ref[...] = _rms(h1, gpre_ref[...]).astype(u_ref.dtype)


def resid_norm(h, x, gains, layer, row_post, row_pre):
    M, D = h.shape
    tm = _pick_tile(M, EW_ROWS)
    row = pl.BlockSpec((tm, D), lambda i: (i, 0))
    return pl.pallas_call(
        _resid_norm_body,
        grid=(M // tm,),
        in_specs=[row, row, _gain_spec(D, layer, row_post), _gain_spec(D, layer, row_pre)],
        out_specs=[row, row],
        out_shape=[jax.ShapeDtypeStruct((M, D), F32), jax.ShapeDtypeStruct((M, D), BF16)],
        compiler_params=_params(("parallel",)),
        name="resid_norm",
    )(h, x, gains, gains)


def _cast_body(x_ref, o_ref):
    o_ref[...] = x_ref[...].astype(o_ref.dtype)


def cast_bf16(w):
    L, K, N = w.shape
    rows = L * K
    tr = _pick_tile(rows, max(SUBLANES, CAST_TILE_BYTES // (4 * N) // SUBLANES * SUBLANES))
    out = pl.pallas_call(
        _cast_body,
        grid=(rows // tr,),
        in_specs=[pl.BlockSpec((tr, N), lambda i: (i, 0))],
        out_specs=pl.BlockSpec((tr, N), lambda i: (i, 0)),
        out_shape=jax.ShapeDtypeStruct((rows, N), BF16),
        compiler_params=_params(("parallel",)),
        name="cast_bf16",
    )(w.reshape(rows, N))
    return out.reshape(L, K, N)


def _mm_body(*refs, act_of, n_acts, n_extra, epilogue):
    n_w = len(act_of)
    a_refs = refs[:n_acts]
    w_refs = refs[n_acts:n_acts + n_w]
    e_refs = refs[n_acts + n_w:n_acts + n_w + n_extra]
    o_ref = refs[n_acts + n_w + n_extra]
    acts = [a[...] for a in a_refs]
    dots = [jnp.dot(acts[ai], w[...].astype(BF16), preferred_element_type=F32) for ai, w in zip(act_of, w_refs)]
    o_ref[...] = epilogue(dots, [e[...] for e in e_refs]).astype(o_ref.dtype)


def _tile_major(w, tn):
    L, K, N = w.shape
    return jnp.swapaxes(w.reshape(L, K, N // tn, tn), 1, 2)


def _weight_spec(w, layer, tn):
    if w.ndim == 4:
        assert w.shape[3] == tn
        return pl.BlockSpec((None, None, w.shape[2], tn), lambda i, j: (layer, j, 0, 0))
    return pl.BlockSpec((None, w.shape[1], tn), lambda i, j: (layer, 0, j))


def fused_mm(acts, weights, layer, extras, epilogue, n_out, out_dtype, name, tm_target=None, tn=None):
    M = acts[0].shape[0]
    tm = _pick_tile(M, ROW_TILE if tm_target is None else min(tm_target, ROW_TILE))
    tn = _pick_tile(n_out, COL_TILE, LANES) if tn is None else tn
    grid = (M // tm, pl.cdiv(n_out, tn))
    in_specs = ([pl.BlockSpec((tm, a.shape[1]), lambda i, j: (i, 0)) for a in acts]
                + [_weight_spec(w, layer, tn) for _, w in weights]
                + [pl.BlockSpec((tm, tn), functools.partial(lambda i, j, off: (i, j + off), off=off))
                   for _, off in extras])
    body = functools.partial(_mm_body, act_of=tuple(a for a, _ in weights), n_acts=len(acts),
                             n_extra=len(extras), epilogue=epilogue)
    return pl.pallas_call(
        body,
        grid=grid,
        in_specs=in_specs,
        out_specs=pl.BlockSpec((tm, tn), lambda i, j: (i, j)),
        out_shape=jax.ShapeDtypeStruct((M, n_out), out_dtype),
        compiler_params=_params(("parallel", "parallel")),
        name=name,
    )(*acts, *[w for _, w in weights], *[e for e, _ in extras])


def _ep_identity(dots, extras):
    return dots[0]


def _ep_merge(dots, extras):
    return jax.nn.sigmoid(extras[0]) * dots[0] + jax.nn.sigmoid(extras[1]) * dots[1]


def _ep_swiglu(dots, extras):
    return jax.nn.silu(dots[0]) * dots[1]


def _ep_ple(dots, extras):
    return extras[0] + jax.nn.sigmoid(dots[0]) * dots[1]


def _shift_mix(x, prev_first, mu):
    rolled = pltpu.roll(x, 1, axis=0)
    row = lax.broadcasted_iota(jnp.int32, x.shape, 0)
    prev = jnp.where(row == 0, prev_first, rolled)
    return x + (prev - x) * mu


def _head_sum(x, n_heads):
    s = x[:, 0:LANES]
    for b in range(1, x.shape[1] // LANES):
        s = s + x[:, b * LANES:(b + 1) * LANES]
    shift = LANES // 2
    while shift >= n_heads:
        s = s + pltpu.roll(s, shift, axis=1)
        shift //= 2
    return s


def _tile_cols(x, n):
    return jnp.concatenate([x] * n, axis=1)


def _replicate(x, rep, out_ref):
    rows = x.shape[0]
    n_blk = x.shape[1] // LANES
    grp = rep.shape[1]
    xs = jnp.concatenate([x[:, b * LANES:(b + 1) * LANES] for b in range(n_blk)], axis=0)
    hi = xs.astype(BF16)
    rest = xs - hi.astype(F32)
    mid = rest.astype(BF16)
    lo = (rest - mid.astype(F32)).astype(BF16)
    y = jnp.dot(jnp.concatenate([hi, mid, lo], axis=1), rep, preferred_element_type=F32)
    for b in range(n_blk):
        out_ref[:, b * grp:(b + 1) * grp] = y[b * rows:(b + 1) * rows, :]


def _rwkv_prep_body(r_ref, k_ref, v_ref, sm_ref, rp_ref, kp_ref, vp_ref, smp_ref,
                    rs_ref, ks_ref, vs_ref, sms_ref, vec_ref, musm_ref,
                    w2_ref, a2_ref, g2_ref, rep_ref,
                    nkk_ref, wr_ref, kka_ref, kmod_ref, pend_ref, c_ref, vmix_ref, g_ref,
                    *, ld, la, lgp, n_heads):
    j = pl.program_id(1)
    first = j == 0

    def prev_row(p_ref, s_ref):
        return jnp.where(first, s_ref[...], p_ref[SUBLANES - 1:SUBLANES, :])

    def vec(row):
        return vec_ref[row:row + 1, :]

    r = _shift_mix(r_ref[...], prev_row(rp_ref, rs_ref), vec(V_MU_R))
    k = _shift_mix(k_ref[...], prev_row(kp_ref, ks_ref), vec(V_MU_K))
    vmix_ref[...] = _shift_mix(v_ref[...], prev_row(vp_ref, vs_ref), vec(V_MU_V))
    sm = _shift_mix(sm_ref[...], prev_row(smp_ref, sms_ref), musm_ref[...])

    wd = jnp.tanh(sm[:, 0:ld]).astype(BF16)
    ad = sm[:, ld:ld + la].astype(BF16)
    gd = jax.nn.sigmoid(sm[:, ld + la:ld + la + lgp]).astype(BF16)
    g_ref[...] = jnp.dot(gd, g2_ref[...], preferred_element_type=F32)

    tc = r.shape[0]
    n_blk = r.shape[1] // LANES
    w = -_softplus(-(vec(V_W0) + jnp.dot(wd, w2_ref[...], preferred_element_type=F32))) - 0.5
    neg_log_decay = jnp.exp(w)
    a = jax.nn.sigmoid(vec(V_A0) + jnp.dot(ad, a2_ref[...], preferred_element_type=F32))
    kk = k * vec(V_KK)
    inv_norm = 1.0 / jnp.maximum(jnp.sqrt(_head_sum(kk * kk, n_heads)), 1e-12)
    kkn = kk * _tile_cols(inv_norm, n_blk)
    kmod = k * (1.0 + (a - 1.0) * vec(V_KA))
    kka = kkn * a
    rk = r * kmod
    c_ref[:, 0:LANES] = _head_sum(kka * r, n_heads)
    c_ref[:, LANES:2 * LANES] = _head_sum(rk, n_heads)
    c_ref[:, 2 * LANES:3 * LANES] = _head_sum(rk * vec(V_RK), n_heads)

    ti = lax.broadcasted_iota(jnp.int32, (tc, tc), 0)
    si = lax.broadcasted_iota(jnp.int32, (tc, tc), 1)
    tri = (si <= ti).astype(F32)
    log_p = -jnp.dot(tri, neg_log_decay, precision=HIGHEST, preferred_element_type=F32)
    p_incl = jnp.exp(log_p)
    p_prev = jnp.exp(log_p + neg_log_decay)
    inv_p = jnp.exp(-log_p)

    rep = rep_ref[...]
    _replicate(-kkn * p_prev, rep, nkk_ref)
    _replicate(r * p_incl, rep, wr_ref)
    _replicate(kka * inv_p, rep, kka_ref)
    _replicate(kmod * inv_p, rep, kmod_ref)
    _replicate(p_incl[tc - SUBLANES:tc, :], rep, pend_ref)


def rwkv_prep(proj_r, row_off, n_seq, T, shift, shift_layer, lw, layer):
    RW = lw['vec'].shape[-1]
    rep_w = R_HEAD_DIM * LANES
    tc = min(SEQ_CHUNK, T)
    assert T % tc == 0 and row_off % tc == 0 and tc % SUBLANES == 0
    nch = T // tc
    rows = n_seq * T
    off_b = row_off // tc
    sm_col = 3 * RW // SMALL_W
    ld, la, lgp = lw['ld'], lw['la'], lw['lgp']

    def tile(width, colblk):
        return pl.BlockSpec((tc, width), lambda s, j: (off_b + s * nch + j, colblk))

    def prev(width, colblk):
        per = tc // SUBLANES
        return pl.BlockSpec((SUBLANES, width),
                            lambda s, j: (jnp.maximum((off_b + s * nch + j) * per - 1, 0), colblk))

    def state(width, colblk):
        return pl.BlockSpec((None, None, 1, width), lambda s, j: (shift_layer, s, 0, colblk))

    def stacked(arr):
        return pl.BlockSpec((None,) + arr.shape[1:], lambda s, j: (layer,) + (0,) * (arr.ndim - 1))

    def out(width):
        return pl.BlockSpec((tc, width), lambda s, j: (s * nch + j, 0))

    in_specs = [tile(RW, 0), tile(RW, 1), tile(RW, 2), tile(SMALL_W, sm_col),
                prev(RW, 0), prev(RW, 1), prev(RW, 2), prev(SMALL_W, sm_col),
                state(RW, 0), state(RW, 1), state(RW, 2), state(SMALL_W, sm_col),
                stacked(lw['vec']), stacked(lw['mu_sm']),
                stacked(lw['w2']), stacked(lw['a2']), stacked(lw['g2']),
                pl.BlockSpec(lw['rep'].shape, lambda s, j: (0, 0))]
    out_specs = ([out(rep_w)] * 4
                 + [pl.BlockSpec((SUBLANES, rep_w), lambda s, j: (s * nch + j, 0)),
                    out(3 * LANES), out(RW), out(RW)])
    out_shape = ([jax.ShapeDtypeStruct((rows, rep_w), F32)] * 4
                 + [jax.ShapeDtypeStruct((n_seq * nch * SUBLANES, rep_w), F32),
                    jax.ShapeDtypeStruct((rows, 3 * LANES), F32),
                    jax.ShapeDtypeStruct((rows, RW), F32), jax.ShapeDtypeStruct((rows, RW), F32)])
    body = functools.partial(_rwkv_prep_body, ld=ld, la=la, lgp=lgp, n_heads=RW // R_HEAD_DIM)
    return pl.pallas_call(
        body,
        grid=(n_seq, nch),
        in_specs=in_specs,
        out_specs=out_specs,
        out_shape=out_shape,
        compiler_params=_params(("parallel", "arbitrary")),
        name="rwkv_prep",
    )(proj_r, proj_r, proj_r, proj_r, proj_r, proj_r, proj_r, proj_r,
      shift, shift, shift, shift, lw['vec'], lw['mu_sm'], lw['w2'], lw['a2'], lw['g2'], lw['rep'])


def _rwkv_rec_body(nkk_ref, wr_ref, kka_ref, kmod_ref, pend_ref, v_ref, c_ref, s0_ref,
                   y_ref, sout_ref, s_ref, *, tc, n_grp, nch):
    j = pl.program_id(1)

    @pl.when(j == 0)
    def _():
        s_ref[...] = s0_ref[...]

    def row(ref, t, k):
        return jnp.broadcast_to(ref[t, pl.ds(k, 1), :], (SUBLANES, LANES))

    def reduce(t, tiles):
        acc = [[None, None] for _ in range(2 * n_grp)]
        for k in range(R_HEAD_DIM):
            nkk = row(nkk_ref, t, k)
            wr = row(wr_ref, t, k)
            for g, sg in enumerate(tiles(k)):
                for slot, term in ((g, sg * nkk), (n_grp + g, sg * wr)):
                    prev = acc[slot][k % 2]
                    acc[slot][k % 2] = term if prev is None else prev + term
        return tuple(a[0] + a[1] for a in acc)

    def step(t, carry):
        sa, yp = carry[:n_grp], carry[n_grp:]
        crow = c_ref[pl.ds(t, 1), :]
        c1 = crow[:, 0:LANES]
        c2 = crow[:, LANES:2 * LANES]
        vg = [v_ref[t, g * SUBLANES:(g + 1) * SUBLANES, :] for g in range(n_grp)]
        for g in range(n_grp):
            y_ref[t, g * SUBLANES:(g + 1) * SUBLANES, :] = yp[g] + sa[g] * c1 + vg[g] * c2

        def updated(k):
            kka = row(kka_ref, t, k)
            kmod = row(kmod_ref, t, k)
            new = [s_ref[g, k] + sa[g] * kka + vg[g] * kmod for g in range(n_grp)]
            for g in range(n_grp):
                s_ref[g, k] = new[g]
            return new

        return reduce(jnp.minimum(t + 1, tc - 1), updated)

    carry0 = reduce(0, lambda k: [s_ref[g, k] for g in range(n_grp)])
    lax.fori_loop(0, tc, step, carry0)

    for k in range(R_HEAD_DIM):
        p_end = row(pend_ref, SUBLANES - 1, k)
        for g in range(n_grp):
            s_ref[g, k] = s_ref[g, k] * p_end

    @pl.when(j == nch - 1)
    def _():
        sout_ref[...] = s_ref[...]


def rwkv_rec(nkk, wr, kka, kmod, pend, vmix, c, s0, s0_layer, n_seq, T):
    rows = n_seq * T
    n_grp = s0.shape[2]
    v_rows = n_grp * SUBLANES
    tc = min(SEQ_CHUNK, T)
    nch = T // tc
    k3 = lambda x: x.reshape(x.shape[0], R_HEAD_DIM, LANES)
    kspec = pl.BlockSpec((tc, R_HEAD_DIM, LANES), lambda s, j: (s * nch + j, 0, 0))
    pspec = pl.BlockSpec((SUBLANES, R_HEAD_DIM, LANES), lambda s, j: (s * nch + j, 0, 0))
    vspec = pl.BlockSpec((tc, v_rows, LANES), lambda s, j: (s * nch + j, 0, 0))
    tile = (n_grp, R_HEAD_DIM, SUBLANES, LANES)
    s_in = pl.BlockSpec((None, None) + tile, lambda s, j: (s0_layer, s, 0, 0, 0, 0))
    s_out = pl.BlockSpec((None,) + tile, lambda s, j: (s, 0, 0, 0, 0))
    body = functools.partial(_rwkv_rec_body, tc=tc, n_grp=n_grp, nch=nch)
    return pl.pallas_call(
        body,
        grid=(n_seq, nch),
        in_specs=[kspec] * 4 + [pspec, vspec, pl.BlockSpec((tc, 3 * LANES), lambda s, j: (s * nch + j, 0)), s_in],
        out_specs=[vspec, s_out],
        out_shape=[jax.ShapeDtypeStruct((rows, v_rows, LANES), F32),
                   jax.ShapeDtypeStruct(s0.shape[1:], F32)],
        scratch_shapes=[pltpu.VMEM(tile, F32)],
        compiler_params=_params(("parallel", "arbitrary")),
        name="rwkv_rec",
    )(k3(nkk), k3(wr), k3(kka), k3(kmod), k3(pend), vmix.reshape(rows, v_rows, LANES), c, s0)


def _rwkv_post_body(y_ref, v_ref, g_ref, c_ref, ln_ref, o_ref, *, n_grp, n_heads):
    def head_sum(x):
        return _tile_cols(_head_sum(x, n_heads), n_grp)

    y = y_ref[...]
    inv_n = 1.0 / R_HEAD_DIM
    mu = head_sum(y) * inv_n
    d = y - mu
    var = head_sum(d * d) * inv_n
    yln = d * lax.rsqrt(var + R_GN_EPS) * ln_ref[0:1, :] + ln_ref[1:2, :]
    c3 = _tile_cols(c_ref[:, 2 * LANES:3 * LANES], n_grp)
    o_ref[...] = ((yln + c3 * v_ref[...]) * g_ref[...]).astype(o_ref.dtype)


def rwkv_post(y, vmix, g, c, ln, layer, n_heads):
    M, RW = y.shape
    tm = _pick_tile(M, EW_ROWS)
    row = pl.BlockSpec((tm, RW), lambda i: (i, 0))
    body = functools.partial(_rwkv_post_body, n_grp=RW // LANES, n_heads=n_heads)
    return pl.pallas_call(
        body,
        grid=(M // tm,),
        in_specs=[row, row, row, pl.BlockSpec((tm, 3 * LANES), lambda i: (i, 0)),
                  pl.BlockSpec((None, 2, RW), lambda i: (layer, 0, 0))],
        out_specs=row,
        out_shape=jax.ShapeDtypeStruct((M, RW), BF16),
        compiler_params=_params(("parallel",)),
        name="rwkv_post",
    )(y, vmix, g, c, ln)


def _gla_body(q_ref, k_ref, v_ref, rg_ref, ag_ref, a2_ref, ab_ref, gn_ref, s0_ref,
              o_ref, sout_ref, st_ref, *, C, nch, n_heads, dk, dv):
    c = pl.program_id(1)

    @pl.when(c == 0)
    def _():
        st_ref[...] = s0_ref[...]

    x = jnp.dot(ag_ref[...], a2_ref[...], precision=HIGHEST, preferred_element_type=F32) + ab_ref[...]
    log_a = -_softplus(-x) * (1.0 / G_GATE_TAU)
    ti = lax.broadcasted_iota(jnp.int32, (C, C), 0)
    si = lax.broadcasted_iota(jnp.int32, (C, C), 1)
    tri = (si <= ti).astype(F32)
    b_all = jnp.dot(tri, log_a, precision=HIGHEST, preferred_element_type=F32)

    nt = (((1,), (1,)), ((), ()))
    tn = (((0,), (0,)), ((), ()))
    scale = float(dk) ** -0.5
    rowi = lax.broadcasted_iota(jnp.int32, (GLA_SUB, dk), 0)
    lane = lax.broadcasted_iota(jnp.int32, (GLA_SUB, LANES), 1)

    for h in range(n_heads):
        ksl = slice(h * dk, (h + 1) * dk)
        vsl = slice(h * dv, (h + 1) * dv)
        b = b_all[:, ksl]
        b_last = b[C - 1:C, :]
        q = q_ref[:, ksl] * scale
        k = k_ref[:, ksl]
        v = v_ref[:, vsl]
        vh = v.astype(BF16)
        st = st_ref[h]

        inter = lax.dot_general((q * jnp.exp(b)).astype(BF16), st.astype(BF16), nt,
                                preferred_element_type=F32)
        blocks = []
        for blk in range(C // GLA_SUB):
            t0 = blk * GLA_SUB
            rows = slice(t0, t0 + GLA_SUB)
            qb, kb, bb = q[rows], k[rows], b[rows]
            acc = inter[rows]
            if blk > 0:
                b0 = bb[0:1, :]
                qd = (qb * jnp.exp(bb - b0)).astype(BF16)
                kd = (k[0:t0] * jnp.exp(b0 - b[0:t0])).astype(BF16)
                sc = lax.dot_general(qd, kd, nt, preferred_element_type=F32)
                acc = acc + jnp.dot(sc.astype(BF16), vh[0:t0], preferred_element_type=F32)
            sc_d = jnp.zeros((GLA_SUB, LANES), F32)
            for s in range(GLA_SUB):
                e = jnp.where(rowi >= s, jnp.exp(bb - bb[s:s + 1, :]), 0.0)
                sc_s = jnp.sum(qb * e * kb[s:s + 1, :], axis=-1, keepdims=True)
                sc_d = jnp.where(lane == s, sc_s, sc_d)
            acc = acc + jnp.dot(sc_d[:, 0:GLA_SUB].astype(BF16), vh[rows], preferred_element_type=F32)
            blocks.append(acc)
        o = jnp.concatenate(blocks, axis=0) if len(blocks) > 1 else blocks[0]

        kdec = (k * jnp.exp(b_last - b)).astype(BF16)
        st_ref[h] = st * jnp.exp(b_last) + lax.dot_general(vh, kdec, tn, preferred_element_type=F32)

        ms = jnp.mean(o * o, axis=-1, keepdims=True)
        o_ref[:, vsl] = (o * lax.rsqrt(ms + NORM_EPS) * gn_ref[:, vsl]
                         * jax.nn.silu(rg_ref[:, vsl])).astype(o_ref.dtype)

    @pl.when(c == nch - 1)
    def _():
        sout_ref[...] = st_ref[...]


def gla(proj_g, proj_r, row_off, n_seq, T, s0t, s0_layer, lw, layer, o_ag):
    GK, GV = lw['gla_ab'].shape[-1], lw['gla_norm'].shape[-1]
    dk, dv = GK // G_HEADS, GV // G_HEADS
    C = min(SEQ_CHUNK, T)
    assert T % C == 0 and C % GLA_SUB == 0 and row_off % C == 0
    assert GV % GK == 0 and (2 * GK) % GV == 0 and o_ag % LANES == 0
    nch = T // C
    rows = n_seq * T
    off_b = row_off // C

    def tile(width, col0):
        assert col0 % width == 0
        return pl.BlockSpec((C, width), lambda s, c: (off_b + s * nch + c, col0 // width))

    def stacked(arr):
        return pl.BlockSpec((None,) + arr.shape[1:], lambda s, c: (layer,) + (0,) * (arr.ndim - 1))

    in_specs = [tile(GK, 0), tile(GK, GK), tile(GV, 2 * GK), tile(GV, 2 * GK + GV),
                tile(LANES, o_ag),
                stacked(lw['gla_a2']), stacked(lw['gla_ab']), stacked(lw['gla_norm']),
                pl.BlockSpec((None, None, G_HEADS, dv, dk), lambda s, c: (s0_layer, s, 0, 0, 0))]
    body = functools.partial(_gla_body, C=C, nch=nch, n_heads=G_HEADS, dk=dk, dv=dv)
    return pl.pallas_call(
        body,
        grid=(n_seq, nch),
        in_specs=in_specs,
        out_specs=[pl.BlockSpec((C, GV), lambda s, c: (s * nch + c, 0)),
                   pl.BlockSpec((None, G_HEADS, dv, dk), lambda s, c: (s, 0, 0, 0))],
        out_shape=[jax.ShapeDtypeStruct((rows, GV), BF16),
                   jax.ShapeDtypeStruct(s0t.shape[1:], F32)],
        scratch_shapes=[pltpu.VMEM((G_HEADS, dv, dk), F32)],
        compiler_params=_params(("parallel", "arbitrary")),
        name="gla",
    )(proj_g, proj_g, proj_g, proj_g, proj_r, lw['gla_a2'], lw['gla_ab'], lw['gla_norm'], s0t)


def _hx_to_xh(w, n_heads):
    sh = w.shape[:-1]
    x = w.shape[-1] // n_heads
    return jnp.swapaxes(w.reshape(*sh, n_heads, x), -1, -2).reshape(*sh, n_heads * x)


def _xh_to_hx(w, n_heads):
    sh = w.shape[:-1]
    x = w.shape[-1] // n_heads
    return jnp.swapaxes(w.reshape(*sh, x, n_heads), -1, -2).reshape(*sh, n_heads * x)


def _pad_last(x, width):
    return jnp.pad(x, [(0, 0)] * (x.ndim - 1) + [(0, width - x.shape[-1])])


def _split_r_cols(x, RW, ld, la, lg):
    offs = [0, RW, 2 * RW, 3 * RW, 3 * RW + ld, 3 * RW + ld + la, 3 * RW + ld + la + lg]
    return [x[..., offs[i]:offs[i + 1]] for i in range(6)]


def _small_block(wd, ad, gd, lgp, tail=None):
    parts = [wd, ad, _pad_last(gd, lgp)]
    if tail is not None:
        parts.append(tail)
    return _pad_last(jnp.concatenate(parts, axis=-1), SMALL_W)


def _r_cols_to_kernel(x, RW, ld, la, lg, lgp, n_heads, tail=None):
    x_r, x_k, x_v, x_wd, x_ad, x_gd = _split_r_cols(x, RW, ld, la, lg)
    return jnp.concatenate([_hx_to_xh(x_r, n_heads), _hx_to_xh(x_k, n_heads), _hx_to_xh(x_v, n_heads),
                            _small_block(x_wd, x_ad, x_gd, lgp, tail)], axis=-1)


def _r_cols_from_kernel(x, RW, ld, la, lg, n_heads):
    sm = x[..., 3 * RW:]
    return jnp.concatenate(
        [_xh_to_hx(x[..., 0:RW], n_heads), _xh_to_hx(x[..., RW:2 * RW], n_heads),
         _xh_to_hx(x[..., 2 * RW:3 * RW], n_heads),
         sm[..., 0:ld], sm[..., ld:ld + la], sm[..., ld + la:ld + la + lg]], axis=-1)


def _state_to_kernel(s, n_heads):
    lead = s.shape[:-3]
    n = len(lead)
    vv = LANES // n_heads
    g = R_HEAD_DIM // (SUBLANES * vv)
    x = s.reshape(*lead, n_heads, g, SUBLANES, vv, R_HEAD_DIM)
    x = x.transpose(*range(n), n + 1, n + 4, n + 2, n + 3, n)
    return x.reshape(*lead, g, R_HEAD_DIM, SUBLANES, LANES)


def _state_from_kernel(s, n_heads):
    lead = s.shape[:-4]
    n = len(lead)
    g = s.shape[-4]
    vv = LANES // n_heads
    x = s.reshape(*lead, g, R_HEAD_DIM, SUBLANES, vv, n_heads)
    x = x.transpose(*range(n), n + 4, n, n + 2, n + 3, n + 1)
    return x.reshape(*lead, n_heads, R_HEAD_DIM, R_HEAD_DIM)


def kernel(x_prompt, x_sample, state_shift, state_rwkv, state_gla, p_prompt, p_sample, norm_mix_pre, norm_mix_post, norm_ffn_pre, norm_ffn_post, norm_ple, w_in, rwkv_mu, rwkv_w0, rwkv_w2, rwkv_a0, rwkv_a2, rwkv_g2, rwkv_k_k, rwkv_k_a, rwkv_r_k, rwkv_ln_w, rwkv_ln_b, gla_a2, gla_ab, gla_norm, w_proj_rwkv, w_proj_gla, w_out, ffn_gate, ffn_up, ffn_down, ple_proj, ple_gate):
    Bp, Tp, D = x_prompt.shape
    Bs, Ts, _ = x_sample.shape
    L = w_in.shape[0]
    RW = rwkv_w0.shape[-1]
    H = RW // R_HEAD_DIM
    assert LANES % H == 0 and R_HEAD_DIM % (SUBLANES * (LANES // H)) == 0
    ld, la, lg = rwkv_w2.shape[1], rwkv_a2.shape[1], rwkv_g2.shape[1]
    lgp = _round_up(lg, LANES)
    GK, GV, GR = gla_a2.shape[-1], gla_norm.shape[-1], gla_a2.shape[1]
    FF = ffn_gate.shape[-1]
    Mp, Ms = Bp * Tp, Bs * Ts
    groups = [(0, Bp, Tp), (Mp, Bs, Ts)]

    NR = 3 * RW + SMALL_W
    NG = 2 * GK + 2 * GV
    o_ag = 3 * RW + ld + la + lgp
    assert o_ag + LANES <= NR and (3 * RW) % SMALL_W == 0 and GR <= LANES
    r_cols = 3 * RW + ld + la + lg
    g_cols = NG + GR
    to_kernel = functools.partial(_r_cols_to_kernel, RW=RW, ld=ld, la=la, lg=lg, lgp=lgp, n_heads=H)
    win_r = to_kernel(w_in[..., :r_cols], tail=w_in[..., r_cols + NG:r_cols + g_cols]).astype(BF16)
    win_g = w_in[..., r_cols:r_cols + NG].astype(BF16)
    win_gate = w_in[..., r_cols + g_cols:].astype(BF16)
    assert win_r.shape[-1] == NR and win_gate.shape[-1] == 2 * D
    assert NR % COL_TILE == 0 and NG % COL_TILE == 0 and (2 * D) % COL_TILE == 0
    win_r, win_g, win_gate = (_tile_major(w, COL_TILE) for w in (win_r, win_g, win_gate))

    mu_k = to_kernel(rwkv_mu)
    kq = LANES // H
    src = jnp.arange(LANES)
    dst = jnp.arange(kq * LANES)
    rep_mat = jnp.tile((src[:, None] == ((dst // LANES) * H + dst % H)[None, :]).astype(BF16), (3, 1))
    lw = dict(
        vec=jnp.stack([mu_k[:, 0:RW], mu_k[:, RW:2 * RW], mu_k[:, 2 * RW:3 * RW],
                       _hx_to_xh(rwkv_w0, H), _hx_to_xh(rwkv_a0, H), _hx_to_xh(rwkv_k_k, H),
                       _hx_to_xh(rwkv_k_a, H), _hx_to_xh(rwkv_r_k.reshape(L, RW), H)], axis=1),
        mu_sm=mu_k[:, None, 3 * RW:],
        w2=_hx_to_xh(rwkv_w2, H).astype(BF16), a2=_hx_to_xh(rwkv_a2, H).astype(BF16),
        g2=jnp.pad(_hx_to_xh(rwkv_g2, H), ((0, 0), (0, lgp - lg), (0, 0))).astype(BF16),
        rep=rep_mat, ld=ld, la=la, lgp=lgp,
        ln=jnp.stack([_hx_to_xh(rwkv_ln_w, H), _hx_to_xh(rwkv_ln_b, H)], axis=1),
        gla_a2=jnp.pad(gla_a2, ((0, 0), (0, LANES - GR), (0, 0))), gla_ab=gla_ab[:, None], gla_norm=gla_norm[:, None])
    gains = jnp.stack([norm_mix_pre, norm_mix_post, norm_ffn_pre, norm_ffn_post, norm_ple], axis=1)[:, :, None]
    w_pr = jnp.swapaxes(w_proj_rwkv.reshape(L, H, R_HEAD_DIM, D), 1, 2).reshape(L, RW, D).astype(BF16)
    w_pg, w_o, w_fg, w_fu, w_plg, w_plp = w_proj_gla, w_out, ffn_gate, ffn_up, ple_gate, ple_proj
    w_fd = cast_bf16(ffn_down)

    shift_states = [jnp.zeros((1, Bp, 1, NR), F32), to_kernel(state_shift)[:, :, None]]
    n_vgrp = R_HEAD_DIM // (SUBLANES * (LANES // H))
    rwkv_states = [jnp.zeros((1, Bp, n_vgrp, R_HEAD_DIM, SUBLANES, LANES), F32),
                   _state_to_kernel(state_rwkv, H)]
    gla_states = [jnp.zeros((1, Bp, G_HEADS, GV // G_HEADS, GK // G_HEADS), F32), jnp.swapaxes(state_gla, -1, -2)]

    h = jnp.concatenate([x_prompt.reshape(Mp, D), x_sample.reshape(Ms, D)], axis=0)
    p_all = jnp.concatenate([p_prompt.reshape(L, Mp, -1), p_sample.reshape(L, Ms, -1)], axis=1).astype(BF16)

    new_shift = [[], []]
    new_rwkv = [[], []]
    new_gla = [[], []]
    tn = _pick_tile(D, COL_TILE, LANES)
    for i in range(L):
        z = norm_cast(h, gains, i, G_MIX_PRE)
        proj_r = fused_mm([z], [(0, win_r)], i, [], _ep_identity, NR, F32, "in_proj_r")
        proj_g = fused_mm([z], [(0, win_g)], i, [], _ep_identity, NG, F32, "in_proj_g")
        proj_gate = fused_mm([z], [(0, win_gate)], i, [], _ep_identity, 2 * D, F32, "in_proj_gate")

        o_r_parts, o_g_parts = [], []
        for gi, (row_off, n_seq, T) in enumerate(groups):
            s_layer = i if gi == 1 else 0
            nkk, wr, kka, kmod, pend, c, vmix, g = rwkv_prep(
                proj_r, row_off, n_seq, T, shift_states[gi], s_layer, lw, i)
            y, s_new = rwkv_rec(nkk, wr, kka, kmod, pend, vmix, c, rwkv_states[gi], s_layer, n_seq, T)
            o_r_parts.append(rwkv_post(y.reshape(n_seq * T, RW), vmix, g, c, lw['ln'], i, H))
            o_g, s_gla_new = gla(proj_g, proj_r, row_off, n_seq, T, gla_states[gi], s_layer, lw, i, o_ag)
            o_g_parts.append(o_g)
            new_shift[gi].append(proj_r[row_off + T - 1:row_off + n_seq * T:T])
            new_rwkv[gi].append(s_new)
            new_gla[gi].append(s_gla_new)
        o_r = jnp.concatenate(o_r_parts, axis=0)
        o_g = jnp.concatenate(o_g_parts, axis=0)

        merged = fused_mm([o_r, o_g], [(0, w_pr), (1, w_pg)], i, [(proj_gate, 0), (proj_gate, D // tn)],
                          _ep_merge, D, BF16, "merge", tm_target=ROW_TILE // 2)
        mix = fused_mm([merged], [(0, w_o)], i, [], _ep_identity, D, F32, "out_proj")
        h1, u = resid_norm(h, mix, gains, i, G_MIX_POST, G_FFN_PRE)
        t = fused_mm([u], [(0, w_fg), (0, w_fu)], i, [], _ep_swiglu, FF, BF16, "ffn_up", tn=COL_TILE // 2)
        f = fused_mm([t], [(0, w_fd)], i, [], _ep_identity, D, F32, "ffn_down",
                     tm_target=ROW_TILE // 2, tn=COL_TILE // 2)
        h2, pn = resid_norm(h1, f, gains, i, G_FFN_POST, G_PLE)
        h = fused_mm([pn, p_all[i]], [(0, w_plg), (1, w_plp)], i, [(h2, 0)], _ep_ple, D, F32, "ple_mm",
                     tn=COL_TILE // 2)

    from_kernel = functools.partial(_r_cols_from_kernel, RW=RW, ld=ld, la=la, lg=lg, n_heads=H)
    y_prompt = h[:Mp].reshape(Bp, Tp, D)
    y_sample = h[Mp:].reshape(Bs, Ts, D)
    return (y_prompt, y_sample,
            from_kernel(jnp.stack(new_shift[0])), _state_from_kernel(jnp.stack(new_rwkv[0]), H),
            jnp.swapaxes(jnp.stack(new_gla[0]), -1, -2),
            from_kernel(jnp.stack(new_shift[1])), _state_from_kernel(jnp.stack(new_rwkv[1]), H),
            jnp.swapaxes(jnp.stack(new_gla[1]), -1, -2))
```

```python
import functools

import jax
import jax.numpy as jnp
from jax import lax
from jax.experimental import pallas as pl
from jax.experimental.pallas import tpu as pltpu

F32 = jnp.float32
BF16 = jnp.bfloat16
HIGHEST = lax.Precision.HIGHEST

NORM_EPS = 1e-6
R_GN_EPS = 64e-5
R_HEAD_DIM = 64
G_HEADS = 8
G_GATE_TAU = 16.0

LANES = 128
SUBLANES = 8
VMEM_LIMIT = 56 * 1024 * 1024

ROW_TILE = 1536
COL_TILE = 512
EW_ROWS = 256
CAST_TILE_BYTES = 8 * 1024 * 1024
SEQ_CHUNK = 64
GLA_SUB = 16
SMALL_W = 1024

G_MIX_PRE, G_MIX_POST, G_FFN_PRE, G_FFN_POST, G_PLE = range(5)
V_MU_R, V_MU_K, V_MU_V, V_W0, V_A0, V_KK, V_KA, V_RK = range(8)


def _round_up(x, m):
    return (x + m - 1) // m * m


def _pick_tile(n, target, mult=SUBLANES):
    best = None
    for t in range(mult, min(n, target) + 1, mult):
        if n % t == 0:
            best = t
    assert best is not None, (n, target)
    return best


def _params(sem):
    return pltpu.CompilerParams(dimension_semantics=sem, vmem_limit_bytes=VMEM_LIMIT)


def _rms(x, gain):
    ms = jnp.mean(x * x, axis=-1, keepdims=True)
    return x * lax.rsqrt(ms + NORM_EPS) * gain


def _softplus(x):
    return jnp.maximum(x, 0.0) + jnp.log(1.0 + jnp.exp(-jnp.abs(x)))


def _gain_spec(D, layer, row):
    return pl.BlockSpec((None, None, 1, D), lambda *_: (layer, row, 0, 0))


def _norm_cast_body(x_ref, g_ref, o_ref):
    o_ref[...] = _rms(x_ref[...], g_ref[...]).astype(o_ref.dtype)


def norm_cast(x, gains, layer, row):
    M, D = x.shape
    tm = _pick_tile(M, EW_ROWS)
    return pl.pallas_call(
        _norm_cast_body,
        grid=(M // tm,),
        in_specs=[pl.BlockSpec((tm, D), lambda i: (i, 0)), _gain_spec(D, layer, row)],
        out_specs=pl.BlockSpec((tm, D), lambda i: (i, 0)),
        out_shape=jax.ShapeDtypeStruct((M, D), BF16),
        compiler_params=_params(("parallel",)),
        name="norm_cast",
    )(x, gains)


def _resid_norm_body(h_ref, x_ref, gpost_ref, gpre_ref, h1_ref, u_ref):
    h1 = h_ref[...] + _rms(x_ref[...], gpost_ref[...])
    h1_ref[...] = h1
    u_ref[...] = _rms(h1, gpre_ref[...]).astype(u_ref.dtype)


def resid_norm(h, x, gains, layer, row_post, row_pre):
    M, D = h.shape
    tm = _pick_tile(M, EW_ROWS)
    row = pl.BlockSpec((tm, D), lambda i: (i, 0))
    return pl.pallas_call(
        _resid_norm_body,
        grid=(M // tm,),
        in_specs=[row, row, _gain_spec(D, layer, row_post), _gain_spec(D, layer, row_pre)],
        out_specs=[row, row],
        out_shape=[jax.ShapeDtypeStruct((M, D), F32), jax.ShapeDtypeStruct((M, D), BF16)],
        compiler_params=_params(("parallel",)),
        name="resid_norm",
    )(h, x, gains, gains)


def _cast_body(x_ref, o_ref):
    o_ref[...] = x_ref[...].astype(o_ref.dtype)


def cast_bf16(w):
    L, K, N = w.shape
    rows = L * K
    tr = _pick_tile(rows, max(SUBLANES, CAST_TILE_BYTES // (4 * N) // SUBLANES * SUBLANES))
    out = pl.pallas_call(
        _cast_body,
        grid=(rows // tr,),
        in_specs=[pl.BlockSpec((tr, N), lambda i: (i, 0))],
        out_specs=pl.BlockSpec((tr, N), lambda i: (i, 0)),
        out_shape=jax.ShapeDtypeStruct((rows, N), BF16),
        compiler_params=_params(("parallel",)),
        name="cast_bf16",
    )(w.reshape(rows, N))
    return out.reshape(L, K, N)


def _mm_body(*refs, act_of, n_acts, n_extra, epilogue):
    n_w = len(act_of)
    a_refs = refs[:n_acts]
    w_refs = refs[n_acts:n_acts + n_w]
    e_refs = refs[n_acts + n_w:n_acts + n_w + n_extra]
    o_ref = refs[n_acts + n_w + n_extra]
    acts = [a[...] for a in a_refs]
    dots = [jnp.dot(acts[ai], w[...].astype(BF16), preferred_element_type=F32) for ai, w in zip(act_of, w_refs)]
    o_ref[...] = epilogue(dots, [e[...] for e in e_refs]).astype(o_ref.dtype)


def _tile_major(w, tn):
    L, K, N = w.shape
    return jnp.swapaxes(w.reshape(L, K, N // tn, tn), 1, 2)


def _weight_spec(w, layer, tn):
    if w.ndim == 4:
        assert w.shape[3] == tn
        return pl.BlockSpec((None, None, w.shape[2], tn), lambda i, j: (layer, j, 0, 0))
    return pl.BlockSpec((None, w.shape[1], tn), lambda i, j: (layer, 0, j))


def fused_mm(acts, weights, layer, extras, epilogue, n_out, out_dtype, name, tm_target=None, tn=None):
    M = acts[0].shape[0]
    tm = _pick_tile(M, ROW_TILE if tm_target is None else min(tm_target, ROW_TILE))
    tn = _pick_tile(n_out, COL_TILE, LANES) if tn is None else tn
    grid = (M // tm, pl.cdiv(n_out, tn))
    in_specs = ([pl.BlockSpec((tm, a.shape[1]), lambda i, j: (i, 0)) for a in acts]
                + [_weight_spec(w, layer, tn) for _, w in weights]
                + [pl.BlockSpec((tm, tn), functools.partial(lambda i, j, off: (i, j + off), off=off))
                   for _, off in extras])
    body = functools.partial(_mm_body, act_of=tuple(a for a, _ in weights), n_acts=len(acts),
                             n_extra=len(extras), epilogue=epilogue)
    return pl.pallas_call(
        body,
        grid=grid,
        in_specs=in_specs,
        out_specs=pl.BlockSpec((tm, tn), lambda i, j: (i, j)),
        out_shape=jax.ShapeDtypeStruct((M, n_out), out_dtype),
        compiler_params=_params(("parallel", "parallel")),
        name=name,
    )(*acts, *[w for _, w in weights], *[e for e, _ in extras])


def _ep_identity(dots, extras):
    return dots[0]


def _ep_merge(dots, extras):
    return jax.nn.sigmoid(extras[0]) * dots[0] + jax.nn.sigmoid(extras[1]) * dots[1]


def _ep_swiglu(dots, extras):
    return jax.nn.silu(dots[0]) * dots[1]


def _ep_ple(dots, extras):
    return extras[0] + jax.nn.sigmoid(dots[0]) * dots[1]


def _shift_mix(x, prev_first, mu):
    rolled = pltpu.roll(x, 1, axis=0)
    row = lax.broadcasted_iota(jnp.int32, x.shape, 0)
    prev = jnp.where(row == 0, prev_first, rolled)
    return x + (prev - x) * mu


def _head_sum(x, n_heads):
    s = x[:, 0:LANES]
    for b in range(1, x.shape[1] // LANES):
        s = s + x[:, b * LANES:(b + 1) * LANES]
    shift = LANES // 2
    while shift >= n_heads:
        s = s + pltpu.roll(s, shift, axis=1)
        shift //= 2
    return s


def _tile_cols(x, n):
    return jnp.concatenate([x] * n, axis=1)


def _replicate(x, rep, out_ref):
    rows = x.shape[0]
    n_blk = x.shape[1] // LANES
    grp = rep.shape[1]
    xs = jnp.concatenate([x[:, b * LANES:(b + 1) * LANES] for b in range(n_blk)], axis=0)
    hi = xs.astype(BF16)
    rest = xs - hi.astype(F32)
    mid = rest.astype(BF16)
    lo = (rest - mid.astype(F32)).astype(BF16)
    y = jnp.dot(jnp.concatenate([hi, mid, lo], axis=1), rep, preferred_element_type=F32)
    for b in range(n_blk):
        out_ref[:, b * grp:(b + 1) * grp] = y[b * rows:(b + 1) * rows, :]


def _rwkv_prep_body(r_ref, k_ref, v_ref, sm_ref, rp_ref, kp_ref, vp_ref, smp_ref,
                    rs_ref, ks_ref, vs_ref, sms_ref, vec_ref, musm_ref,
                    w2_ref, a2_ref, g2_ref, rep_ref,
                    nkk_ref, wr_ref, kka_ref, kmod_ref, pend_ref, c_ref, vmix_ref, g_ref,
                    *, ld, la, lgp, n_heads):
    j = pl.program_id(1)
    first = j == 0

    def prev_row(p_ref, s_ref):
        return jnp.where(first, s_ref[...], p_ref[SUBLANES - 1:SUBLANES, :])

    def vec(row):
        return vec_ref[row:row + 1, :]

    r = _shift_mix(r_ref[...], prev_row(rp_ref, rs_ref), vec(V_MU_R))
    k = _shift_mix(k_ref[...], prev_row(kp_ref, ks_ref), vec(V_MU_K))
    vmix_ref[...] = _shift_mix(v_ref[...], prev_row(vp_ref, vs_ref), vec(V_MU_V))
    sm = _shift_mix(sm_ref[...], prev_row(smp_ref, sms_ref), musm_ref[...])

    wd = jnp.tanh(sm[:, 0:ld]).astype(BF16)
    ad = sm[:, ld:ld + la].astype(BF16)
    gd = jax.nn.sigmoid(sm[:, ld + la:ld + la + lgp]).astype(BF16)
    g_ref[...] = jnp.dot(gd, g2_ref[...], preferred_element_type=F32)

    tc = r.shape[0]
    n_blk = r.shape[1] // LANES
    w = -_softplus(-(vec(V_W0) + jnp.dot(wd, w2_ref[...], preferred_element_type=F32))) - 0.5
    neg_log_decay = jnp.exp(w)
    a = jax.nn.sigmoid(vec(V_A0) + jnp.dot(ad, a2_ref[...], preferred_element_type=F32))
    kk = k * vec(V_KK)
    inv_norm = 1.0 / jnp.maximum(jnp.sqrt(_head_sum(kk * kk, n_heads)), 1e-12)
    kkn = kk * _tile_cols(inv_norm, n_blk)
    kmod = k * (1.0 + (a - 1.0) * vec(V_KA))
    kka = kkn * a
    rk = r * kmod
    c_ref[:, 0:LANES] = _head_sum(kka * r, n_heads)
    c_ref[:, LANES:2 * LANES] = _head_sum(rk, n_heads)
    c_ref[:, 2 * LANES:3 * LANES] = _head_sum(rk * vec(V_RK), n_heads)

    ti = lax.broadcasted_iota(jnp.int32, (tc, tc), 0)
    si = lax.broadcasted_iota(jnp.int32, (tc, tc), 1)
    tri = (si <= ti).astype(F32)
    log_p = -jnp.dot(tri, neg_log_decay, precision=HIGHEST, preferred_element_type=F32)
    p_incl = jnp.exp(log_p)
    p_prev = jnp.exp(log_p + neg_log_decay)
    inv_p = jnp.exp(-log_p)

    rep = rep_ref[...]
    _replicate(-kkn * p_prev, rep, nkk_ref)
    _replicate(r * p_incl, rep, wr_ref)
    _replicate(kka * inv_p, rep, kka_ref)
    _replicate(kmod * inv_p, rep, kmod_ref)
    _replicate(p_incl[tc - SUBLANES:tc, :], rep, pend_ref)


def rwkv_prep(proj_r, row_off, n_seq, T, shift, shift_layer, lw, layer):
    RW = lw['vec'].shape[-1]
    rep_w = R_HEAD_DIM * LANES
    tc = min(SEQ_CHUNK, T)
    assert T % tc == 0 and row_off % tc == 0 and tc % SUBLANES == 0
    nch = T // tc
    rows = n_seq * T
    off_b = row_off // tc
    sm_col = 3 * RW // SMALL_W
    ld, la, lgp = lw['ld'], lw['la'], lw['lgp']

    def tile(width, colblk):
        return pl.BlockSpec((tc, width), lambda s, j: (off_b + s * nch + j, colblk))

    def prev(width, colblk):
        per = tc // SUBLANES
        return pl.BlockSpec((SUBLANES, width),
                            lambda s, j: (jnp.maximum((off_b + s * nch + j) * per - 1, 0), colblk))

    def state(width, colblk):
        return pl.BlockSpec((None, None, 1, width), lambda s, j: (shift_layer, s, 0, colblk))

    def stacked(arr):
        return pl.BlockSpec((None,) + arr.shape[1:], lambda s, j: (layer,) + (0,) * (arr.ndim - 1))

    def out(width):
        return pl.BlockSpec((tc, width), lambda s, j: (s * nch + j, 0))

    in_specs = [tile(RW, 0), tile(RW, 1), tile(RW, 2), tile(SMALL_W, sm_col),
                prev(RW, 0), prev(RW, 1), prev(RW, 2), prev(SMALL_W, sm_col),
                state(RW, 0), state(RW, 1), state(RW, 2), state(SMALL_W, sm_col),
                stacked(lw['vec']), stacked(lw['mu_sm']),
                stacked(lw['w2']), stacked(lw['a2']), stacked(lw['g2']),
                pl.BlockSpec(lw['rep'].shape, lambda s, j: (0, 0))]
    out_specs = ([out(rep_w)] * 4
                 + [pl.BlockSpec((SUBLANES, rep_w), lambda s, j: (s * nch + j, 0)),
                    out(3 * LANES), out(RW), out(RW)])
    out_shape = ([jax.ShapeDtypeStruct((rows, rep_w), F32)] * 4
                 + [jax.ShapeDtypeStruct((n_seq * nch * SUBLANES, rep_w), F32),
                    jax.ShapeDtypeStruct((rows, 3 * LANES), F32),
                    jax.ShapeDtypeStruct((rows, RW), F32), jax.ShapeDtypeStruct((rows, RW), F32)])
    body = functools.partial(_rwkv_prep_body, ld=ld, la=la, lgp=lgp, n_heads=RW // R_HEAD_DIM)
    return pl.pallas_call(
        body,
        grid=(n_seq, nch),
        in_specs=in_specs,
        out_specs=out_specs,
        out_shape=out_shape,
        compiler_params=_params(("parallel", "arbitrary")),
        name="rwkv_prep",
    )(proj_r, proj_r, proj_r, proj_r, proj_r, proj_r, proj_r, proj_r,
      shift, shift, shift, shift, lw['vec'], lw['mu_sm'], lw['w2'], lw['a2'], lw['g2'], lw['rep'])


def _rwkv_rec_body(nkk_ref, wr_ref, kka_ref, kmod_ref, pend_ref, v_ref, c_ref, s0_ref,
                   y_ref, sout_ref, s_ref, *, tc, n_grp, nch):
    j = pl.program_id(1)

    @pl.when(j == 0)
    def _():
        s_ref[...] = s0_ref[...]

    def row(ref, t, k):
        return jnp.broadcast_to(ref[t, pl.ds(k, 1), :], (SUBLANES, LANES))

    def reduce(t, tiles):
        acc = [[None, None] for _ in range(2 * n_grp)]
        for k in range(R_HEAD_DIM):
            nkk = row(nkk_ref, t, k)
            wr = row(wr_ref, t, k)
            for g, sg in enumerate(tiles(k)):
                for slot, term in ((g, sg * nkk), (n_grp + g, sg * wr)):
                    prev = acc[slot][k % 2]
                    acc[slot][k % 2] = term if prev is None else prev + term
        return tuple(a[0] + a[1] for a in acc)

    def step(t, carry):
        sa, yp = carry[:n_grp], carry[n_grp:]
        crow = c_ref[pl.ds(t, 1), :]
        c1 = crow[:, 0:LANES]
        c2 = crow[:, LANES:2 * LANES]
        vg = [v_ref[t, g * SUBLANES:(g + 1) * SUBLANES, :] for g in range(n_grp)]
        for g in range(n_grp):
            y_ref[t, g * SUBLANES:(g + 1) * SUBLANES, :] = yp[g] + sa[g] * c1 + vg[g] * c2

        def updated(k):
            kka = row(kka_ref, t, k)
            kmod = row(kmod_ref, t, k)
            new = [s_ref[g, k] + sa[g] * kka + vg[g] * kmod for g in range(n_grp)]
            for g in range(n_grp):
                s_ref[g, k] = new[g]
            return new

        return reduce(jnp.minimum(t + 1, tc - 1), updated)

    carry0 = reduce(0, lambda k: [s_ref[g, k] for g in range(n_grp)])
    lax.fori_loop(0, tc, step, carry0)

    for k in range(R_HEAD_DIM):
        p_end = row(pend_ref, SUBLANES - 1, k)
        for g in range(n_grp):
            s_ref[g, k] = s_ref[g, k] * p_end

    @pl.when(j == nch - 1)
    def _():
        sout_ref[...] = s_ref[...]


def rwkv_rec(nkk, wr, kka, kmod, pend, vmix, c, s0, s0_layer, n_seq, T):
    rows = n_seq * T
    n_grp = s0.shape[2]
    v_rows = n_grp * SUBLANES
    tc = min(SEQ_CHUNK, T)
    nch = T // tc
    k3 = lambda x: x.reshape(x.shape[0], R_HEAD_DIM, LANES)
    kspec = pl.BlockSpec((tc, R_HEAD_DIM, LANES), lambda s, j: (s * nch + j, 0, 0))
    pspec = pl.BlockSpec((SUBLANES, R_HEAD_DIM, LANES), lambda s, j: (s * nch + j, 0, 0))
    vspec = pl.BlockSpec((tc, v_rows, LANES), lambda s, j: (s * nch + j, 0, 0))
    tile = (n_grp, R_HEAD_DIM, SUBLANES, LANES)
    s_in = pl.BlockSpec((None, None) + tile, lambda s, j: (s0_layer, s, 0, 0, 0, 0))
    s_out = pl.BlockSpec((None,) + tile, lambda s, j: (s, 0, 0, 0, 0))
    body = functools.partial(_rwkv_rec_body, tc=tc, n_grp=n_grp, nch=nch)
    return pl.pallas_call(
        body,
        grid=(n_seq, nch),
        in_specs=[kspec] * 4 + [pspec, vspec, pl.BlockSpec((tc, 3 * LANES), lambda s, j: (s * nch + j, 0)), s_in],
        out_specs=[vspec, s_out],
        out_shape=[jax.ShapeDtypeStruct((rows, v_rows, LANES), F32),
                   jax.ShapeDtypeStruct(s0.shape[1:], F32)],
        scratch_shapes=[pltpu.VMEM(tile, F32)],
        compiler_params=_params(("parallel", "arbitrary")),
        name="rwkv_rec",
    )(k3(nkk), k3(wr), k3(kka), k3(kmod), k3(pend), vmix.reshape(rows, v_rows, LANES), c, s0)


def _rwkv_post_body(y_ref, v_ref, g_ref, c_ref, ln_ref, o_ref, *, n_grp, n_heads):
    def head_sum(x):
        return _tile_cols(_head_sum(x, n_heads), n_grp)

    y = y_ref[...]
    inv_n = 1.0 / R_HEAD_DIM
    mu = head_sum(y) * inv_n
    d = y - mu
    var = head_sum(d * d) * inv_n
    yln = d * lax.rsqrt(var + R_GN_EPS) * ln_ref[0:1, :] + ln_ref[1:2, :]
    c3 = _tile_cols(c_ref[:, 2 * LANES:3 * LANES], n_grp)
    o_ref[...] = ((yln + c3 * v_ref[...]) * g_ref[...]).astype(o_ref.dtype)


def rwkv_post(y, vmix, g, c, ln, layer, n_heads):
    M, RW = y.shape
    tm = _pick_tile(M, EW_ROWS)
    row = pl.BlockSpec((tm, RW), lambda i: (i, 0))
    body = functools.partial(_rwkv_post_body, n_grp=RW // LANES, n_heads=n_heads)
    return pl.pallas_call(
        body,
        grid=(M // tm,),
        in_specs=[row, row, row, pl.BlockSpec((tm, 3 * LANES), lambda i: (i, 0)),
                  pl.BlockSpec((None, 2, RW), lambda i: (layer, 0, 0))],
        out_specs=row,
        out_shape=jax.ShapeDtypeStruct((M, RW), BF16),
        compiler_params=_params(("parallel",)),
        name="rwkv_post",
    )(y, vmix, g, c, ln)


def _gla_body(q_ref, k_ref, v_ref, rg_ref, ag_ref, a2_ref, ab_ref, gn_ref, s0_ref,
              o_ref, sout_ref, st_ref, *, C, nch, n_heads, dk, dv):
    c = pl.program_id(1)

    @pl.when(c == 0)
    def _():
        st_ref[...] = s0_ref[...]

    x = jnp.dot(ag_ref[...], a2_ref[...], precision=HIGHEST, preferred_element_type=F32) + ab_ref[...]
    log_a = -_softplus(-x) * (1.0 / G_GATE_TAU)
    ti = lax.broadcasted_iota(jnp.int32, (C, C), 0)
    si = lax.broadcasted_iota(jnp.int32, (C, C), 1)
    tri = (si <= ti).astype(F32)
    b_all = jnp.dot(tri, log_a, precision=HIGHEST, preferred_element_type=F32)

    nt = (((1,), (1,)), ((), ()))
    tn = (((0,), (0,)), ((), ()))
    scale = float(dk) ** -0.5
    rowi = lax.broadcasted_iota(jnp.int32, (GLA_SUB, dk), 0)
    lane = lax.broadcasted_iota(jnp.int32, (GLA_SUB, LANES), 1)

    for h in range(n_heads):
        ksl = slice(h * dk, (h + 1) * dk)
        vsl = slice(h * dv, (h + 1) * dv)
        b = b_all[:, ksl]
        b_last = b[C - 1:C, :]
        q = q_ref[:, ksl] * scale
        k = k_ref[:, ksl]
        v = v_ref[:, vsl]
        vh = v.astype(BF16)
        st = st_ref[h]

        inter = lax.dot_general((q * jnp.exp(b)).astype(BF16), st.astype(BF16), nt,
                                preferred_element_type=F32)
        blocks = []
        for blk in range(C // GLA_SUB):
            t0 = blk * GLA_SUB
            rows = slice(t0, t0 + GLA_SUB)
            qb, kb, bb = q[rows], k[rows], b[rows]
            acc = inter[rows]
            if blk > 0:
                b0 = bb[0:1, :]
                qd = (qb * jnp.exp(bb - b0)).astype(BF16)
                kd = (k[0:t0] * jnp.exp(b0 - b[0:t0])).astype(BF16)
                sc = lax.dot_general(qd, kd, nt, preferred_element_type=F32)
                acc = acc + jnp.dot(sc.astype(BF16), vh[0:t0], preferred_element_type=F32)
            sc_d = jnp.zeros((GLA_SUB, LANES), F32)
            for s in range(GLA_SUB):
                e = jnp.where(rowi >= s, jnp.exp(bb - bb[s:s + 1, :]), 0.0)
                sc_s = jnp.sum(qb * e * kb[s:s + 1, :], axis=-1, keepdims=True)
                sc_d = jnp.where(lane == s, sc_s, sc_d)
            acc = acc + jnp.dot(sc_d[:, 0:GLA_SUB].astype(BF16), vh[rows], preferred_element_type=F32)
            blocks.append(acc)
        o = jnp.concatenate(blocks, axis=0) if len(blocks) > 1 else blocks[0]

        kdec = (k * jnp.exp(b_last - b)).astype(BF16)
        st_ref[h] = st * jnp.exp(b_last) + lax.dot_general(vh, kdec, tn, preferred_element_type=F32)

        ms = jnp.mean(o * o, axis=-1, keepdims=True)
        o_ref[:, vsl] = (o * lax.rsqrt(ms + NORM_EPS) * gn_ref[:, vsl]
                         * jax.nn.silu(rg_ref[:, vsl])).astype(o_ref.dtype)

    @pl.when(c == nch - 1)
    def _():
        sout_ref[...] = st_ref[...]


def gla(proj_g, proj_r, row_off, n_seq, T, s0t, s0_layer, lw, layer, o_ag):
    GK, GV = lw['gla_ab'].shape[-1], lw['gla_norm'].shape[-1]
    dk, dv = GK // G_HEADS, GV // G_HEADS
    C = min(SEQ_CHUNK, T)
    assert T % C == 0 and C % GLA_SUB == 0 and row_off % C == 0
    assert GV % GK == 0 and (2 * GK) % GV == 0 and o_ag % LANES == 0
    nch = T // C
    rows = n_seq * T
    off_b = row_off // C

    def tile(width, col0):
        assert col0 % width == 0
        return pl.BlockSpec((C, width), lambda s, c: (off_b + s * nch + c, col0 // width))

    def stacked(arr):
        return pl.BlockSpec((None,) + arr.shape[1:], lambda s, c: (layer,) + (0,) * (arr.ndim - 1))

    in_specs = [tile(GK, 0), tile(GK, GK), tile(GV, 2 * GK), tile(GV, 2 * GK + GV),
                tile(LANES, o_ag),
                stacked(lw['gla_a2']), stacked(lw['gla_ab']), stacked(lw['gla_norm']),
                pl.BlockSpec((None, None, G_HEADS, dv, dk), lambda s, c: (s0_layer, s, 0, 0, 0))]
    body = functools.partial(_gla_body, C=C, nch=nch, n_heads=G_HEADS, dk=dk, dv=dv)
    return pl.pallas_call(
        body,
        grid=(n_seq, nch),
        in_specs=in_specs,
        out_specs=[pl.BlockSpec((C, GV), lambda s, c: (s * nch + c, 0)),
                   pl.BlockSpec((None, G_HEADS, dv, dk), lambda s, c: (s, 0, 0, 0))],
        out_shape=[jax.ShapeDtypeStruct((rows, GV), BF16),
                   jax.ShapeDtypeStruct(s0t.shape[1:], F32)],
        scratch_shapes=[pltpu.VMEM((G_HEADS, dv, dk), F32)],
        compiler_params=_params(("parallel", "arbitrary")),
        name="gla",
    )(proj_g, proj_g, proj_g, proj_g, proj_r, lw['gla_a2'], lw['gla_ab'], lw['gla_norm'], s0t)


def _hx_to_xh(w, n_heads):
    sh = w.shape[:-1]
    x = w.shape[-1] // n_heads
    return jnp.swapaxes(w.reshape(*sh, n_heads, x), -1, -2).reshape(*sh, n_heads * x)


def _xh_to_hx(w, n_heads):
    sh = w.shape[:-1]
    x = w.shape[-1] // n_heads
    return jnp.swapaxes(w.reshape(*sh, x, n_heads), -1, -2).reshape(*sh, n_heads * x)


def _pad_last(x, width):
    return jnp.pad(x, [(0, 0)] * (x.ndim - 1) + [(0, width - x.shape[-1])])


def _split_r_cols(x, RW, ld, la, lg):
    offs = [0, RW, 2 * RW, 3 * RW, 3 * RW + ld, 3 * RW + ld + la, 3 * RW + ld + la + lg]
    return [x[..., offs[i]:offs[i + 1]] for i in range(6)]


def _small_block(wd, ad, gd, lgp, tail=None):
    parts = [wd, ad, _pad_last(gd, lgp)]
    if tail is not None:
        parts.append(tail)
    return _pad_last(jnp.concatenate(parts, axis=-1), SMALL_W)


def _r_cols_to_kernel(x, RW, ld, la, lg, lgp, n_heads, tail=None):
    x_r, x_k, x_v, x_wd, x_ad, x_gd = _split_r_cols(x, RW, ld, la, lg)
    return jnp.concatenate([_hx_to_xh(x_r, n_heads), _hx_to_xh(x_k, n_heads), _hx_to_xh(x_v, n_heads),
                            _small_block(x_wd, x_ad, x_gd, lgp, tail)], axis=-1)


def _r_cols_from_kernel(x, RW, ld, la, lg, n_heads):
    sm = x[..., 3 * RW:]
    return jnp.concatenate(
        [_xh_to_hx(x[..., 0:RW], n_heads), _xh_to_hx(x[..., RW:2 * RW], n_heads),
         _xh_to_hx(x[..., 2 * RW:3 * RW], n_heads),
         sm[..., 0:ld], sm[..., ld:ld + la], sm[..., ld + la:ld + la + lg]], axis=-1)


def _state_to_kernel(s, n_heads):
    lead = s.shape[:-3]
    n = len(lead)
    vv = LANES // n_heads
    g = R_HEAD_DIM // (SUBLANES * vv)
    x = s.reshape(*lead, n_heads, g, SUBLANES, vv, R_HEAD_DIM)
    x = x.transpose(*range(n), n + 1, n + 4, n + 2, n + 3, n)
    return x.reshape(*lead, g, R_HEAD_DIM, SUBLANES, LANES)


def _state_from_kernel(s, n_heads):
    lead = s.shape[:-4]
    n = len(lead)
    g = s.shape[-4]
    vv = LANES // n_heads
    x = s.reshape(*lead, g, R_HEAD_DIM, SUBLANES, vv, n_heads)
    x = x.transpose(*range(n), n + 4, n, n + 2, n + 3, n + 1)
    return x.reshape(*lead, n_heads, R_HEAD_DIM, R_HEAD_DIM)


def kernel(x_prompt, x_sample, state_shift, state_rwkv, state_gla, p_prompt, p_sample, norm_mix_pre, norm_mix_post, norm_ffn_pre, norm_ffn_post, norm_ple, w_in, rwkv_mu, rwkv_w0, rwkv_w2, rwkv_a0, rwkv_a2, rwkv_g2, rwkv_k_k, rwkv_k_a, rwkv_r_k, rwkv_ln_w, rwkv_ln_b, gla_a2, gla_ab, gla_norm, w_proj_rwkv, w_proj_gla, w_out, ffn_gate, ffn_up, ffn_down, ple_proj, ple_gate):
    Bp, Tp, D = x_prompt.shape
    Bs, Ts, _ = x_sample.shape
    L = w_in.shape[0]
    RW = rwkv_w0.shape[-1]
    H = RW // R_HEAD_DIM
    assert LANES % H == 0 and R_HEAD_DIM % (SUBLANES * (LANES // H)) == 0
    ld, la, lg = rwkv_w2.shape[1], rwkv_a2.shape[1], rwkv_g2.shape[1]
    lgp = _round_up(lg, LANES)
    GK, GV, GR = gla_a2.shape[-1], gla_norm.shape[-1], gla_a2.shape[1]
    FF = ffn_gate.shape[-1]
    Mp, Ms = Bp * Tp, Bs * Ts
    groups = [(0, Bp, Tp), (Mp, Bs, Ts)]

    NR = 3 * RW + SMALL_W
    NG = 2 * GK + 2 * GV
    o_ag = 3 * RW + ld + la + lgp
    assert o_ag + LANES <= NR and (3 * RW) % SMALL_W == 0 and GR <= LANES
    r_cols = 3 * RW + ld + la + lg
    g_cols = NG + GR
    to_kernel = functools.partial(_r_cols_to_kernel, RW=RW, ld=ld, la=la, lg=lg, lgp=lgp, n_heads=H)
    win_r = to_kernel(w_in[..., :r_cols], tail=w_in[..., r_cols + NG:r_cols + g_cols]).astype(BF16)
    win_g = w_in[..., r_cols:r_cols + NG].astype(BF16)
    win_gate = w_in[..., r_cols + g_cols:].astype(BF16)
    assert win_r.shape[-1] == NR and win_gate.shape[-1] == 2 * D
    assert NR % COL_TILE == 0 and NG % COL_TILE == 0 and (2 * D) % COL_TILE == 0
    win_r, win_g, win_gate = (_tile_major(w, COL_TILE) for w in (win_r, win_g, win_gate))

    mu_k = to_kernel(rwkv_mu)
    kq = LANES // H
    src = jnp.arange(LANES)
    dst = jnp.arange(kq * LANES)
    rep_mat = jnp.tile((src[:, None] == ((dst // LANES) * H + dst % H)[None, :]).astype(BF16), (3, 1))
    lw = dict(
        vec=jnp.stack([mu_k[:, 0:RW], mu_k[:, RW:2 * RW], mu_k[:, 2 * RW:3 * RW],
                       _hx_to_xh(rwkv_w0, H), _hx_to_xh(rwkv_a0, H), _hx_to_xh(rwkv_k_k, H),
                       _hx_to_xh(rwkv_k_a, H), _hx_to_xh(rwkv_r_k.reshape(L, RW), H)], axis=1),
        mu_sm=mu_k[:, None, 3 * RW:],
        w2=_hx_to_xh(rwkv_w2, H).astype(BF16), a2=_hx_to_xh(rwkv_a2, H).astype(BF16),
        g2=jnp.pad(_hx_to_xh(rwkv_g2, H), ((0, 0), (0, lgp - lg), (0, 0))).astype(BF16),
        rep=rep_mat, ld=ld, la=la, lgp=lgp,
        ln=jnp.stack([_hx_to_xh(rwkv_ln_w, H), _hx_to_xh(rwkv_ln_b, H)], axis=1),
        gla_a2=jnp.pad(gla_a2, ((0, 0), (0, LANES - GR), (0, 0))), gla_ab=gla_ab[:, None], gla_norm=gla_norm[:, None])
    gains = jnp.stack([norm_mix_pre, norm_mix_post, norm_ffn_pre, norm_ffn_post, norm_ple], axis=1)[:, :, None]
    w_pr = jnp.swapaxes(w_proj_rwkv.reshape(L, H, R_HEAD_DIM, D), 1, 2).reshape(L, RW, D).astype(BF16)
    w_pg, w_o, w_fg, w_fu, w_plg, w_plp = w_proj_gla, w_out, ffn_gate, ffn_up, ple_gate, ple_proj
    w_fd = _tile_major(ffn_down.astype(BF16), COL_TILE // 2)

    shift_states = [jnp.zeros((1, Bp, 1, NR), F32), to_kernel(state_shift)[:, :, None]]
    n_vgrp = R_HEAD_DIM // (SUBLANES * (LANES // H))
    rwkv_states = [jnp.zeros((1, Bp, n_vgrp, R_HEAD_DIM, SUBLANES, LANES), F32),
                   _state_to_kernel(state_rwkv, H)]
    gla_states = [jnp.zeros((1, Bp, G_HEADS, GV // G_HEADS, GK // G_HEADS), F32), jnp.swapaxes(state_gla, -1, -2)]

    h = jnp.concatenate([x_prompt.reshape(Mp, D), x_sample.reshape(Ms, D)], axis=0)
    p_all = jnp.concatenate([p_prompt.reshape(L, Mp, -1), p_sample.reshape(L, Ms, -1)], axis=1).astype(BF16)

    new_shift = [[], []]
    new_rwkv = [[], []]
    new_gla = [[], []]
    tn = _pick_tile(D, COL_TILE, LANES)
    for i in range(L):
        z = norm_cast(h, gains, i, G_MIX_PRE)
        proj_r = fused_mm([z], [(0, win_r)], i, [], _ep_identity, NR, F32, "in_proj_r")
        proj_g = fused_mm([z], [(0, win_g)], i, [], _ep_identity, NG, F32, "in_proj_g")
        proj_gate = fused_mm([z], [(0, win_gate)], i, [], _ep_identity, 2 * D, F32, "in_proj_gate")

        o_r_parts, o_g_parts = [], []
        for gi, (row_off, n_seq, T) in enumerate(groups):
            s_layer = i if gi == 1 else 0
            nkk, wr, kka, kmod, pend, c, vmix, g = rwkv_prep(
                proj_r, row_off, n_seq, T, shift_states[gi], s_layer, lw, i)
            y, s_new = rwkv_rec(nkk, wr, kka, kmod, pend, vmix, c, rwkv_states[gi], s_layer, n_seq, T)
            o_r_parts.append(rwkv_post(y.reshape(n_seq * T, RW), vmix, g, c, lw['ln'], i, H))
            o_g, s_gla_new = gla(proj_g, proj_r, row_off, n_seq, T, gla_states[gi], s_layer, lw, i, o_ag)
            o_g_parts.append(o_g)
            new_shift[gi].append(proj_r[row_off + T - 1:row_off + n_seq * T:T])
            new_rwkv[gi].append(s_new)
            new_gla[gi].append(s_gla_new)
        o_r = jnp.concatenate(o_r_parts, axis=0)
        o_g = jnp.concatenate(o_g_parts, axis=0)

        merged = fused_mm([o_r, o_g], [(0, w_pr), (1, w_pg)], i, [(proj_gate, 0), (proj_gate, D // tn)],
                          _ep_merge, D, BF16, "merge", tm_target=ROW_TILE // 2)
        mix = fused_mm([merged], [(0, w_o)], i, [], _ep_identity, D, F32, "out_proj")
        h1, u = resid_norm(h, mix, gains, i, G_MIX_POST, G_FFN_PRE)
        t = fused_mm([u], [(0, w_fg), (0, w_fu)], i, [], _ep_swiglu, FF, BF16, "ffn_up", tn=COL_TILE // 2)
        f = fused_mm([t], [(0, w_fd)], i, [], _ep_identity, D, F32, "ffn_down",
                     tm_target=ROW_TILE // 2, tn=COL_TILE // 2)
        h2, pn = resid_norm(h1, f, gains, i, G_FFN_POST, G_PLE)
        h = fused_mm([pn, p_all[i]], [(0, w_plg), (1, w_plp)], i, [(h2, 0)], _ep_ple, D, F32, "ple_mm",
                     tn=COL_TILE // 2)

    from_kernel = functools.partial(_r_cols_from_kernel, RW=RW, ld=ld, la=la, lg=lg, n_heads=H)
    y_prompt = h[:Mp].reshape(Bp, Tp, D)
    y_sample = h[Mp:].reshape(Bs, Ts, D)
    return (y_prompt, y_sample,
            from_kernel(jnp.stack(new_shift[0])), _state_from_kernel(jnp.stack(new_rwkv[0]), H),
            jnp.swapaxes(jnp.stack(new_gla[0]), -1, -2),
            from_kernel(jnp.stack(new_shift[1])), _state_from_kernel(jnp.stack(new_rwkv[1]), H),
            jnp.swapaxes(jnp.stack(new_gla[1]), -1, -2))
```
